```python
import jax
import jax.numpy as jnp
from jax import lax
import numpy as np

D_MODEL = 1024
BATCH = 8
SEQ = 2048
DEPTH = 2
DEC_BATCH = 128
DEC_SEQ = 1
PAST_LEN = 8192
PAGE_SIZE = 128

RET_HEADS = 8
RET_DK = 64
RET_DV = 64
RET_CHUNK = 128
MLA_HEADS = 8
MLA_NOPE = 64
MLA_ROPE = 32
MLA_V = 64
Q_LORA = 256
KV_LORA = 128
Q_BLOCK = 128
ROPE_THETA = 10000.0
MIX_OUT = RET_HEADS * RET_DV + MLA_HEADS * MLA_V
IN_WIDTH = 2 * RET_HEADS * RET_DK + 2 * RET_HEADS * RET_DV + Q_LORA + KV_LORA + MLA_ROPE
CONV_WIDTH = 31
N_GROUPS = 4
EXPERTS_PER_GROUP = 8
N_EXPERTS = N_GROUPS * EXPERTS_PER_GROUP
TOP_K_INNER = 2
D_EXPERT = 256
PLE_DIM = 256
EPS = 1e-6
N_ATTN_LAYERS = (DEPTH + 1) // 2
N_CONV_LAYERS = DEPTH // 2

kernel_name = 'hybrid_retention_mla_conformer_hmoe_step'


def rmsnorm(x, w):
    xf = x.astype(jnp.float32)
    y = xf * lax.rsqrt(jnp.mean(xf * xf, axis=-1, keepdims=True) + EPS)
    return (y * w.astype(jnp.float32)).astype(x.dtype)


def layernorm(x, w, b):
    xf = x.astype(jnp.float32)
    mu = jnp.mean(xf, axis=-1, keepdims=True)
    var = jnp.mean(jnp.square(xf - mu), axis=-1, keepdims=True)
    return ((xf - mu) * lax.rsqrt(var + EPS) * w.astype(jnp.float32) + b.astype(jnp.float32)).astype(x.dtype)


def rope(x, pos):
    d = x.shape[-1]
    half = d // 2
    inv = ROPE_THETA ** (-2.0 * jnp.arange(half, dtype=jnp.float32) / d)
    ang = pos.astype(jnp.float32)[:, None] * inv[None, :]
    cos = jnp.cos(ang)[None, :, None, :]
    sin = jnp.sin(ang)[None, :, None, :]
    x1 = x[..., :half].astype(jnp.float32)
    x2 = x[..., half:].astype(jnp.float32)
    return jnp.concatenate([x1 * cos - x2 * sin, x1 * sin + x2 * cos], axis=-1).astype(x.dtype)


def retention(q, k, v, s0, chunk):
    b, l, h, _ = q.shape
    dv = v.shape[-1]
    n = l // chunk
    log_g = jnp.log1p(-jnp.exp2(-5.0 - jnp.arange(h, dtype=jnp.float32)))
    idx = jnp.arange(chunk, dtype=jnp.float32)
    rel = idx[:, None] - idx[None, :]
    decay = jnp.where(rel[None] >= 0, jnp.exp(jnp.maximum(rel, 0.0)[None] * log_g[:, None, None]), 0.0)
    q_dec = jnp.exp((idx[:, None] + 1.0) * log_g[None, :])
    k_dec = jnp.exp((chunk - 1.0 - idx)[:, None] * log_g[None, :])
    c_dec = jnp.exp(chunk * log_g)

    def to_chunks(t):
        return t.astype(jnp.float32).reshape(b, n, chunk, h, t.shape[-1]).transpose(1, 0, 2, 3, 4)

    def step(s, inp):
        qb, kb, vb = inp
        sc = jnp.einsum('bihd,bjhd->bhij', qb, kb) * decay[None]
        o = (jnp.einsum('bhij,bjhv->bihv', sc, vb)
             + jnp.einsum('bihd,bhdv->bihv', qb * q_dec[None, :, :, None], s))
        s = s * c_dec[None, :, None, None] + jnp.einsum('bjhd,bjhv->bhdv', kb * k_dec[None, :, :, None], vb)
        return s, o

    s_fin, o = lax.scan(step, s0.astype(jnp.float32), (to_chunks(q), to_chunks(k), to_chunks(v)))
    o = o.transpose(1, 0, 2, 3, 4).reshape(b, l, h, dv)
    return o, s_fin


def mla_prompt_attend(q_lat, q_pe, c, k_pe):
    b, l, h, r = q_lat.shape
    nb = l // Q_BLOCK
    qlb = q_lat.reshape(b, nb, Q_BLOCK, h, r).transpose(1, 0, 2, 3, 4)
    qpb = q_pe.reshape(b, nb, Q_BLOCK, h, q_pe.shape[-1]).transpose(1, 0, 2, 3, 4)
    starts = jnp.arange(nb) * Q_BLOCK
    key_pos = jnp.arange(l)

    def block(args):
        ql, qp, s = args
        sc = (jnp.einsum('bqhc,bkc->bhqk', ql, c) + jnp.einsum('bqhr,bkr->bhqk', qp, k_pe)).astype(jnp.float32)
        mask = key_pos[None, :] <= (s + jnp.arange(Q_BLOCK))[:, None]
        p = jax.nn.softmax(jnp.where(mask, sc, -jnp.inf), axis=-1).astype(c.dtype)
        return jnp.einsum('bhqk,bkc->bqhc', p, c)

    o = lax.map(block, (qlb, qpb, starts))
    return o.transpose(1, 0, 2, 3, 4).reshape(b, l, h, r)


def mla_decode_attend(q_lat, q_pe, c_new, kpe_new, c_past, kpe_past):
    l = q_lat.shape[1]
    past = c_past.shape[1]
    sc_past = jnp.einsum('bqhc,bkc->bhqk', q_lat, c_past) + jnp.einsum('bqhr,bkr->bhqk', q_pe, kpe_past)
    sc_new = jnp.einsum('bqhc,bkc->bhqk', q_lat, c_new) + jnp.einsum('bqhr,bkr->bhqk', q_pe, kpe_new)
    causal = jnp.tril(jnp.ones((l, l), dtype=bool))
    sc_new = jnp.where(causal, sc_new.astype(jnp.float32), -jnp.inf)
    p = jax.nn.softmax(jnp.concatenate([sc_past.astype(jnp.float32), sc_new], axis=-1), axis=-1).astype(c_new.dtype)
    return (jnp.einsum('bhqk,bkc->bqhc', p[..., :past], c_past)
            + jnp.einsum('bhqk,bkc->bqhc', p[..., past:], c_new))


def mixer_ret_mla(hn, pos, w_in, ret_gn_w, q_norm_w, kv_norm_w, w_uq, w_ukv, w_out, s0, past):
    b, l, _ = hn.shape
    u = hn @ w_in
    sizes = (RET_HEADS * RET_DK, RET_HEADS * RET_DK, RET_HEADS * RET_DV, RET_HEADS * RET_DV, Q_LORA, KV_LORA, MLA_ROPE)
    cuts = [sum(sizes[:i + 1]) for i in range(len(sizes) - 1)]
    rq, rk, rv, rg, cq, ckv, kpe = jnp.split(u, cuts, axis=-1)
    rq = rope(rq.reshape(b, l, RET_HEADS, RET_DK), pos)
    rk = rope(rk.reshape(b, l, RET_HEADS, RET_DK), pos) * (RET_DK ** -0.5)
    rv = rv.reshape(b, l, RET_HEADS, RET_DV)
    chunk = RET_CHUNK if l % RET_CHUNK == 0 else l
    o_r, s_new = retention(rq, rk, rv, s0, chunk)
    mu = jnp.mean(o_r, axis=-1, keepdims=True)
    var = jnp.mean(jnp.square(o_r - mu), axis=-1, keepdims=True)
    o_r = ((o_r - mu) * lax.rsqrt(var + EPS)).reshape(b, l, RET_HEADS * RET_DV) * ret_gn_w.astype(jnp.float32)
    o_r = (o_r * jax.nn.silu(rg.astype(jnp.float32))).astype(hn.dtype)
    scale = (MLA_NOPE + MLA_ROPE) ** -0.5
    cq = rmsnorm(cq, q_norm_w)
    q = jnp.einsum('blq,qhd->blhd', cq, w_uq)
    q_nope, q_pe = q[..., :MLA_NOPE], q[..., MLA_NOPE:]
    q_pe = rope(q_pe, pos) * scale
    c = rmsnorm(ckv, kv_norm_w)
    k_pe = rope(kpe[:, :, None, :], pos)[:, :, 0, :]
    q_lat = jnp.einsum('blhn,chn->blhc', q_nope, w_ukv[..., :MLA_NOPE]) * scale
    if past is None:
        o_lat = mla_prompt_attend(q_lat, q_pe, c, k_pe)
    else:
        o_lat = mla_decode_attend(q_lat, q_pe, c, k_pe, past[0], past[1])
    o_m = jnp.einsum('blhc,chv->blhv', o_lat, w_ukv[..., MLA_NOPE:]).reshape(b, l, MLA_HEADS * MLA_V)
    out = jnp.concatenate([o_r, o_m], axis=-1) @ w_out
    return out, s_new.astype(s0.dtype), c, k_pe


def mixer_conv(hn, pw1, pw1_b, dw, dw_b, ln_w, ln_b, pw2, pw2_b, buf0):
    d = hn.shape[-1]
    u = hn @ pw1 + pw1_b
    a = u[..., :d] * jax.nn.sigmoid(u[..., d:])
    full = jnp.concatenate([buf0.astype(a.dtype), a], axis=1)
    y = lax.conv_general_dilated(full, dw[:, None, :].astype(a.dtype), window_strides=(1,), padding='VALID',
                                 dimension_numbers=('NWC', 'WIO', 'NWC'), feature_group_count=d) + dw_b
    y = jax.nn.silu(layernorm(y, ln_w, ln_b))
    out = y @ pw2 + pw2_b
    return out, full[:, -(CONV_WIDTH - 1):, :]


def hier_moe(x, w_group, b_group, w_router, b_router, w_gate, w_up, w_down):
    b, l, d = x.shape
    xf = x.reshape(b * l, d)
    g_logits = (xf @ w_group + b_group).astype(jnp.float32)
    g_prob = jax.nn.softmax(g_logits, axis=-1)
    g_sel = jnp.argmax(g_logits, axis=-1)
    g_w = jnp.take_along_axis(g_prob, g_sel[:, None], axis=1)[:, 0]
    e_logits = (jnp.einsum('nd,dge->nge', xf, w_router) + b_router).astype(jnp.float32)
    e_sel = jnp.take_along_axis(e_logits, g_sel[:, None, None], axis=1)[:, 0]
    top_v, top_i = lax.top_k(e_sel, TOP_K_INNER)
    top_w = jax.nn.softmax(top_v, axis=-1) * g_w[:, None]
    expert_id = g_sel[:, None] * EXPERTS_PER_GROUP + top_i
    combine = jnp.sum(jax.nn.one_hot(expert_id, N_EXPERTS, dtype=jnp.float32) * top_w[..., None], axis=1)
    y = jnp.zeros_like(xf)
    for e in range(N_EXPERTS):
        h = jax.nn.silu(xf @ w_gate[e]) * (xf @ w_up[e])
        y = y + combine[:, e:e + 1].astype(xf.dtype) * (h @ w_down[e])
    return y.reshape(b, l, d)


def trunk(x, p, pos, W, ret_s0, conv_buf0, mla_past):
    b, l, d = x.shape
    h = x
    lat_rows, rope_rows, ret_states, conv_bufs = [], [], [], []
    for i in range(DEPTH):
        hn = rmsnorm(h, W['norm_mix'][i])
        if i % 2 == 0:
            a = i // 2
            if mla_past is None:
                s0 = jnp.zeros((b, RET_HEADS, RET_DK, RET_DV), x.dtype)
                past = None
            else:
                s0 = ret_s0[a]
                lat, rp, pt = mla_past
                nseq, npg = pt.shape
                past = (lat[a, pt].reshape(nseq, npg * lat.shape[2], KV_LORA),
                        rp[a, pt].reshape(nseq, npg * rp.shape[2], MLA_ROPE))
            out, s_new, c, k_pe = mixer_ret_mla(hn, pos, W['w_in'][a], W['ret_gn_w'][a], W['mla_q_norm'][a],
                                                W['mla_kv_norm'][a], W['w_uq'][a], W['w_ukv'][a], W['w_out'][a],
                                                s0, past)
            lat_rows.append(c)
            rope_rows.append(k_pe)
            ret_states.append(s_new)
        else:
            ci = i // 2
            buf0 = jnp.zeros((b, CONV_WIDTH - 1, d), x.dtype) if conv_buf0 is None else conv_buf0[ci]
            out, buf = mixer_conv(hn, W['conv_pw1'][ci], W['conv_pw1_b'][ci], W['conv_dw'][ci], W['conv_dw_b'][ci],
                                  W['conv_ln_w'][ci], W['conv_ln_b'][ci], W['conv_pw2'][ci], W['conv_pw2_b'][ci], buf0)
            conv_bufs.append(buf)
        h = h + out
        h = h + hier_moe(rmsnorm(h, W['norm_ffn'][i]), W['moe_w_group'][i], W['moe_b_group'][i],
                         W['moe_w_router'][i], W['moe_b_router'][i], W['moe_w_gate'][i], W['moe_w_up'][i],
                         W['moe_w_down'][i])
        h = h + (p[i] @ W['ple_proj'][i]) * jax.nn.sigmoid(rmsnorm(h, W['ple_norm'][i]) @ W['ple_gate'][i])
    y = rmsnorm(h, W['norm_final'])
    return y, jnp.stack(lat_rows), jnp.stack(rope_rows), jnp.stack(ret_states), jnp.stack(conv_bufs)


def setup_inputs(seed: int = 0) -> dict:
    key = jax.random.key(seed)
    ks = jax.random.split(key, 64)
    it = iter(range(64))

    def nrm(shape, scale):
        return jax.random.normal(ks[next(it)], shape, jnp.float32) * scale

    def gain(shape):
        return 1.0 + 0.05 * jax.random.normal(ks[next(it)], shape, jnp.float32)

    n_pages = PAST_LEN // PAGE_SIZE
    n_pool = (DEC_BATCH * n_pages * 5) // 4
    d = D_MODEL
    page_table = jax.random.permutation(ks[next(it)], n_pool)[:DEC_BATCH * n_pages].reshape(DEC_BATCH, n_pages).astype(jnp.int32)
    return {
        'x_prompt': nrm((BATCH, SEQ, d), 1.0),
        'x_sample': nrm((DEC_BATCH, DEC_SEQ, d), 1.0),
        'p_prompt': nrm((DEPTH, BATCH, SEQ, PLE_DIM), 1.0),
        'p_sample': nrm((DEPTH, DEC_BATCH, DEC_SEQ, PLE_DIM), 1.0),
        'cache_mla_latent': nrm((N_ATTN_LAYERS, n_pool, PAGE_SIZE, KV_LORA), 1.0),
        'cache_mla_rope': nrm((N_ATTN_LAYERS, n_pool, PAGE_SIZE, MLA_ROPE), 1.0),
        'state_retention': nrm((N_ATTN_LAYERS, DEC_BATCH, RET_HEADS, RET_DK, RET_DV), 0.3),
        'state_conv': nrm((N_CONV_LAYERS, DEC_BATCH, CONV_WIDTH - 1, d), 0.5),
        'page_table': page_table,
        'norm_mix': gain((DEPTH, d)),
        'norm_ffn': gain((DEPTH, d)),
        'norm_final': gain((d,)),
        'w_in': nrm((N_ATTN_LAYERS, d, IN_WIDTH), d ** -0.5),
        'ret_gn_w': gain((N_ATTN_LAYERS, RET_HEADS * RET_DV)),
        'mla_q_norm': gain((N_ATTN_LAYERS, Q_LORA)),
        'mla_kv_norm': gain((N_ATTN_LAYERS, KV_LORA)),
        'w_uq': nrm((N_ATTN_LAYERS, Q_LORA, MLA_HEADS, MLA_NOPE + MLA_ROPE), Q_LORA ** -0.5),
        'w_ukv': nrm((N_ATTN_LAYERS, KV_LORA, MLA_HEADS, MLA_NOPE + MLA_V), KV_LORA ** -0.5),
        'w_out': nrm((N_ATTN_LAYERS, MIX_OUT, d), MIX_OUT ** -0.5),
        'conv_pw1': nrm((N_CONV_LAYERS, d, 2 * d), d ** -0.5),
        'conv_pw1_b': nrm((N_CONV_LAYERS, 2 * d), 0.02),
        'conv_dw': nrm((N_CONV_LAYERS, CONV_WIDTH, d), CONV_WIDTH ** -0.5),
        'conv_dw_b': nrm((N_CONV_LAYERS, d), 0.02),
        'conv_ln_w': gain((N_CONV_LAYERS, d)),
        'conv_ln_b': nrm((N_CONV_LAYERS, d), 0.02),
        'conv_pw2': nrm((N_CONV_LAYERS, d, d), d ** -0.5),
        'conv_pw2_b': nrm((N_CONV_LAYERS, d), 0.02),
        'moe_w_group': nrm((DEPTH, d, N_GROUPS), d ** -0.5),
        'moe_b_group': nrm((DEPTH, N_GROUPS), 0.01),
        'moe_w_router': nrm((DEPTH, d, N_GROUPS, EXPERTS_PER_GROUP), d ** -0.5),
        'moe_b_router': nrm((DEPTH, N_GROUPS, EXPERTS_PER_GROUP), 0.01),
        'moe_w_gate': nrm((DEPTH, N_EXPERTS, d, D_EXPERT), d ** -0.5),
        'moe_w_up': nrm((DEPTH, N_EXPERTS, d, D_EXPERT), d ** -0.5),
        'moe_w_down': nrm((DEPTH, N_EXPERTS, D_EXPERT, d), D_EXPERT ** -0.5),
        'ple_proj': nrm((DEPTH, PLE_DIM, d), PLE_DIM ** -0.5),
        'ple_norm': gain((DEPTH, d)),
        'ple_gate': nrm((DEPTH, d, d), d ** -0.5),
    }


def reference(x_prompt, x_sample, p_prompt, p_sample, cache_mla_latent, cache_mla_rope, state_retention, state_conv,
              page_table, norm_mix, norm_ffn, norm_final, w_in, ret_gn_w, mla_q_norm, mla_kv_norm, w_uq, w_ukv, w_out,
              conv_pw1, conv_pw1_b, conv_dw, conv_dw_b, conv_ln_w, conv_ln_b, conv_pw2, conv_pw2_b,
              moe_w_group, moe_b_group, moe_w_router, moe_b_router, moe_w_gate, moe_w_up, moe_w_down,
              ple_proj, ple_norm, ple_gate):
    W = dict(norm_mix=norm_mix, norm_ffn=norm_ffn, norm_final=norm_final, w_in=w_in, ret_gn_w=ret_gn_w,
             mla_q_norm=mla_q_norm, mla_kv_norm=mla_kv_norm, w_uq=w_uq, w_ukv=w_ukv, w_out=w_out,
             conv_pw1=conv_pw1, conv_pw1_b=conv_pw1_b, conv_dw=conv_dw, conv_dw_b=conv_dw_b, conv_ln_w=conv_ln_w,
             conv_ln_b=conv_ln_b, conv_pw2=conv_pw2, conv_pw2_b=conv_pw2_b, moe_w_group=moe_w_group,
             moe_b_group=moe_b_group, moe_w_router=moe_w_router, moe_b_router=moe_b_router, moe_w_gate=moe_w_gate,
             moe_w_up=moe_w_up, moe_w_down=moe_w_down, ple_proj=ple_proj, ple_norm=ple_norm, ple_gate=ple_gate)
    past_len = page_table.shape[1] * cache_mla_latent.shape[2]
    pos_prompt = jnp.arange(x_prompt.shape[1])
    pos_sample = past_len + jnp.arange(x_sample.shape[1])
    y_prompt, lat_p, rope_p, ret_p, conv_p = trunk(x_prompt, p_prompt, pos_prompt, W, None, None, None)
    y_sample, lat_s, rope_s, ret_s, conv_s = trunk(x_sample, p_sample, pos_sample, W, state_retention, state_conv,
                                                   (cache_mla_latent, cache_mla_rope, page_table))
    return (y_prompt, y_sample, lat_p, rope_p, ret_p, conv_p, lat_s, rope_s, ret_s, conv_s)
```

```python
import functools

import jax
import jax.numpy as jnp
from jax import lax
from jax.experimental import pallas as pl
from jax.experimental.pallas import tpu as pltpu

F32 = jnp.float32
BF16 = jnp.bfloat16
I32 = jnp.int32

EPS = 1e-6
ROPE_THETA = 10000.0
RET_HEADS = 8
RET_DK = 64
RET_DV = 64
RET_CHUNK = 128
MLA_HEADS = 8
MLA_NOPE = 64
MLA_ROPE = 32
MLA_V = 64
Q_LORA = 256
KV_LORA = 128
CONV_WIDTH = 31
N_GROUPS = 4
EXPERTS_PER_GROUP = 8
N_EXPERTS = N_GROUPS * EXPERTS_PER_GROUP
LANES = 128
Q_BLOCK = 128
KV_BLOCK = 256
VMEM_LIMIT_BYTES = 56 * 1024 * 1024


def _params(*sem):
    return pltpu.CompilerParams(dimension_semantics=sem, vmem_limit_bytes=VMEM_LIMIT_BYTES)


def _rms(x, w):
    return x * lax.rsqrt(jnp.mean(x * x, axis=-1, keepdims=True) + EPS) * w


def _sigmoid(x):
    return 1.0 / (1.0 + jnp.exp(-x))


def _dot(a, b):
    return jnp.dot(a, b, preferred_element_type=F32)


def _dot_nt(a, b):
    return lax.dot_general(a, b, (((1,), (1,)), ((), ())), preferred_element_type=F32)


def _dot_tn(a, b):
    return lax.dot_general(a, b, (((0,), (0,)), ((), ())), preferred_element_type=F32)


def _rope(x, cos_t, sin_t, period):
    half = period // 2
    lane = lax.broadcasted_iota(I32, x.shape, 1)
    first = (lane % period) < half
    rot = jnp.where(first, pltpu.roll(x, LANES - half, 1), pltpu.roll(x, half, 1))
    return x * cos_t + rot * sin_t


def _rope_tables(pos, period):
    half = period // 2
    inv = ROPE_THETA ** (-2.0 * jnp.arange(half, dtype=F32) / period)
    ang = pos.astype(F32)[:, None] * inv[None, :]
    cos = jnp.cos(ang)
    sin = jnp.sin(ang)
    cos_p = jnp.concatenate([cos, cos], axis=1)
    sin_p = jnp.concatenate([-sin, sin], axis=1)
    reps = LANES // period
    return jnp.tile(cos_p, (1, reps)), jnp.tile(sin_p, (1, reps))


def _proj_in_kernel(x_ref, nw_ref, win_ref, cr_ref, sr_ref, cp_ref, sp_ref, qn_ref, kvn_ref,
                    wuqn_ref, wuqp_ref, wk_ref,
                    rq_ref, rk_ref, rv_ref, rg_ref, qlat_ref, qpe_ref, c_ref, kpe_ref, kcat_ref):
    hn = _rms(x_ref[...], nw_ref[...]).astype(BF16)
    cr, sr, cp, sp = cr_ref[...], sr_ref[...], cp_ref[...], sp_ref[...]
    hk = RET_HEADS * RET_DK
    hv = RET_HEADS * RET_DV

    def mm(lo, width):
        return _dot(hn, win_ref[:, lo:lo + width])

    u = mm(0, hk)
    rq_ref[...] = jnp.concatenate(
        [_rope(u[:, g:g + LANES], cr, sr, RET_DK) for g in range(0, hk, LANES)], axis=1).astype(BF16)
    u = mm(hk, hk)
    rk_ref[...] = jnp.concatenate(
        [_rope(u[:, g:g + LANES], cr, sr, RET_DK) * (RET_DK ** -0.5) for g in range(0, hk, LANES)],
        axis=1).astype(BF16)
    rv_ref[...] = mm(2 * hk, hv).astype(BF16)
    rg_ref[...] = mm(2 * hk + hv, hv)
    off = 2 * hk + 2 * hv
    cqn = _rms(mm(off, Q_LORA), qn_ref[...]).astype(BF16)
    c = _rms(mm(off + Q_LORA, KV_LORA), kvn_ref[...])
    c_ref[...] = c
    kp = _rope(mm(off + Q_LORA + KV_LORA, LANES), cp, sp, MLA_ROPE)
    kpe_ref[...] = kp[:, :MLA_ROPE]
    kcat_ref[...] = jnp.concatenate([c, kp], axis=1).astype(BF16)
    scale = (MLA_NOPE + MLA_ROPE) ** -0.5
    qn = _dot(cqn, wuqn_ref[...])
    for h in range(MLA_HEADS):
        ql = _dot(qn[:, h * LANES:(h + 1) * LANES].astype(BF16), wk_ref[h]) * scale
        qlat_ref[:, h * KV_LORA:(h + 1) * KV_LORA] = ql.astype(BF16)
    qp = _dot(cqn, wuqp_ref[...])
    qpe_ref[...] = jnp.concatenate(
        [_rope(qp[:, g:g + LANES], cp, sp, MLA_ROPE) * scale
         for g in range(0, MLA_HEADS * MLA_ROPE, LANES)], axis=1).astype(BF16)


def _proj_in(x2, nw, win, tabs, qn, kvn, wuqn, wuqp, wk, tm):
    n, d = x2.shape
    cr, sr, cp, sp = tabs
    ltab = cr.shape[0]
    nblk = ltab // tm if ltab >= tm else 0
    tab_rows = tm if nblk else ltab
    tab_map = (lambda i: (i % nblk, 0)) if nblk else (lambda i: (0, 0))
    row = lambda i: (i, 0)
    const = lambda i: (0, 0)
    hk = RET_HEADS * RET_DK
    hv = RET_HEADS * RET_DV
    tab_spec = pl.BlockSpec((tab_rows, LANES), tab_map)
    out_cols = [(hk, BF16), (hk, BF16), (hv, BF16), (hv, F32), (MLA_HEADS * KV_LORA, BF16),
                (MLA_HEADS * MLA_ROPE, BF16), (KV_LORA, F32), (MLA_ROPE, F32), (2 * LANES, BF16)]
    return pl.pallas_call(
        _proj_in_kernel,
        grid=(n // tm,),
        in_specs=[pl.BlockSpec((tm, d), row), pl.BlockSpec((1, d), const),
                  pl.BlockSpec(win.shape, const), tab_spec, tab_spec, tab_spec, tab_spec,
                  pl.BlockSpec((1, Q_LORA), const), pl.BlockSpec((1, KV_LORA), const),
                  pl.BlockSpec(wuqn.shape, const), pl.BlockSpec(wuqp.shape, const),
                  pl.BlockSpec(wk.shape, lambda i: (0, 0, 0))],
        out_specs=[pl.BlockSpec((tm, w), row) for w, _ in out_cols],
        out_shape=[jax.ShapeDtypeStruct((n, w), dt) for w, dt in out_cols],
        compiler_params=_params("parallel"),
        name="proj_in",
    )(x2, nw, win, cr, sr, cp, sp, qn, kvn, wuqn, wuqp, wk)


def _ret_kernel(rq_ref, rk_ref, rv_ref, rg_ref, gnw_ref, decay_ref, qdec_ref, kdec_ref, cdec_ref,
                o_ref, s_out_ref, s_scr):
    ci = pl.program_id(1)

    @pl.when(ci == 0)
    def _():
        s_scr[...] = jnp.zeros_like(s_scr)

    outs = []
    for h in range(RET_HEADS):
        sl = slice(h * RET_DK, (h + 1) * RET_DK)
        q = rq_ref[:, sl]
        k = rk_ref[:, sl]
        v = rv_ref[:, sl]
        s = s_scr[h]
        sc = _dot_nt(q, k) * decay_ref[h]
        o = _dot(sc.astype(BF16), v) + _dot(q, s.astype(BF16)) * qdec_ref[:, sl]
        kd = (k.astype(F32) * kdec_ref[:, sl]).astype(BF16)
        s_scr[h] = s * cdec_ref[:, sl] + _dot_tn(kd, v)
        mu = jnp.mean(o, axis=-1, keepdims=True)
        var = jnp.mean(jnp.square(o - mu), axis=-1, keepdims=True)
        outs.append((o - mu) * lax.rsqrt(var + EPS))
    g = rg_ref[...]
    o_ref[...] = (jnp.concatenate(outs, axis=1) * gnw_ref[...] * (g * _sigmoid(g))).astype(BF16)

    @pl.when(ci == pl.num_programs(1) - 1)
    def _():
        s_out_ref[...] = s_scr[...]


def _ret_tables(chunk):
    h = RET_HEADS
    log_g = jnp.log1p(-jnp.exp2(-5.0 - jnp.arange(h, dtype=F32)))
    idx = jnp.arange(chunk, dtype=F32)
    rel = idx[:, None] - idx[None, :]
    decay = jnp.where(rel[None] >= 0, jnp.exp(jnp.maximum(rel, 0.0)[None] * log_g[:, None, None]), 0.0)
    q_dec = jnp.exp((idx[:, None] + 1.0) * log_g[None, :])
    k_dec = jnp.exp((chunk - 1.0 - idx)[:, None] * log_g[None, :])
    c_dec = jnp.exp(chunk * log_g)
    rep = lambda t: jnp.repeat(t, RET_DK, axis=-1)
    return decay, rep(q_dec), rep(k_dec), rep(c_dec[None, :])


def _retention_prompt(rq, rk, rv, rg, gnw, b, l):
    chunk = RET_CHUNK if l % RET_CHUNK == 0 else l
    decay, qdec, kdec, cdec = _ret_tables(chunk)
    hk = RET_HEADS * RET_DK
    n_chunks = l // chunk
    row = lambda bi, ci: (bi * n_chunks + ci, 0)
    const = lambda bi, ci: (0, 0)
    return pl.pallas_call(
        _ret_kernel,
        grid=(b, n_chunks),
        in_specs=[pl.BlockSpec((chunk, hk), row)] * 4 + [
            pl.BlockSpec((1, hk), const),
            pl.BlockSpec(decay.shape, lambda bi, ci: (0, 0, 0)),
            pl.BlockSpec((chunk, hk), const), pl.BlockSpec((chunk, hk), const),
            pl.BlockSpec((1, hk), const)],
        out_specs=[pl.BlockSpec((chunk, hk), row),
                   pl.BlockSpec((None, RET_HEADS, RET_DK, RET_DV), lambda bi, ci: (bi, 0, 0, 0))],
        out_shape=[jax.ShapeDtypeStruct((b * l, hk), BF16),
                   jax.ShapeDtypeStruct((b, RET_HEADS, RET_DK, RET_DV), F32)],
        scratch_shapes=[pltpu.VMEM((RET_HEADS, RET_DK, RET_DV), F32)],
        compiler_params=_params("parallel", "arbitrary"),
        name="retention_prompt",
    )(rq, rk, rv, rg, gnw, decay, qdec, kdec, cdec)


def _ret_decode_kernel(qc_ref, kc_ref, qr_ref, kr_ref, v_ref, g_ref, gnw_ref, gam_ref, s_ref,
                       o_ref, s_out_ref):
    s = s_ref[...]
    gam = gam_ref[...]
    v = v_ref[...]
    qk = jnp.sum(qr_ref[...] * kr_ref[...], axis=-1, keepdims=True)
    cross = jnp.sum(qc_ref[...] * gam * s, axis=-2, keepdims=True)
    o = qk * v + cross
    s_out_ref[...] = s * gam + kc_ref[...] * v
    mu = jnp.mean(o, axis=-1, keepdims=True)
    var = jnp.mean(jnp.square(o - mu), axis=-1, keepdims=True)
    g = g_ref[...]
    o_ref[...] = (o - mu) * lax.rsqrt(var + EPS) * gnw_ref[...] * (g * _sigmoid(g))


def _retention_decode(rq, rk, rv, rg, gnw, s0, bb=8):
    b = rq.shape[0]
    h, dk, dv = RET_HEADS, RET_DK, RET_DV
    gam = jnp.exp(jnp.log1p(-jnp.exp2(-5.0 - jnp.arange(h, dtype=F32))))
    gam4 = jnp.broadcast_to(gam[None, :, None, None], (1, h, 1, dv))
    col = lambda t: t.astype(F32).reshape(b, h, dk, 1)
    rowv = lambda t, w: t.astype(F32).reshape(b, h, 1, w)
    cspec = pl.BlockSpec((bb, h, dk, 1), lambda i: (i, 0, 0, 0))
    rspec = pl.BlockSpec((bb, h, 1, dk), lambda i: (i, 0, 0, 0))
    vspec = pl.BlockSpec((bb, h, 1, dv), lambda i: (i, 0, 0, 0))
    sspec = pl.BlockSpec((bb, h, dk, dv), lambda i: (i, 0, 0, 0))
    one = pl.BlockSpec((1, h, 1, dv), lambda i: (0, 0, 0, 0))
    o, s_new = pl.pallas_call(
        _ret_decode_kernel,
        grid=(b // bb,),
        in_specs=[cspec, cspec, rspec, rspec, vspec, vspec, one, one, sspec],
        out_specs=[vspec, sspec],
        out_shape=[jax.ShapeDtypeStruct((b, h, 1, dv), F32),
                   jax.ShapeDtypeStruct((b, h, dk, dv), F32)],
        compiler_params=_params("parallel"),
        name="retention_decode",
    )(col(rq), col(rk), rowv(rq, dk), rowv(rk, dk), rowv(rv, dv), rowv(rg, dv),
      gnw.reshape(1, h, 1, dv), gam4, s0)
    return o.reshape(b, h * dv), s_new


def _attn_kernel(qlat_ref, qpe_ref, kcat_ref, wv_ref, o_ref, m_scr, l_scr, acc_scr):
    i = pl.program_id(1)
    lane = lax.broadcasted_iota(I32, (Q_BLOCK, LANES), 1)
    per_group = LANES // MLA_ROPE
    parts = []
    for h in range(MLA_HEADS):
        g, r = divmod(h, per_group)
        pe = qpe_ref[:, g * LANES:(g + 1) * LANES]
        keep = (lane >= r * MLA_ROPE) & (lane < (r + 1) * MLA_ROPE)
        pe = jnp.where(keep, pe, jnp.zeros_like(pe))
        parts.append(jnp.concatenate([qlat_ref[:, h * KV_LORA:(h + 1) * KV_LORA], pe], axis=1))
    q = jnp.concatenate(parts, axis=0)
    rows = MLA_HEADS * Q_BLOCK
    m_scr[...] = jnp.full_like(m_scr, -jnp.inf)
    l_scr[...] = jnp.zeros_like(l_scr)
    acc_scr[...] = jnp.zeros_like(acc_scr)

    def block(j, masked):
        k0 = pl.multiple_of(j * KV_BLOCK, KV_BLOCK)
        kc = kcat_ref[pl.ds(k0, KV_BLOCK), :]
        s = _dot_nt(q, kc)
        if masked:
            qpos = i * Q_BLOCK + lax.broadcasted_iota(I32, (rows, KV_BLOCK), 0) % Q_BLOCK
            kpos = k0 + lax.broadcasted_iota(I32, (rows, KV_BLOCK), 1)
            s = jnp.where(kpos <= qpos, s, -jnp.inf)
        m_old = m_scr[...]
        m_new = jnp.maximum(m_old, jnp.max(s, axis=-1, keepdims=True))
        alpha = jnp.exp(m_old - m_new)
        p = jnp.exp(s - m_new)
        l_scr[...] = alpha * l_scr[...] + jnp.sum(p, axis=-1, keepdims=True)
        acc_scr[...] = alpha * acc_scr[...] + _dot(p.astype(BF16), kc[:, :KV_LORA])
        m_scr[...] = m_new

    n_full = (i * Q_BLOCK) // KV_BLOCK

    def body(j, carry):
        block(j, False)
        return carry

    lax.fori_loop(0, n_full, body, 0)
    block(n_full, True)
    o = acc_scr[...] / l_scr[...]
    outs = [_dot(o[h * Q_BLOCK:(h + 1) * Q_BLOCK].astype(BF16), wv_ref[h]) for h in range(MLA_HEADS)]
    o_ref[...] = jnp.concatenate(outs, axis=1).astype(BF16)


def _attention_prompt(qlat, qpe, kcat, wv, b, l):
    nq = l // Q_BLOCK
    rows = MLA_HEADS * Q_BLOCK
    qmap = lambda bi, i: (bi * nq + i, 0)
    return pl.pallas_call(
        _attn_kernel,
        grid=(b, nq),
        in_specs=[pl.BlockSpec((Q_BLOCK, qlat.shape[1]), qmap),
                  pl.BlockSpec((Q_BLOCK, qpe.shape[1]), qmap),
                  pl.BlockSpec((l, kcat.shape[1]), lambda bi, i: (bi, 0)),
                  pl.BlockSpec(wv.shape, lambda bi, i: (0, 0, 0))],
        out_specs=pl.BlockSpec((Q_BLOCK, MLA_HEADS * MLA_V), qmap),
        out_shape=jax.ShapeDtypeStruct((b * l, MLA_HEADS * MLA_V), BF16),
        scratch_shapes=[pltpu.VMEM((rows, 1), F32), pltpu.VMEM((rows, 1), F32),
                        pltpu.VMEM((rows, KV_LORA), F32)],
        compiler_params=_params("parallel", "arbitrary"),
        name="attention_prompt",
    )(qlat, qpe, kcat, wv)


def _attn_decode_kernel(pt_ref, ql_ref, qp_ref, cn_ref, kn_ref, lat_hbm, rope_hbm, o_ref,
                        kbuf, rbuf, sem):
    b = pl.program_id(0)
    nb = pl.num_programs(0)
    n_pages = pt_ref.shape[1]
    page = lat_hbm.shape[1]

    def copies(seq, slot, p):
        phys = pt_ref[seq, p]
        dst = pl.ds(p * page, page)
        return (pltpu.make_async_copy(lat_hbm.at[phys], kbuf.at[slot, dst], sem.at[slot, 0]),
                pltpu.make_async_copy(rope_hbm.at[phys], rbuf.at[slot, dst], sem.at[slot, 1]))

    def start_all(seq, slot):
        def body(p, carry):
            for cp in copies(seq, slot, p):
                cp.start()
            return carry
        lax.fori_loop(0, n_pages, body, 0)

    def wait_all(seq, slot):
        def body(p, carry):
            for cp in copies(seq, slot, p):
                cp.wait()
            return carry
        lax.fori_loop(0, n_pages, body, 0)

    slot = b % 2

    @pl.when(b == 0)
    def _():
        start_all(0, 0)

    @pl.when(b + 1 < nb)
    def _():
        start_all(b + 1, 1 - slot)

    wait_all(b, slot)
    ql = ql_ref[...]
    qp = qp_ref[...]
    kc = kbuf[slot].astype(BF16)
    kr = rbuf[slot].astype(BF16)
    s_past = _dot_nt(ql, kc) + _dot_nt(qp, kr)
    cn = cn_ref[...].astype(BF16).astype(F32)
    kn = kn_ref[...].astype(BF16).astype(F32)
    s_new = (jnp.sum(ql.astype(F32) * cn, axis=-1, keepdims=True)
             + jnp.sum(qp.astype(F32) * kn, axis=-1, keepdims=True))
    m = jnp.maximum(jnp.max(s_past, axis=-1, keepdims=True), s_new)
    p_past = jnp.exp(s_past - m)
    p_new = jnp.exp(s_new - m)
    denom = jnp.sum(p_past, axis=-1, keepdims=True) + p_new
    o = _dot(p_past.astype(BF16), kc) + p_new.astype(BF16).astype(F32) * cn
    o_ref[...] = o / denom


def _attention_decode(qlat, qpe, c_new, kpe_new, lat_cache, rope_cache, page_table):
    b = qlat.shape[0]
    h = MLA_HEADS
    n_pages = page_table.shape[1]
    page = lat_cache.shape[1]
    past = n_pages * page
    blk = lambda w: pl.BlockSpec((None, h, w), lambda i, pt: (i, 0, 0))
    one = lambda w: pl.BlockSpec((None, 1, w), lambda i, pt: (i, 0, 0))
    grid_spec = pltpu.PrefetchScalarGridSpec(
        num_scalar_prefetch=1,
        grid=(b,),
        in_specs=[blk(KV_LORA), blk(MLA_ROPE), one(KV_LORA), one(MLA_ROPE),
                  pl.BlockSpec(memory_space=pl.ANY), pl.BlockSpec(memory_space=pl.ANY)],
        out_specs=blk(KV_LORA),
        scratch_shapes=[pltpu.VMEM((2, past, KV_LORA), F32), pltpu.VMEM((2, past, MLA_ROPE), F32),
                        pltpu.SemaphoreType.DMA((2, 2))],
    )
    return pl.pallas_call(
        _attn_decode_kernel,
        grid_spec=grid_spec,
        out_shape=jax.ShapeDtypeStruct((b, h, KV_LORA), F32),
        compiler_params=_params("arbitrary"),
        name="attention_decode",
    )(page_table, qlat.reshape(b, h, KV_LORA), qpe.reshape(b, h, MLA_ROPE),
      c_new.reshape(b, 1, KV_LORA), kpe_new.reshape(b, 1, MLA_ROPE), lat_cache, rope_cache)


def _route(logits):
    lane = lax.broadcasted_iota(I32, logits.shape, 1)
    lane_f = lane.astype(F32)
    neg = -jnp.inf
    big = float(LANES)
    gmask = lane < N_GROUPS
    gl = jnp.where(gmask, logits, neg)
    gmax = jnp.max(gl, axis=-1, keepdims=True)
    g_sel = jnp.min(jnp.where(gl == gmax, lane_f, big), axis=-1, keepdims=True)
    gsum = jnp.sum(jnp.where(gmask, jnp.exp(gl - gmax), 0.0), axis=-1, keepdims=True)
    g_w = 1.0 / gsum
    lo = N_GROUPS + EXPERTS_PER_GROUP * g_sel
    emask = (lane_f >= lo) & (lane_f < lo + EXPERTS_PER_GROUP)
    el = jnp.where(emask, logits, neg)
    v1 = jnp.max(el, axis=-1, keepdims=True)
    i1 = jnp.min(jnp.where(el == v1, lane_f, big), axis=-1, keepdims=True)
    el2 = jnp.where(lane_f == i1, neg, el)
    v2 = jnp.max(el2, axis=-1, keepdims=True)
    i2 = jnp.min(jnp.where(el2 == v2, lane_f, big), axis=-1, keepdims=True)
    e2 = jnp.exp(v2 - v1)
    w1 = g_w / (1.0 + e2)
    w2 = g_w * e2 / (1.0 + e2)
    ids = jnp.where(lane == 0, i1 - N_GROUPS, jnp.where(lane == 1, i2 - N_GROUPS, 0.0)).astype(I32)
    wts = jnp.where(lane == 0, w1, jnp.where(lane == 1, w2, 0.0))
    return ids, wts


def _mix_out_kernel(*refs, n_act, has_pre, has_bias):
    refs = list(refs)
    acts = [refs.pop(0) for _ in range(n_act)]
    ws = [refs.pop(0) for _ in range(n_act)]
    pre = refs.pop(0) if has_pre else None
    bias = refs.pop(0) if has_bias else None
    h_ref, nw_ref, rw_ref, rb_ref, h1_ref, xn_ref, rid_ref, rwt_ref = refs
    out = None
    for k, (a_ref, w_ref) in enumerate(zip(acts, ws)):
        a = a_ref[...]
        if has_pre and k == n_act - 1:
            a = _dot(a.astype(BF16), pre[...])
        t = _dot(a.astype(BF16), w_ref[...])
        out = t if out is None else out + t
    if has_bias:
        out = out + bias[...]
    h1 = h_ref[...] + out
    h1_ref[...] = h1
    xn = _rms(h1, nw_ref[...])
    xn_ref[...] = xn
    logits = jnp.dot(xn, rw_ref[...], preferred_element_type=F32,
                     precision=lax.Precision.HIGHEST) + rb_ref[...]
    ids, wts = _route(logits)
    rid_ref[...] = ids
    rwt_ref[...] = wts


def _mix_out(acts, ws, h, nw, rw, rb, tm, pre=None, bias=None):
    n, d = h.shape
    row = lambda i: (i, 0)
    const = lambda i: (0, 0)
    ins = list(acts) + list(ws)
    specs = [pl.BlockSpec((tm, a.shape[1]), row) for a in acts] + [pl.BlockSpec(w.shape, const) for w in ws]
    if pre is not None:
        ins.append(pre)
        specs.append(pl.BlockSpec(pre.shape, const))
    if bias is not None:
        ins.append(bias)
        specs.append(pl.BlockSpec(bias.shape, const))
    ins += [h, nw, rw, rb]
    specs += [pl.BlockSpec((tm, d), row), pl.BlockSpec((1, d), const),
              pl.BlockSpec(rw.shape, const), pl.BlockSpec(rb.shape, const)]
    kern = functools.partial(_mix_out_kernel, n_act=len(acts), has_pre=pre is not None,
                             has_bias=bias is not None)
    return pl.pallas_call(
        kern,
        grid=(n // tm,),
        in_specs=specs,
        out_specs=[pl.BlockSpec((tm, d), row), pl.BlockSpec((tm, d), row),
                   pl.BlockSpec((tm, LANES), row), pl.BlockSpec((tm, LANES), row)],
        out_shape=[jax.ShapeDtypeStruct((n, d), F32), jax.ShapeDtypeStruct((n, d), F32),
                   jax.ShapeDtypeStruct((n, LANES), I32), jax.ShapeDtypeStruct((n, LANES), F32)],
        compiler_params=_params("parallel"),
        name="mix_out_route",
    )(*ins)


def _route_slots(ids, tr):
    n = ids.shape[0]
    e = ids.reshape(-1)
    onehot = (e[:, None] == jnp.arange(N_EXPERTS, dtype=I32)[None, :]).astype(I32)
    csum = jnp.cumsum(onehot, axis=0)
    rank = jnp.take_along_axis(csum, e[:, None], axis=1)[:, 0] - 1
    counts = csum[-1]
    padded = ((counts + tr - 1) // tr) * tr
    ends = jnp.cumsum(padded)
    slot = (ends - padded)[e] + rank
    n_tiles = (2 * n) // tr + N_EXPERTS
    src = jnp.zeros((n_tiles * tr,), I32).at[slot].set(jnp.arange(2 * n, dtype=I32) // 2)
    tile_start = jnp.arange(n_tiles, dtype=I32) * tr
    tile_expert = jnp.minimum(jnp.searchsorted(ends, tile_start, side="right"), N_EXPERTS - 1).astype(I32)
    n_used = (ends[-1] // tr).astype(I32).reshape(1)
    return src, slot.reshape(n, 2), tile_expert, n_used, n_tiles


def _gather_kernel(idx_ref, src_hbm, out_ref, sem):
    rows = out_ref.shape[0]

    def row_copy(r):
        return pltpu.make_async_copy(src_hbm.at[pl.ds(idx_ref[0, r], 1)], out_ref.at[pl.ds(r, 1)], sem)

    def start(r, carry):
        row_copy(r).start()
        return carry

    def wait(r, carry):
        row_copy(r).wait()
        return carry

    lax.fori_loop(0, rows, start, 0)
    lax.fori_loop(0, rows, wait, 0)


def _smem_rows(idx, rows):
    return idx.reshape(-1, 1, rows), pl.BlockSpec((None, 1, rows), lambda i: (i, 0, 0), memory_space=pltpu.SMEM)


def _gather_rows(src, idx, tr):
    n_rows = idx.shape[0]
    d = src.shape[1]
    idx3, idx_spec = _smem_rows(idx, tr)
    return pl.pallas_call(
        _gather_kernel,
        grid=(n_rows // tr,),
        in_specs=[idx_spec, pl.BlockSpec(memory_space=pl.ANY)],
        out_specs=pl.BlockSpec((tr, d), lambda i: (i, 0)),
        out_shape=jax.ShapeDtypeStruct((n_rows, d), src.dtype),
        scratch_shapes=[pltpu.SemaphoreType.DMA(())],
        compiler_params=_params("arbitrary"),
        name="moe_gather",
    )(idx3, src)


def _ffn_kernel(te_ref, nu_ref, x_ref, wg_ref, wu_ref, wd_ref, o_ref):
    i = pl.program_id(0)

    @pl.when(i < nu_ref[0])
    def _():
        x = x_ref[...].astype(BF16)
        g = _dot(x, wg_ref[...].astype(BF16))
        u = _dot(x, wu_ref[...].astype(BF16))
        hmid = (g * _sigmoid(g) * u).astype(BF16)
        o_ref[...] = _dot(hmid, wd_ref[...].astype(BF16))

    @pl.when(i >= nu_ref[0])
    def _():
        o_ref[...] = jnp.zeros_like(o_ref)


def _moe_ffn(xs, tile_expert, n_used, wg, wu, wd, tr):
    n_rows, d = xs.shape
    de = wg.shape[2]
    grid_spec = pltpu.PrefetchScalarGridSpec(
        num_scalar_prefetch=2,
        grid=(n_rows // tr,),
        in_specs=[pl.BlockSpec((tr, d), lambda i, te, nu: (i, 0)),
                  pl.BlockSpec((None, d, de), lambda i, te, nu: (te[i], 0, 0)),
                  pl.BlockSpec((None, d, de), lambda i, te, nu: (te[i], 0, 0)),
                  pl.BlockSpec((None, de, d), lambda i, te, nu: (te[i], 0, 0))],
        out_specs=pl.BlockSpec((tr, d), lambda i, te, nu: (i, 0)),
    )
    return pl.pallas_call(
        _ffn_kernel,
        grid_spec=grid_spec,
        out_shape=jax.ShapeDtypeStruct((n_rows, d), F32),
        compiler_params=_params("arbitrary"),
        name="moe_ffn",
    )(tile_expert, n_used, xs, wg, wu, wd)


def _combine_kernel(s1_ref, s2_ref, h_ref, w_ref, ys_hbm, o_ref, g1, g2, sem):
    rows = o_ref.shape[0]

    def copies(r):
        return (pltpu.make_async_copy(ys_hbm.at[pl.ds(s1_ref[0, r], 1)], g1.at[pl.ds(r, 1)], sem.at[0]),
                pltpu.make_async_copy(ys_hbm.at[pl.ds(s2_ref[0, r], 1)], g2.at[pl.ds(r, 1)], sem.at[1]))

    def start(r, carry):
        for cp in copies(r):
            cp.start()
        return carry

    def wait(r, carry):
        for cp in copies(r):
            cp.wait()
        return carry

    lax.fori_loop(0, rows, start, 0)
    lax.fori_loop(0, rows, wait, 0)
    w = w_ref[...]
    o_ref[...] = h_ref[...] + w[:, 0:1] * g1[...] + w[:, 1:2] * g2[...]


def _moe_combine(h, wts, ys, slots, tm):
    n, d = h.shape
    row = lambda i: (i, 0)
    s1, s_spec = _smem_rows(slots[:, 0], tm)
    s2, _ = _smem_rows(slots[:, 1], tm)
    return pl.pallas_call(
        _combine_kernel,
        grid=(n // tm,),
        in_specs=[s_spec, s_spec, pl.BlockSpec((tm, d), row), pl.BlockSpec((tm, LANES), row),
                  pl.BlockSpec(memory_space=pl.ANY)],
        out_specs=pl.BlockSpec((tm, d), row),
        out_shape=jax.ShapeDtypeStruct((n, d), F32),
        scratch_shapes=[pltpu.VMEM((tm, d), F32), pltpu.VMEM((tm, d), F32),
                        pltpu.SemaphoreType.DMA((2,))],
        compiler_params=_params("arbitrary"),
        name="moe_combine",
    )(s1, s2, h, wts, ys)


def _moe(h1, xn, rid, rwt, wg, wu, wd, tr, tm):
    src, slots, tile_expert, n_used, _ = _route_slots(rid[:, :2], tr)
    xs = _gather_rows(xn, src, tr)
    ys = _moe_ffn(xs, tile_expert, n_used, wg, wu, wd, tr)
    return _moe_combine(h1, rwt, ys, slots, tm)


def _ple_kernel(*refs, tail):
    h_ref, p_ref, proj_ref, pn_ref, gate_ref = refs[:5]
    h = h_ref[...]
    pp = _dot(p_ref[...].astype(BF16), proj_ref[...])
    g = _dot(_rms(h, pn_ref[...]).astype(BF16), gate_ref[...])
    h3 = h + pp * _sigmoid(g)
    if tail == "conv_in":
        nm_ref, pw1_ref, pb1_ref, h3_ref, a_ref = refs[5:]
        h3_ref[...] = h3
        d = h3.shape[1]
        hn = _rms(h3, nm_ref[...]).astype(BF16)
        ua = _dot(hn, pw1_ref[:, :d]) + pb1_ref[:, :d]
        ub = _dot(hn, pw1_ref[:, d:]) + pb1_ref[:, d:]
        a_ref[...] = ua * _sigmoid(ub)
    else:
        nf_ref, y_ref = refs[5:]
        y_ref[...] = _rms(h3, nf_ref[...])


def _ple(h, p, proj, pn, gate, tm, tail, extra):
    n, d = h.shape
    row = lambda i: (i, 0)
    const = lambda i: (0, 0)
    ins = [h, p, proj, pn, gate] + list(extra)
    specs = [pl.BlockSpec((tm, d), row), pl.BlockSpec((tm, p.shape[1]), row),
             pl.BlockSpec(proj.shape, const), pl.BlockSpec((1, d), const),
             pl.BlockSpec(gate.shape, const)] + [pl.BlockSpec(e.shape, const) for e in extra]
    n_out = 2 if tail == "conv_in" else 1
    return pl.pallas_call(
        functools.partial(_ple_kernel, tail=tail),
        grid=(n // tm,),
        in_specs=specs,
        out_specs=[pl.BlockSpec((tm, d), row)] * n_out,
        out_shape=[jax.ShapeDtypeStruct((n, d), F32)] * n_out,
        compiler_params=_params("parallel"),
        name="ple_" + tail,
    )(*ins)


def _ln_swish(y, w, b):
    mu = jnp.mean(y, axis=-1, keepdims=True)
    var = jnp.mean(jnp.square(y - mu), axis=-1, keepdims=True)
    z = (y - mu) * lax.rsqrt(var + EPS) * w + b
    return z * _sigmoid(z)


def _conv_kernel(a_ref, dw_ref, dwb_ref, lnw_ref, lnb_ref, o_ref, ext, acc):
    t = pl.program_id(1)
    tl = a_ref.shape[0]
    halo = ext.shape[0] - tl

    @pl.when(t == 0)
    def _():
        ext[0:halo, :] = jnp.zeros((halo, ext.shape[1]), F32)

    ext[halo:, :] = a_ref[...]
    first = halo - (CONV_WIDTH - 1)
    for c0 in range(0, ext.shape[1], LANES):
        cs = slice(c0, c0 + LANES)
        s = ext[first:first + tl, cs] * dw_ref[0:1, cs]
        for j in range(1, CONV_WIDTH):
            s = s + ext[first + j:first + j + tl, cs] * dw_ref[j:j + 1, cs]
        acc[:, cs] = s
    ext[0:halo, :] = ext[tl:tl + halo, :]
    o_ref[...] = _ln_swish(acc[...] + dwb_ref[...], lnw_ref[...], lnb_ref[...]).astype(BF16)


def _conv_prompt(a, dw, dwb, lnw, lnb, b, l, tl=256):
    d = a.shape[1]
    nt = l // tl
    halo = 32
    row = lambda bi, t: (bi * nt + t, 0)
    const = lambda bi, t: (0, 0)
    return pl.pallas_call(
        _conv_kernel,
        grid=(b, nt),
        in_specs=[pl.BlockSpec((tl, d), row), pl.BlockSpec(dw.shape, const),
                  pl.BlockSpec((1, d), const), pl.BlockSpec((1, d), const), pl.BlockSpec((1, d), const)],
        out_specs=pl.BlockSpec((tl, d), row),
        out_shape=jax.ShapeDtypeStruct((b * l, d), BF16),
        scratch_shapes=[pltpu.VMEM((tl + halo, d), F32), pltpu.VMEM((tl, d), F32)],
        compiler_params=_params("parallel", "arbitrary"),
        name="conv_prompt",
    )(a, dw, dwb, lnw, lnb)


def _conv_decode_kernel(buf_ref, a_ref, dw_ref, dwb_ref, lnw_ref, lnb_ref, y_ref, new_ref):
    w = CONV_WIDTH - 1
    buf = buf_ref[...]
    a = a_ref[...]
    y = jnp.sum(buf * dw_ref[0:w, :][None], axis=1, keepdims=True) + a * dw_ref[w:w + 1, :][None]
    y_ref[...] = _ln_swish(y + dwb_ref[...][None], lnw_ref[...][None], lnb_ref[...][None])
    new_ref[:, 0:w - 1, :] = buf[:, 1:w, :]
    new_ref[:, w - 1:w, :] = a


def _conv_decode(buf, a, dw, dwb, lnw, lnb, bb=8):
    b, w, d = buf.shape
    blk = lambda r: pl.BlockSpec((bb, r, d), lambda i: (i, 0, 0))
    const = lambda i: (0, 0)
    y, new = pl.pallas_call(
        _conv_decode_kernel,
        grid=(b // bb,),
        in_specs=[blk(w), blk(1), pl.BlockSpec(dw.shape, const), pl.BlockSpec((1, d), const),
                  pl.BlockSpec((1, d), const), pl.BlockSpec((1, d), const)],
        out_specs=[blk(1), blk(w)],
        out_shape=[jax.ShapeDtypeStruct((b, 1, d), F32), jax.ShapeDtypeStruct((b, w, d), F32)],
        compiler_params=_params("parallel"),
        name="conv_decode",
    )(buf, a.reshape(b, 1, d), dw, dwb, lnw, lnb)
    return y.reshape(b, d), new


def _prep_weights(W):
    P = {}
    w_in = W["w_in"][0]
    kpe_cols = w_in[:, -MLA_ROPE:]
    P["w_in"] = jnp.concatenate([w_in[:, :-MLA_ROPE]] + [kpe_cols] * (LANES // MLA_ROPE), axis=1).astype(BF16)
    w_uq = W["w_uq"][0]
    nope = jnp.pad(w_uq[..., :MLA_NOPE], ((0, 0), (0, 0), (0, LANES - MLA_NOPE)))
    P["w_uq_nope"] = nope.reshape(Q_LORA, MLA_HEADS * LANES).astype(BF16)
    P["w_uq_pe"] = w_uq[..., MLA_NOPE:].reshape(Q_LORA, MLA_HEADS * MLA_ROPE).astype(BF16)
    w_ukv = W["w_ukv"][0]
    wk = jnp.transpose(w_ukv[..., :MLA_NOPE], (1, 2, 0))
    P["wk"] = jnp.pad(wk, ((0, 0), (0, LANES - MLA_NOPE), (0, 0))).astype(BF16)
    wv = jnp.transpose(w_ukv[..., MLA_NOPE:], (1, 0, 2))
    P["wv"] = wv.astype(BF16)
    eye = jnp.eye(MLA_HEADS, dtype=F32)
    P["wv_bd"] = (wv[:, :, None, :] * eye[:, None, :, None]).reshape(
        MLA_HEADS * KV_LORA, MLA_HEADS * MLA_V).astype(BF16)
    w_out = W["w_out"][0].astype(BF16)
    P["w_out_r"] = w_out[:RET_HEADS * RET_DV]
    P["w_out_m"] = w_out[RET_HEADS * RET_DV:]
    d = w_out.shape[1]
    for i in range(W["moe_w_group"].shape[0]):
        rw = jnp.concatenate([W["moe_w_group"][i], W["moe_w_router"][i].reshape(d, N_EXPERTS)], axis=1)
        rb = jnp.concatenate([W["moe_b_group"][i], W["moe_b_router"][i].reshape(N_EXPERTS)])
        padc = LANES - rw.shape[1]
        P["router_w", i] = jnp.pad(rw, ((0, 0), (0, padc)))
        P["router_b", i] = jnp.pad(rb, (0, padc)).reshape(1, LANES)
        P["ple_proj", i] = W["ple_proj"][i].astype(BF16)
        P["ple_gate", i] = W["ple_gate"][i].astype(BF16)
    P["pw1"] = W["conv_pw1"][0].astype(BF16)
    P["pw2"] = W["conv_pw2"][0].astype(BF16)
    P["dw"] = jnp.pad(W["conv_dw"][0], ((0, 1), (0, 0)))
    return P


def _vec(v):
    return v.reshape(1, -1)


def _trunk(x, p, pos, W, P, tm, tr, past):
    b, l, d = x.shape
    n = b * l
    h = x.reshape(n, d)
    tabs = _rope_tables(pos, RET_DK) + _rope_tables(pos, MLA_ROPE)
    rq, rk, rv, rg, qlat, qpe, c, kpe, kcat = _proj_in(
        h, _vec(W["norm_mix"][0]), P["w_in"], tabs, _vec(W["mla_q_norm"][0]), _vec(W["mla_kv_norm"][0]),
        P["w_uq_nope"], P["w_uq_pe"], P["wk"], tm)
    gnw = _vec(W["ret_gn_w"][0])
    if past is None:
        o_r, s_new = _retention_prompt(rq, rk, rv, rg, gnw, b, l)
        o_m = _attention_prompt(qlat, qpe, kcat, P["wv"], b, l)
        pre = None
    else:
        ret_s0, conv_buf0, lat_cache, rope_cache, page_table = past
        o_r, s_new = _retention_decode(rq, rk, rv, rg, gnw, ret_s0)
        o_m = _attention_decode(qlat, qpe, c, kpe, lat_cache, rope_cache, page_table)
        o_m = o_m.reshape(n, MLA_HEADS * KV_LORA)
        pre = P["wv_bd"]
    h1, xn, rid, rwt = _mix_out([o_r, o_m], [P["w_out_r"], P["w_out_m"]], h, _vec(W["norm_ffn"][0]),
                                P["router_w", 0], P["router_b", 0], tm, pre=pre)
    h2 = _moe(h1, xn, rid, rwt, W["moe_w_gate"][0], W["moe_w_up"][0], W["moe_w_down"][0], tr, tm)
    h3, a = _ple(h2, p[0].reshape(n, -1), P["ple_proj", 0], _vec(W["ple_norm"][0]), P["ple_gate", 0], tm,
                 "conv_in", [_vec(W["norm_mix"][1]), P["pw1"], _vec(W["conv_pw1_b"][0])])
    conv_vecs = (_vec(W["conv_dw_b"][0]), _vec(W["conv_ln_w"][0]), _vec(W["conv_ln_b"][0]))
    if past is None:
        yact = _conv_prompt(a, P["dw"], *conv_vecs, b, l)
        conv_state = a.reshape(b, l, d)[:, l - (CONV_WIDTH - 1):, :]
    else:
        yact, conv_state = _conv_decode(conv_buf0, a, P["dw"], *conv_vecs)
    h4, xn, rid, rwt = _mix_out([yact], [P["pw2"]], h3, _vec(W["norm_ffn"][1]),
                                P["router_w", 1], P["router_b", 1], tm, bias=_vec(W["conv_pw2_b"][0]))
    h5 = _moe(h4, xn, rid, rwt, W["moe_w_gate"][1], W["moe_w_up"][1], W["moe_w_down"][1], tr, tm)
    (y,) = _ple(h5, p[1].reshape(n, -1), P["ple_proj", 1], _vec(W["ple_norm"][1]), P["ple_gate", 1], tm,
                "final", [_vec(W["norm_final"])])
    return (y.reshape(b, l, d), c.reshape(1, b, l, KV_LORA), kpe.reshape(1, b, l, MLA_ROPE),
            s_new[None], conv_state[None])


def kernel(x_prompt, x_sample, p_prompt, p_sample, cache_mla_latent, cache_mla_rope, state_retention, state_conv,
           page_table, norm_mix, norm_ffn, norm_final, w_in, ret_gn_w, mla_q_norm, mla_kv_norm, w_uq, w_ukv, w_out,
           conv_pw1, conv_pw1_b, conv_dw, conv_dw_b, conv_ln_w, conv_ln_b, conv_pw2, conv_pw2_b,
           moe_w_group, moe_b_group, moe_w_router, moe_b_router, moe_w_gate, moe_w_up, moe_w_down,
           ple_proj, ple_norm, ple_gate):
    W = dict(norm_mix=norm_mix, norm_ffn=norm_ffn, norm_final=norm_final, w_in=w_in, ret_gn_w=ret_gn_w,
             mla_q_norm=mla_q_norm, mla_kv_norm=mla_kv_norm, w_uq=w_uq, w_ukv=w_ukv, w_out=w_out,
             conv_pw1=conv_pw1, conv_pw1_b=conv_pw1_b, conv_dw=conv_dw, conv_dw_b=conv_dw_b, conv_ln_w=conv_ln_w,
             conv_ln_b=conv_ln_b, conv_pw2=conv_pw2, conv_pw2_b=conv_pw2_b, moe_w_group=moe_w_group,
             moe_b_group=moe_b_group, moe_w_router=moe_w_router, moe_b_router=moe_b_router, moe_w_gate=moe_w_gate,
             moe_w_up=moe_w_up, moe_w_down=moe_w_down, ple_proj=ple_proj, ple_norm=ple_norm, ple_gate=ple_gate)
    assert w_in.shape[0] == 1 and conv_pw1.shape[0] == 1, "one retention/attention layer, one conv layer"
    P = _prep_weights(W)
    past_len = page_table.shape[1] * cache_mla_latent.shape[2]
    pos_prompt = jnp.arange(x_prompt.shape[1])
    pos_sample = past_len + jnp.arange(x_sample.shape[1])
    tm_prompt = min(512, x_prompt.shape[1])
    outs_p = _trunk(x_prompt, p_prompt, pos_prompt, W, P, tm_prompt, 256, None)
    past = (state_retention[0], state_conv[0], cache_mla_latent[0], cache_mla_rope[0], page_table)
    tm_sample = x_sample.shape[0] * x_sample.shape[1]
    outs_s = _trunk(x_sample, p_sample, pos_sample, W, P, tm_sample, 32, past)
    y_p, lat_p, rope_p, ret_p, conv_p = outs_p
    y_s, lat_s, rope_s, ret_s, conv_s = outs_s
    return (y_p, y_s, lat_p, rope_p, ret_p, conv_p, lat_s, rope_s, ret_s, conv_s)
```

```python
import functools

import jax
import jax.numpy as jnp
from jax import lax
from jax.experimental import pallas as pl
from jax.experimental.pallas import tpu as pltpu

F32 = jnp.float32
BF16 = jnp.bfloat16
I32 = jnp.int32

EPS = 1e-6
ROPE_THETA = 10000.0
RET_HEADS = 8
RET_DK = 64
RET_DV = 64
RET_CHUNK = 128
MLA_HEADS = 8
MLA_NOPE = 64
MLA_ROPE = 32
MLA_V = 64
Q_LORA = 256
KV_LORA = 128
CONV_WIDTH = 31
N_GROUPS = 4
EXPERTS_PER_GROUP = 8
N_EXPERTS = N_GROUPS * EXPERTS_PER_GROUP
LANES = 128
Q_BLOCK = 128
KV_BLOCK = 256
VMEM_LIMIT_BYTES = 56 * 1024 * 1024


def _params(*sem):
    return pltpu.CompilerParams(dimension_semantics=sem, vmem_limit_bytes=VMEM_LIMIT_BYTES)


def _rms(x, w):
    return x * lax.rsqrt(jnp.mean(x * x, axis=-1, keepdims=True) + EPS) * w


def _sigmoid(x):
    return 1.0 / (1.0 + jnp.exp(-x))


def _dot(a, b):
    return jnp.dot(a, b, preferred_element_type=F32)


def _dot_nt(a, b):
    return lax.dot_general(a, b, (((1,), (1,)), ((), ())), preferred_element_type=F32)


def _dot_tn(a, b):
    return lax.dot_general(a, b, (((0,), (0,)), ((), ())), preferred_element_type=F32)


def _rope(x, cos_t, sin_t, period):
    half = period // 2
    lane = lax.broadcasted_iota(I32, x.shape, 1)
    first = (lane % period) < half
    rot = jnp.where(first, pltpu.roll(x, LANES - half, 1), pltpu.roll(x, half, 1))
    return x * cos_t + rot * sin_t


def _rope_tables(pos, period):
    half = period // 2
    inv = ROPE_THETA ** (-2.0 * jnp.arange(half, dtype=F32) / period)
    ang = pos.astype(F32)[:, None] * inv[None, :]
    cos = jnp.cos(ang)
    sin = jnp.sin(ang)
    cos_p = jnp.concatenate([cos, cos], axis=1)
    sin_p = jnp.concatenate([-sin, sin], axis=1)
    reps = LANES // period
    return jnp.tile(cos_p, (1, reps)), jnp.tile(sin_p, (1, reps))


def _proj_in_kernel(x_ref, nw_ref, win_ref, cr_ref, sr_ref, cp_ref, sp_ref, qn_ref, kvn_ref,
                    wuqn_ref, wuqp_ref, wk_ref,
                    rq_ref, rk_ref, rv_ref, rg_ref, qlat_ref, qpe_ref, c_ref, kpe_ref, kcat_ref):
    hn = _rms(x_ref[...], nw_ref[...]).astype(BF16)
    cr, sr, cp, sp = cr_ref[...], sr_ref[...], cp_ref[...], sp_ref[...]
    hk = RET_HEADS * RET_DK
    hv = RET_HEADS * RET_DV

    def mm(lo, width):
        return _dot(hn, win_ref[:, lo:lo + width])

    u = mm(0, hk)
    rq_ref[...] = jnp.concatenate(
        [_rope(u[:, g:g + LANES], cr, sr, RET_DK) for g in range(0, hk, LANES)], axis=1).astype(BF16)
    u = mm(hk, hk)
    rk_ref[...] = jnp.concatenate(
        [_rope(u[:, g:g + LANES], cr, sr, RET_DK) * (RET_DK ** -0.5) for g in range(0, hk, LANES)],
        axis=1).astype(BF16)
    rv_ref[...] = mm(2 * hk, hv).astype(BF16)
    rg_ref[...] = mm(2 * hk + hv, hv)
    off = 2 * hk + 2 * hv
    cqn = _rms(mm(off, Q_LORA), qn_ref[...]).astype(BF16)
    c = _rms(mm(off + Q_LORA, KV_LORA), kvn_ref[...])
    c_ref[...] = c
    kp = _rope(mm(off + Q_LORA + KV_LORA, LANES), cp, sp, MLA_ROPE)
    kpe_ref[...] = kp[:, :MLA_ROPE]
    kcat_ref[...] = jnp.concatenate([c, kp], axis=1).astype(BF16)
    scale = (MLA_NOPE + MLA_ROPE) ** -0.5
    qn = _dot(cqn, wuqn_ref[...])
    for h in range(MLA_HEADS):
        ql = _dot(qn[:, h * LANES:(h + 1) * LANES].astype(BF16), wk_ref[h]) * scale
        qlat_ref[:, h * KV_LORA:(h + 1) * KV_LORA] = ql.astype(BF16)
    qp = _dot(cqn, wuqp_ref[...])
    qpe_ref[...] = jnp.concatenate(
        [_rope(qp[:, g:g + LANES], cp, sp, MLA_ROPE) * scale
         for g in range(0, MLA_HEADS * MLA_ROPE, LANES)], axis=1).astype(BF16)


def _proj_in(x2, nw, win, tabs, qn, kvn, wuqn, wuqp, wk, tm):
    n, d = x2.shape
    cr, sr, cp, sp = tabs
    ltab = cr.shape[0]
    nblk = ltab // tm if ltab >= tm else 0
    tab_rows = tm if nblk else ltab
    tab_map = (lambda i: (i % nblk, 0)) if nblk else (lambda i: (0, 0))
    row = lambda i: (i, 0)
    const = lambda i: (0, 0)
    hk = RET_HEADS * RET_DK
    hv = RET_HEADS * RET_DV
    tab_spec = pl.BlockSpec((tab_rows, LANES), tab_map)
    out_cols = [(hk, BF16), (hk, BF16), (hv, BF16), (hv, F32), (MLA_HEADS * KV_LORA, BF16),
                (MLA_HEADS * MLA_ROPE, BF16), (KV_LORA, F32), (MLA_ROPE, F32), (2 * LANES, BF16)]
    return pl.pallas_call(
        _proj_in_kernel,
        grid=(n // tm,),
        in_specs=[pl.BlockSpec((tm, d), row), pl.BlockSpec((1, d), const),
                  pl.BlockSpec(win.shape, const), tab_spec, tab_spec, tab_spec, tab_spec,
                  pl.BlockSpec((1, Q_LORA), const), pl.BlockSpec((1, KV_LORA), const),
                  pl.BlockSpec(wuqn.shape, const), pl.BlockSpec(wuqp.shape, const),
                  pl.BlockSpec(wk.shape, lambda i: (0, 0, 0))],
        out_specs=[pl.BlockSpec((tm, w), row) for w, _ in out_cols],
        out_shape=[jax.ShapeDtypeStruct((n, w), dt) for w, dt in out_cols],
        compiler_params=_params("parallel"),
        name="proj_in",
    )(x2, nw, win, cr, sr, cp, sp, qn, kvn, wuqn, wuqp, wk)


def _ret_kernel(rq_ref, rk_ref, rv_ref, rg_ref, gnw_ref, decay_ref, qdec_ref, kdec_ref, cdec_ref,
                o_ref, s_out_ref, s_scr):
    ci = pl.program_id(1)

    @pl.when(ci == 0)
    def _():
        s_scr[...] = jnp.zeros_like(s_scr)

    outs = []
    for h in range(RET_HEADS):
        sl = slice(h * RET_DK, (h + 1) * RET_DK)
        q = rq_ref[:, sl]
        k = rk_ref[:, sl]
        v = rv_ref[:, sl]
        s = s_scr[h]
        sc = _dot_nt(q, k) * decay_ref[h]
        o = _dot(sc.astype(BF16), v) + _dot(q, s.astype(BF16)) * qdec_ref[:, sl]
        kd = (k.astype(F32) * kdec_ref[:, sl]).astype(BF16)
        s_scr[h] = s * cdec_ref[:, sl] + _dot_tn(kd, v)
        mu = jnp.mean(o, axis=-1, keepdims=True)
        var = jnp.mean(jnp.square(o - mu), axis=-1, keepdims=True)
        outs.append((o - mu) * lax.rsqrt(var + EPS))
    g = rg_ref[...]
    o_ref[...] = (jnp.concatenate(outs, axis=1) * gnw_ref[...] * (g * _sigmoid(g))).astype(BF16)

    @pl.when(ci == pl.num_programs(1) - 1)
    def _():
        s_out_ref[...] = s_scr[...]


def _ret_tables(chunk):
    h = RET_HEADS
    log_g = jnp.log1p(-jnp.exp2(-5.0 - jnp.arange(h, dtype=F32)))
    idx = jnp.arange(chunk, dtype=F32)
    rel = idx[:, None] - idx[None, :]
    decay = jnp.where(rel[None] >= 0, jnp.exp(jnp.maximum(rel, 0.0)[None] * log_g[:, None, None]), 0.0)
    q_dec = jnp.exp((idx[:, None] + 1.0) * log_g[None, :])
    k_dec = jnp.exp((chunk - 1.0 - idx)[:, None] * log_g[None, :])
    c_dec = jnp.exp(chunk * log_g)
    rep = lambda t: jnp.repeat(t, RET_DK, axis=-1)
    return decay, rep(q_dec), rep(k_dec), rep(c_dec[None, :])


def _retention_prompt(rq, rk, rv, rg, gnw, b, l):
    chunk = RET_CHUNK if l % RET_CHUNK == 0 else l
    decay, qdec, kdec, cdec = _ret_tables(chunk)
    hk = RET_HEADS * RET_DK
    n_chunks = l // chunk
    row = lambda bi, ci: (bi * n_chunks + ci, 0)
    const = lambda bi, ci: (0, 0)
    return pl.pallas_call(
        _ret_kernel,
        grid=(b, n_chunks),
        in_specs=[pl.BlockSpec((chunk, hk), row)] * 4 + [
            pl.BlockSpec((1, hk), const),
            pl.BlockSpec(decay.shape, lambda bi, ci: (0, 0, 0)),
            pl.BlockSpec((chunk, hk), const), pl.BlockSpec((chunk, hk), const),
            pl.BlockSpec((1, hk), const)],
        out_specs=[pl.BlockSpec((chunk, hk), row),
                   pl.BlockSpec((None, None, RET_HEADS, RET_DK, RET_DV), lambda bi, ci: (0, bi, 0, 0, 0))],
        out_shape=[jax.ShapeDtypeStruct((b * l, hk), BF16),
                   jax.ShapeDtypeStruct((1, b, RET_HEADS, RET_DK, RET_DV), F32)],
        scratch_shapes=[pltpu.VMEM((RET_HEADS, RET_DK, RET_DV), F32)],
        compiler_params=_params("parallel", "arbitrary"),
        name="retention_prompt",
    )(rq, rk, rv, rg, gnw, decay, qdec, kdec, cdec)


def _ret_decode_kernel(qc_ref, kc_ref, qr_ref, kr_ref, v_ref, g_ref, gnw_ref, gam_ref, s_ref,
                       o_ref, s_out_ref):
    s = s_ref[...]
    gam = gam_ref[...]
    v = v_ref[...]
    qk = jnp.sum(qr_ref[...] * kr_ref[...], axis=-1, keepdims=True)
    cross = jnp.sum(qc_ref[...] * gam * s, axis=-2, keepdims=True)
    o = qk * v + cross
    s_out_ref[...] = s * gam + kc_ref[...] * v
    mu = jnp.mean(o, axis=-1, keepdims=True)
    var = jnp.mean(jnp.square(o - mu), axis=-1, keepdims=True)
    g = g_ref[...]
    o_ref[...] = (o - mu) * lax.rsqrt(var + EPS) * gnw_ref[...] * (g * _sigmoid(g))


def _retention_decode(rq, rk, rv, rg, gnw, s0, bb=8):
    b = rq.shape[0]
    h, dk, dv = RET_HEADS, RET_DK, RET_DV
    gam = jnp.exp(jnp.log1p(-jnp.exp2(-5.0 - jnp.arange(h, dtype=F32))))
    gam4 = jnp.broadcast_to(gam[None, :, None, None], (1, h, 1, dv))
    col = lambda t: t.astype(F32).reshape(b, h, dk, 1)
    rowv = lambda t, w: t.astype(F32).reshape(b, h, 1, w)
    cspec = pl.BlockSpec((bb, h, dk, 1), lambda i: (i, 0, 0, 0))
    rspec = pl.BlockSpec((bb, h, 1, dk), lambda i: (i, 0, 0, 0))
    vspec = pl.BlockSpec((bb, h, 1, dv), lambda i: (i, 0, 0, 0))
    sspec = pl.BlockSpec((None, bb, h, dk, dv), lambda i: (0, i, 0, 0, 0))
    one = pl.BlockSpec((1, h, 1, dv), lambda i: (0, 0, 0, 0))
    o, s_new = pl.pallas_call(
        _ret_decode_kernel,
        grid=(b // bb,),
        in_specs=[cspec, cspec, rspec, rspec, vspec, vspec, one, one, sspec],
        out_specs=[vspec, sspec],
        out_shape=[jax.ShapeDtypeStruct((b, h, 1, dv), F32),
                   jax.ShapeDtypeStruct((1, b, h, dk, dv), F32)],
        compiler_params=_params("parallel"),
        name="retention_decode",
    )(col(rq), col(rk), rowv(rq, dk), rowv(rk, dk), rowv(rv, dv), rowv(rg, dv),
      gnw.reshape(1, h, 1, dv), gam4, s0)
    return o.reshape(b, h * dv), s_new


def _attn_kernel(qlat_ref, qpe_ref, kcat_ref, wv_ref, o_ref, q_scr, m_scr, acc_scr):
    i = pl.program_id(1)
    lane = lax.broadcasted_iota(I32, (Q_BLOCK, LANES), 1)
    per_group = LANES // MLA_ROPE
    parts = []
    for h in range(MLA_HEADS):
        g, r = divmod(h, per_group)
        pe = qpe_ref[:, g * LANES:(g + 1) * LANES]
        keep = (lane >= r * MLA_ROPE) & (lane < (r + 1) * MLA_ROPE)
        pe = jnp.where(keep, pe, jnp.zeros_like(pe))
        parts.append(jnp.concatenate([qlat_ref[:, h * KV_LORA:(h + 1) * KV_LORA], pe], axis=1))
    q_scr[...] = jnp.concatenate(parts, axis=0)
    rows = MLA_HEADS * Q_BLOCK
    m_scr[...] = jnp.full_like(m_scr, -jnp.inf)
    acc_scr[...] = jnp.zeros_like(acc_scr)
    ones = jnp.ones((KV_BLOCK, LANES), BF16)

    def block(j, masked):
        k0 = pl.multiple_of(j * KV_BLOCK, KV_BLOCK)
        kc = kcat_ref[pl.ds(k0, KV_BLOCK), :]
        s = _dot_nt(q_scr[...], kc)
        if masked:
            qpos = i * Q_BLOCK + lax.broadcasted_iota(I32, (rows, KV_BLOCK), 0) % Q_BLOCK
            kpos = k0 + lax.broadcasted_iota(I32, (rows, KV_BLOCK), 1)
            s = jnp.where(kpos <= qpos, s, -jnp.inf)
        s0, s1 = s[:, :LANES], s[:, LANES:]
        m_old = m_scr[...]
        m_new = jnp.maximum(m_old, jnp.max(jnp.maximum(s0, s1), axis=-1, keepdims=True))
        alpha = jnp.exp(m_old - m_new)
        p = jnp.concatenate([jnp.exp(s0 - m_new), jnp.exp(s1 - m_new)], axis=1).astype(BF16)
        vext = jnp.concatenate([kc[:, :KV_LORA], ones], axis=1)
        acc_scr[...] = jnp.concatenate([alpha, alpha], axis=1) * acc_scr[...] + _dot(p, vext)
        m_scr[...] = m_new

    n_full = (i * Q_BLOCK) // KV_BLOCK

    def body(j, carry):
        block(j, False)
        return carry

    lax.fori_loop(0, n_full, body, 0)
    block(n_full, True)
    o = acc_scr[:, :KV_LORA] / acc_scr[:, KV_LORA:]
    outs = [_dot(o[h * Q_BLOCK:(h + 1) * Q_BLOCK].astype(BF16), wv_ref[h]) for h in range(MLA_HEADS)]
    o_ref[...] = jnp.concatenate(outs, axis=1).astype(BF16)


def _attention_prompt(qlat, qpe, kcat, wv, b, l):
    nq = l // Q_BLOCK
    rows = MLA_HEADS * Q_BLOCK
    qmap = lambda bi, i: (bi * nq + i, 0)
    return pl.pallas_call(
        _attn_kernel,
        grid=(b, nq),
        in_specs=[pl.BlockSpec((Q_BLOCK, qlat.shape[1]), qmap),
                  pl.BlockSpec((Q_BLOCK, qpe.shape[1]), qmap),
                  pl.BlockSpec((l, kcat.shape[1]), lambda bi, i: (bi, 0)),
                  pl.BlockSpec(wv.shape, lambda bi, i: (0, 0, 0))],
        out_specs=pl.BlockSpec((Q_BLOCK, MLA_HEADS * MLA_V), qmap),
        out_shape=jax.ShapeDtypeStruct((b * l, MLA_HEADS * MLA_V), BF16),
        scratch_shapes=[pltpu.VMEM((rows, 2 * LANES), BF16), pltpu.VMEM((rows, LANES), F32),
                        pltpu.VMEM((rows, 2 * KV_LORA), F32)],
        compiler_params=_params("parallel", "arbitrary"),
        name="attention_prompt",
    )(qlat, qpe, kcat, wv)


def _attn_decode_kernel(pt_ref, ql_ref, qp_ref, cn_ref, kn_ref, lat_hbm, rope_hbm, o_ref,
                        kbuf, rbuf, sem):
    b = pl.program_id(0)
    nb = pl.num_programs(0)
    n_pages = pt_ref.shape[1]
    page = lat_hbm.shape[2]

    def copies(seq, slot, p):
        phys = pt_ref[seq, p]
        dst = pl.ds(p * page, page)
        return (pltpu.make_async_copy(lat_hbm.at[0, phys], kbuf.at[slot, dst], sem.at[slot, 0]),
                pltpu.make_async_copy(rope_hbm.at[0, phys], rbuf.at[slot, dst], sem.at[slot, 1]))

    def start_all(seq, slot):
        def body(p, carry):
            for cp in copies(seq, slot, p):
                cp.start()
            return carry
        lax.fori_loop(0, n_pages, body, 0)

    def wait_all(seq, slot):
        def body(p, carry):
            for cp in copies(seq, slot, p):
                cp.wait()
            return carry
        lax.fori_loop(0, n_pages, body, 0)

    slot = b % 2

    @pl.when(b == 0)
    def _():
        start_all(0, 0)

    @pl.when(b + 1 < nb)
    def _():
        start_all(b + 1, 1 - slot)

    wait_all(b, slot)
    ql = ql_ref[...]
    qp = qp_ref[...]
    kc = kbuf[slot].astype(BF16)
    kr = rbuf[slot].astype(BF16)
    s_past = _dot_nt(ql, kc) + _dot_nt(qp, kr)
    cn = cn_ref[...].astype(BF16).astype(F32)
    kn = kn_ref[...].astype(BF16).astype(F32)
    s_new = (jnp.sum(ql.astype(F32) * cn, axis=-1, keepdims=True)
             + jnp.sum(qp.astype(F32) * kn, axis=-1, keepdims=True))
    m = jnp.maximum(jnp.max(s_past, axis=-1, keepdims=True), s_new)
    p_past = jnp.exp(s_past - m)
    p_new = jnp.exp(s_new - m)
    denom = jnp.sum(p_past, axis=-1, keepdims=True) + p_new
    o = _dot(p_past.astype(BF16), kc) + p_new.astype(BF16).astype(F32) * cn
    o_ref[...] = o / denom


def _attention_decode(qlat, qpe, c_new, kpe_new, lat_cache, rope_cache, page_table):
    b = qlat.shape[0]
    h = MLA_HEADS
    n_pages = page_table.shape[1]
    page = lat_cache.shape[2]
    past = n_pages * page
    blk = lambda w: pl.BlockSpec((None, h, w), lambda i, pt: (i, 0, 0))
    one = lambda w: pl.BlockSpec((None, 1, w), lambda i, pt: (i, 0, 0))
    grid_spec = pltpu.PrefetchScalarGridSpec(
        num_scalar_prefetch=1,
        grid=(b,),
        in_specs=[blk(KV_LORA), blk(MLA_ROPE), one(KV_LORA), one(MLA_ROPE),
                  pl.BlockSpec(memory_space=pl.ANY), pl.BlockSpec(memory_space=pl.ANY)],
        out_specs=blk(KV_LORA),
        scratch_shapes=[pltpu.VMEM((2, past, KV_LORA), F32), pltpu.VMEM((2, past, MLA_ROPE), F32),
                        pltpu.SemaphoreType.DMA((2, 2))],
    )
    return pl.pallas_call(
        _attn_decode_kernel,
        grid_spec=grid_spec,
        out_shape=jax.ShapeDtypeStruct((b, h, KV_LORA), F32),
        compiler_params=_params("arbitrary"),
        name="attention_decode",
    )(page_table, qlat.reshape(b, h, KV_LORA), qpe.reshape(b, h, MLA_ROPE),
      c_new.reshape(b, 1, KV_LORA), kpe_new.reshape(b, 1, MLA_ROPE), lat_cache, rope_cache)


def _route(logits):
    lane = lax.broadcasted_iota(I32, logits.shape, 1)
    lane_f = lane.astype(F32)
    neg = -jnp.inf
    big = float(LANES)
    gmask = lane < N_GROUPS
    gl = jnp.where(gmask, logits, neg)
    gmax = jnp.max(gl, axis=-1, keepdims=True)
    g_sel = jnp.min(jnp.where(gl == gmax, lane_f, big), axis=-1, keepdims=True)
    gsum = jnp.sum(jnp.where(gmask, jnp.exp(gl - gmax), 0.0), axis=-1, keepdims=True)
    g_w = 1.0 / gsum
    lo = N_GROUPS + EXPERTS_PER_GROUP * g_sel
    emask = (lane_f >= lo) & (lane_f < lo + EXPERTS_PER_GROUP)
    el = jnp.where(emask, logits, neg)
    v1 = jnp.max(el, axis=-1, keepdims=True)
    i1 = jnp.min(jnp.where(el == v1, lane_f, big), axis=-1, keepdims=True)
    el2 = jnp.where(lane_f == i1, neg, el)
    v2 = jnp.max(el2, axis=-1, keepdims=True)
    i2 = jnp.min(jnp.where(el2 == v2, lane_f, big), axis=-1, keepdims=True)
    e2 = jnp.exp(v2 - v1)
    w1 = g_w / (1.0 + e2)
    w2 = g_w * e2 / (1.0 + e2)
    return i1 - N_GROUPS, i2 - N_GROUPS, w1, w2


def _bf16_parts(w):
    hi = w.astype(BF16).astype(F32)
    mid = (w - hi).astype(BF16).astype(F32)
    return hi, mid, w - hi - mid


def _lane_pack(shape, cols):
    lane = lax.broadcasted_iota(I32, shape, 1)
    out = jnp.zeros(shape, F32)
    for k, col in enumerate(cols):
        out = jnp.where(lane == k, col, out)
    return out


def _mix_out_kernel(*refs, n_act, has_pre, has_bias):
    refs = list(refs)
    acts = [refs.pop(0) for _ in range(n_act)]
    ws = [refs.pop(0) for _ in range(n_act)]
    pre = refs.pop(0) if has_pre else None
    bias = refs.pop(0) if has_bias else None
    h_ref, nw_ref, rwh_ref, rwl_ref, rb_ref, h1_ref, xe_ref, rid_ref, cnt_ref = refs
    out = None
    for k, (a_ref, w_ref) in enumerate(zip(acts, ws)):
        a = a_ref[...]
        if has_pre and k == n_act - 1:
            a = _dot(a.astype(BF16), pre[...])
        t = _dot(a.astype(BF16), w_ref[...])
        out = t if out is None else out + t
    if has_bias:
        out = out + bias[...]
    h1 = h_ref[...] + out
    h1_ref[...] = h1
    xn = _rms(h1, nw_ref[...])
    xh = xn.astype(BF16)
    xl = (xn - xh.astype(F32)).astype(BF16)
    logits = _dot(xh, rwh_ref[...]) + _dot(xl, rwh_ref[...]) + _dot(xh, rwl_ref[...]) + rb_ref[...]
    e1, e2, w1, w2 = _route(logits)
    meta = _lane_pack(logits.shape, (e1, e2) + _bf16_parts(w1) + _bf16_parts(w2))
    xe_ref[...] = jnp.concatenate([xn, meta], axis=1).astype(BF16)
    rid_ref[...] = _lane_pack(logits.shape, (e1, e2)).astype(I32)
    lane_f = lax.broadcasted_iota(I32, logits.shape, 1).astype(F32)
    hit = jnp.where((lane_f == e1) | (lane_f == e2), 1.0, 0.0)
    cnt_ref[...] = jnp.sum(hit, axis=0, keepdims=True)


def _mix_out(acts, ws, h, nw, rw, rb, tm, pre=None, bias=None):
    n, d = h.shape
    row = lambda i: (i, 0)
    const = lambda i: (0, 0)
    ins = list(acts) + list(ws)
    specs = [pl.BlockSpec((tm, a.shape[1]), row) for a in acts] + [pl.BlockSpec(w.shape, const) for w in ws]
    if pre is not None:
        ins.append(pre)
        specs.append(pl.BlockSpec(pre.shape, const))
    if bias is not None:
        ins.append(bias)
        specs.append(pl.BlockSpec(bias.shape, const))
    ins += [h, nw, rw[0], rw[1], rb]
    specs += [pl.BlockSpec((tm, d), row), pl.BlockSpec((1, d), const),
              pl.BlockSpec(rw[0].shape, const), pl.BlockSpec(rw[1].shape, const), pl.BlockSpec(rb.shape, const)]
    kern = functools.partial(_mix_out_kernel, n_act=len(acts), has_pre=pre is not None,
                             has_bias=bias is not None)
    outs = pl.pallas_call(
        kern,
        grid=(n // tm,),
        in_specs=specs,
        out_specs=[pl.BlockSpec((tm, d), row), pl.BlockSpec((tm, d + LANES), row),
                   pl.BlockSpec((tm, LANES), row), pl.BlockSpec((None, 1, LANES), lambda i: (i, 0, 0))],
        out_shape=[jax.ShapeDtypeStruct((n, d), F32), jax.ShapeDtypeStruct((n, d + LANES), BF16),
                   jax.ShapeDtypeStruct((n, LANES), I32), jax.ShapeDtypeStruct((n // tm, 1, LANES), F32)],
        compiler_params=_params("parallel"),
        name="mix_out_route",
    )(*ins)
    return tuple(outs)


SEG_ALIGN = 16
FFN_ROWS = 256


def _seg_bits(t):
    bits = []
    b = SEG_ALIGN
    while b <= t:
        bits.append(b)
        b *= 2
    return tuple(reversed(bits))


def _slot_cap(t):
    cap = 2 * t + N_EXPERTS * (SEG_ALIGN - 1)
    return -(-cap // FFN_ROWS) * FFN_ROWS


def _moe_meta(counts, tile_tokens):
    cnt = counts[:, :N_EXPERTS].astype(I32)
    ell = ((cnt + SEG_ALIGN - 1) // SEG_ALIGN) * SEG_ALIGN
    off = jnp.cumsum(ell, axis=1) - ell
    tot = jnp.sum(ell, axis=0)
    region = ((tot + FFN_ROWS - 1) // FFN_ROWS) * FFN_ROWS
    gend = jnp.cumsum(region)
    gbase = gend - region
    gseg = gbase[None, :] + jnp.cumsum(ell, axis=0) - ell
    n_pairs = 2 * sum(tile_tokens)
    n_rows_max = n_pairs + len(tile_tokens) * N_EXPERTS * (SEG_ALIGN - 1) + N_EXPERTS * (FFN_ROWS - 1)
    n_steps = -(-n_rows_max // FFN_ROWS)
    starts = jnp.arange(n_steps, dtype=I32) * FFN_ROWS
    step_expert = jnp.minimum(jnp.sum((gend[None, :] <= starts[:, None]).astype(I32), axis=1), N_EXPERTS - 1)
    n_used = (gend[-1] // FFN_ROWS).reshape(1)
    off_f = off.astype(F32)
    return dict(off=off, ell=ell, gseg=gseg, tail_start=gbase + tot, tail_len=region - tot, end=gend[-1:],
                step_expert=step_expert, n_used=n_used, n_rows=n_steps * FFN_ROWS,
                off_col=off_f[:, :, None], off_row=jnp.pad(off_f, ((0, 0), (0, LANES - N_EXPERTS)))[:, None, :])


def _chunk_copies(n, bits, make):
    for b in bits:
        hi = n & (-(2 * b))

        @pl.when((n & b) != 0)
        def _():
            make(hi, b)


def _pair_slots_rows(ids_t, off_col, t):
    sub = lax.broadcasted_iota(I32, (N_EXPERTS, t), 0).astype(F32)
    e1 = jnp.where(sub == ids_t[0:1, :], 1.0, 0.0)
    e2 = jnp.where(sub == ids_t[1:2, :], 1.0, 0.0)
    before = jnp.where(lax.broadcasted_iota(I32, (t, t), 0) < lax.broadcasted_iota(I32, (t, t), 1), 1.0, 0.0)
    before = before.astype(BF16)
    cum1 = _dot(e1.astype(BF16), before)
    cum2 = _dot(e2.astype(BF16), before)
    cnt1 = jnp.sum(e1, axis=1, keepdims=True)
    slot0 = jnp.sum(e1 * (off_col + cum1), axis=0, keepdims=True)
    slot1 = jnp.sum(e2 * (off_col + cnt1 + cum2), axis=0, keepdims=True)
    return slot0, slot1


def _pair_slots_cols(ids, off_row, t):
    lane = lax.broadcasted_iota(I32, (t, LANES), 1).astype(F32)
    idf = ids.astype(F32)
    e1 = jnp.where(lane == idf[:, 0:1], 1.0, 0.0)
    e2 = jnp.where(lane == idf[:, 1:2], 1.0, 0.0)
    before = jnp.where(lax.broadcasted_iota(I32, (t, t), 1) < lax.broadcasted_iota(I32, (t, t), 0), 1.0, 0.0)
    before = before.astype(BF16)
    cum1 = _dot(before, e1.astype(BF16))
    cum2 = _dot(before, e2.astype(BF16))
    cnt1 = jnp.sum(e1, axis=0, keepdims=True)
    slot0 = jnp.sum(e1 * (off_row + cum1), axis=1, keepdims=True)
    slot1 = jnp.sum(e2 * (off_row + cnt1 + cum2), axis=1, keepdims=True)
    return slot0, slot1


def _permute_kernel(off_ref, ell_ref, gseg_ref, tstart_ref, tlen_ref, end_ref, xe_ref, ids_ref, offcol_ref, *rest,
                    tile0, zero_tiles, bits):
    xg_hbm, xs_scr, zero_scr, sem = rest[-4:]
    s = pl.program_id(0)
    last = pl.num_programs(0) - 1
    t = xe_ref.shape[0]
    cap = xs_scr.shape[1]
    slot = s % 2

    def seg_copies(tile, buf, fn):
        def body(e, carry):
            o = off_ref[tile, e]
            g = gseg_ref[tile, e]

            def make(hi, b):
                src = xs_scr.at[buf, pl.ds(pl.multiple_of(o + hi, SEG_ALIGN), b)]
                dst = xg_hbm.at[pl.ds(pl.multiple_of(g + hi, SEG_ALIGN), b)]
                fn(pltpu.make_async_copy(src, dst, sem.at[buf]))

            _chunk_copies(ell_ref[tile, e], bits, make)
            return carry

        lax.fori_loop(0, N_EXPERTS, body, 0)

    if zero_tiles is not None:
        zrows = zero_scr.shape[0]

        def zero_fill(g, n, fn):
            def make(hi, b):
                dst = xg_hbm.at[pl.ds(pl.multiple_of(g + hi, SEG_ALIGN), b)]
                fn(pltpu.make_async_copy(zero_scr.at[pl.ds(0, b)], dst, sem.at[2]))

            _chunk_copies(n, _seg_bits(zrows), make)

        def zero_copies(fn):
            def body(e, carry):
                zero_fill(tstart_ref[e], tlen_ref[e], fn)
                for zt in zero_tiles:
                    zero_fill(gseg_ref[zt, e], ell_ref[zt, e], fn)
                return carry

            lax.fori_loop(0, N_EXPERTS, body, 0)

            def trailing(k, carry):
                dst = xg_hbm.at[pl.ds(pl.multiple_of(end_ref[0] + k * zrows, SEG_ALIGN), zrows)]
                fn(pltpu.make_async_copy(zero_scr, dst, sem.at[2]))
                return carry

            n_trailing = lax.shift_right_logical(xg_hbm.shape[0] - end_ref[0], zrows.bit_length() - 1)
            lax.fori_loop(0, n_trailing, trailing, 0)

        @pl.when(s == 0)
        def _():
            zero_scr[...] = jnp.zeros_like(zero_scr)
            zero_copies(lambda cp: cp.start())
            zero_copies(lambda cp: cp.wait())

    @pl.when(s >= 2)
    def _():
        seg_copies(tile0 + s - 2, slot, lambda cp: cp.wait())

    tile = tile0 + s
    ids_t = ids_ref[...].astype(F32).T
    slot0, slot1 = _pair_slots_rows(ids_t, offcol_ref[...], t)
    used = off_ref[tile, N_EXPERTS - 1] + ell_ref[tile, N_EXPERTS - 1]
    xe = xe_ref[...]
    for r0 in range(0, cap, FFN_ROWS):
        @pl.when(r0 < used)
        def _():
            rows = (lax.broadcasted_iota(I32, (FFN_ROWS, t), 0) + r0).astype(F32)
            pm = jnp.where((rows == slot0) | (rows == slot1), 1.0, 0.0).astype(BF16)
            xs_scr[slot, pl.ds(r0, FFN_ROWS), :] = _dot(pm, xe).astype(BF16)

    seg_copies(tile, slot, lambda cp: cp.start())

    @pl.when(s == last)
    def _():
        @pl.when(s >= 1)
        def _():
            seg_copies(tile - 1, 1 - slot, lambda cp: cp.wait())

        seg_copies(tile, slot, lambda cp: cp.wait())


def _moe_permute(meta, xe, ids, t, tile0, xg_prev=None):
    n, de = xe.shape
    cap = _slot_cap(t)
    n_tiles = meta["off"].shape[0]
    zero_tiles = None if xg_prev is not None else tuple(range(tile0 + n // t, n_tiles))
    prefetch = [meta["off"], meta["ell"], meta["gseg"], meta["tail_start"], meta["tail_len"], meta["end"]]
    imap = lambda s, *_: (s, 0)
    in_specs = [pl.BlockSpec((t, de), imap), pl.BlockSpec((t, LANES), imap),
                pl.BlockSpec((None, N_EXPERTS, 1), lambda s, *_: (tile0 + s, 0, 0))]
    ins = [xe, ids, meta["off_col"]]
    aliases = {}
    if xg_prev is not None:
        in_specs.append(pl.BlockSpec(memory_space=pl.ANY))
        ins.append(xg_prev)
        aliases = {len(prefetch) + len(ins) - 1: 0}
    grid_spec = pltpu.PrefetchScalarGridSpec(
        num_scalar_prefetch=len(prefetch),
        grid=(n // t,),
        in_specs=in_specs,
        out_specs=pl.BlockSpec(memory_space=pl.ANY),
        scratch_shapes=[pltpu.VMEM((2, cap, de), BF16), pltpu.VMEM((FFN_ROWS // 2, de), BF16),
                        pltpu.SemaphoreType.DMA((3,))],
    )
    return pl.pallas_call(
        functools.partial(_permute_kernel, tile0=tile0, zero_tiles=zero_tiles, bits=_seg_bits(t)),
        grid_spec=grid_spec,
        out_shape=jax.ShapeDtypeStruct((meta["n_rows"], de), BF16),
        input_output_aliases=aliases,
        compiler_params=_params("arbitrary"),
        name="moe_permute",
    )(*prefetch, *ins)


def _ffn_kernel(te_ref, nu_ref, x_ref, wg_ref, wu_ref, wd_ref, o_ref, wg_b, wu_b, wd_b):
    i = pl.program_id(0)
    e = te_ref[i]
    d = o_ref.shape[1]

    @pl.when((i == 0) | (e != te_ref[jnp.maximum(i - 1, 0)]))
    def _():
        wg_b[...] = wg_ref[...].astype(BF16)
        wu_b[...] = wu_ref[...].astype(BF16)
        wd_b[...] = wd_ref[...].astype(BF16)

    @pl.when(i < nu_ref[0])
    def _():
        x = x_ref[:, :d]
        meta = x_ref[:, d:].astype(F32)
        w1 = meta[:, 2:3] + meta[:, 3:4] + meta[:, 4:5]
        w2 = meta[:, 5:6] + meta[:, 6:7] + meta[:, 7:8]
        w = jnp.where(meta[:, 0:1] == e.astype(F32), w1, w2)
        g = _dot(x, wg_b[...])
        u = _dot(x, wu_b[...])
        hmid = (g * _sigmoid(g) * u).astype(BF16)
        o_ref[...] = (_dot(hmid, wd_b[...]) * w).astype(BF16)

    @pl.when(i >= nu_ref[0])
    def _():
        o_ref[...] = jnp.zeros_like(o_ref)


def _moe_ffn(xs, tile_expert, n_used, wg, wu, wd, tr):
    n_rows, d = xs.shape[0], wg.shape[1]
    de = wg.shape[2]
    grid_spec = pltpu.PrefetchScalarGridSpec(
        num_scalar_prefetch=2,
        grid=(n_rows // tr,),
        in_specs=[pl.BlockSpec((tr, xs.shape[1]), lambda i, te, nu: (i, 0)),
                  pl.BlockSpec((None, d, de), lambda i, te, nu: (te[i], 0, 0)),
                  pl.BlockSpec((None, d, de), lambda i, te, nu: (te[i], 0, 0)),
                  pl.BlockSpec((None, de, d), lambda i, te, nu: (te[i], 0, 0))],
        out_specs=pl.BlockSpec((tr, d), lambda i, te, nu: (i, 0)),
        scratch_shapes=[pltpu.VMEM((d, de), BF16), pltpu.VMEM((d, de), BF16), pltpu.VMEM((de, d), BF16)],
    )
    return pl.pallas_call(
        _ffn_kernel,
        grid_spec=grid_spec,
        out_shape=jax.ShapeDtypeStruct((n_rows, d), BF16),
        compiler_params=_params("arbitrary"),
        name="moe_ffn",
    )(tile_expert, n_used, xs, wg, wu, wd)


def _unpermute_kernel(off_ref, ell_ref, gseg_ref, ids_ref, offrow_ref, h_ref, yg_hbm, o_ref, ys_scr, sem,
                      *, tile0, bits):
    s = pl.program_id(0)
    t = h_ref.shape[0]
    cap = ys_scr.shape[1]
    slot = s % 2

    def seg_copies(tile, buf, fn):
        def body(e, carry):
            o = off_ref[tile, e]
            g = gseg_ref[tile, e]

            def make(hi, b):
                src = yg_hbm.at[pl.ds(pl.multiple_of(g + hi, SEG_ALIGN), b)]
                dst = ys_scr.at[buf, pl.ds(pl.multiple_of(o + hi, SEG_ALIGN), b)]
                fn(pltpu.make_async_copy(src, dst, sem.at[buf]))

            _chunk_copies(ell_ref[tile, e], bits, make)
            return carry

        lax.fori_loop(0, N_EXPERTS, body, 0)

    tile = tile0 + s

    @pl.when(s == 0)
    def _():
        ys_scr[...] = jnp.zeros_like(ys_scr)
        seg_copies(tile, slot, lambda cp: cp.start())

    @pl.when(s + 1 < pl.num_programs(0))
    def _():
        seg_copies(tile + 1, 1 - slot, lambda cp: cp.start())

    seg_copies(tile, slot, lambda cp: cp.wait())
    slot0, slot1 = _pair_slots_cols(ids_ref[...], offrow_ref[...], t)
    cols = lax.broadcasted_iota(I32, (t, cap), 1).astype(F32)
    pu = jnp.where((cols == slot0) | (cols == slot1), 1.0, 0.0).astype(BF16)
    o_ref[...] = h_ref[...] + _dot(pu, ys_scr[slot])


def _moe_unpermute(meta, h, ids, yg, t, tile0):
    n, d = h.shape
    cap = _slot_cap(t)
    prefetch = [meta["off"], meta["ell"], meta["gseg"]]
    imap = lambda s, *_: (s, 0)
    grid_spec = pltpu.PrefetchScalarGridSpec(
        num_scalar_prefetch=len(prefetch),
        grid=(n // t,),
        in_specs=[pl.BlockSpec((t, LANES), imap),
                  pl.BlockSpec((None, 1, LANES), lambda s, *_: (tile0 + s, 0, 0)),
                  pl.BlockSpec((t, d), imap), pl.BlockSpec(memory_space=pl.ANY)],
        out_specs=pl.BlockSpec((t, d), imap),
        scratch_shapes=[pltpu.VMEM((2, cap, d), BF16), pltpu.SemaphoreType.DMA((2,))],
    )
    return pl.pallas_call(
        functools.partial(_unpermute_kernel, tile0=tile0, bits=_seg_bits(t)),
        grid_spec=grid_spec,
        out_shape=jax.ShapeDtypeStruct((n, d), F32),
        compiler_params=_params("arbitrary"),
        name="moe_unpermute",
    )(*prefetch, ids, meta["off_row"], h, yg)


def _moe(groups, wg, wu, wd):
    tiles = [t for g in groups for t in [g[4]] * (g[0].shape[0] // g[4])]
    meta = _moe_meta(jnp.concatenate([g[3].reshape(-1, LANES) for g in groups], axis=0), tiles)
    xg = None
    tile0 = 0
    for h1, xe, ids, _, t in groups:
        xg = _moe_permute(meta, xe, ids, t, tile0, xg)
        tile0 += h1.shape[0] // t
    yg = _moe_ffn(xg, meta["step_expert"], meta["n_used"], wg, wu, wd, FFN_ROWS)
    outs = []
    tile0 = 0
    for h1, xe, ids, _, t in groups:
        outs.append(_moe_unpermute(meta, h1, ids, yg, t, tile0))
        tile0 += h1.shape[0] // t
    return outs


def _ple_kernel(*refs, tail):
    h_ref, p_ref, proj_ref, pn_ref, gate_ref = refs[:5]
    h = h_ref[...]
    pp = _dot(p_ref[...].astype(BF16), proj_ref[...])
    g = _dot(_rms(h, pn_ref[...]).astype(BF16), gate_ref[...])
    h3 = h + pp * _sigmoid(g)
    if tail == "conv_in":
        nm_ref, pw1_ref, pb1_ref, h3_ref, a_ref = refs[5:]
        h3_ref[...] = h3
        d = h3.shape[1]
        hn = _rms(h3, nm_ref[...]).astype(BF16)
        ua = _dot(hn, pw1_ref[:, :d]) + pb1_ref[:, :d]
        ub = _dot(hn, pw1_ref[:, d:]) + pb1_ref[:, d:]
        a_ref[...] = ua * _sigmoid(ub)
    else:
        nf_ref, y_ref = refs[5:]
        y_ref[...] = _rms(h3, nf_ref[...])


def _ple(h, p, proj, pn, gate, tm, tail, extra):
    n, d = h.shape
    row = lambda i: (i, 0)
    const = lambda i: (0, 0)
    ins = [h, p, proj, pn, gate] + list(extra)
    specs = [pl.BlockSpec((tm, d), row), pl.BlockSpec((tm, p.shape[1]), row),
             pl.BlockSpec(proj.shape, const), pl.BlockSpec((1, d), const),
             pl.BlockSpec(gate.shape, const)] + [pl.BlockSpec(e.shape, const) for e in extra]
    n_out = 2 if tail == "conv_in" else 1
    return pl.pallas_call(
        functools.partial(_ple_kernel, tail=tail),
        grid=(n // tm,),
        in_specs=specs,
        out_specs=[pl.BlockSpec((tm, d), row)] * n_out,
        out_shape=[jax.ShapeDtypeStruct((n, d), F32)] * n_out,
        compiler_params=_params("parallel"),
        name="ple_" + tail,
    )(*ins)


def _ln_swish(y, w, b):
    mu = jnp.mean(y, axis=-1, keepdims=True)
    var = jnp.mean(jnp.square(y - mu), axis=-1, keepdims=True)
    z = (y - mu) * lax.rsqrt(var + EPS) * w + b
    return z * _sigmoid(z)


def _conv_kernel(a_ref, dw_ref, dwb_ref, lnw_ref, lnb_ref, o_ref, ext, acc):
    t = pl.program_id(1)
    tl = a_ref.shape[0]
    halo = ext.shape[0] - tl

    @pl.when(t == 0)
    def _():
        ext[0:halo, :] = jnp.zeros((halo, ext.shape[1]), F32)

    ext[halo:, :] = a_ref[...]
    first = halo - (CONV_WIDTH - 1)
    for c0 in range(0, ext.shape[1], LANES):
        cs = slice(c0, c0 + LANES)
        s = ext[first:first + tl, cs] * dw_ref[0:1, cs]
        for j in range(1, CONV_WIDTH):
            s = s + ext[first + j:first + j + tl, cs] * dw_ref[j:j + 1, cs]
        acc[:, cs] = s
    ext[0:halo, :] = ext[tl:tl + halo, :]
    o_ref[...] = _ln_swish(acc[...] + dwb_ref[...], lnw_ref[...], lnb_ref[...]).astype(BF16)


def _conv_prompt(a, dw, dwb, lnw, lnb, b, l, tl=256):
    d = a.shape[1]
    nt = l // tl
    halo = 32
    row = lambda bi, t: (bi * nt + t, 0)
    const = lambda bi, t: (0, 0)
    return pl.pallas_call(
        _conv_kernel,
        grid=(b, nt),
        in_specs=[pl.BlockSpec((tl, d), row), pl.BlockSpec(dw.shape, const),
                  pl.BlockSpec((1, d), const), pl.BlockSpec((1, d), const), pl.BlockSpec((1, d), const)],
        out_specs=pl.BlockSpec((tl, d), row),
        out_shape=jax.ShapeDtypeStruct((b * l, d), BF16),
        scratch_shapes=[pltpu.VMEM((tl + halo, d), F32), pltpu.VMEM((tl, d), F32)],
        compiler_params=_params("parallel", "arbitrary"),
        name="conv_prompt",
    )(a, dw, dwb, lnw, lnb)


def _conv_decode_kernel(buf_ref, a_ref, dw_ref, dwb_ref, lnw_ref, lnb_ref, y_ref, new_ref):
    w = CONV_WIDTH - 1
    buf = buf_ref[...]
    a = a_ref[...]
    y = jnp.sum(buf * dw_ref[0:w, :][None], axis=1, keepdims=True) + a * dw_ref[w:w + 1, :][None]
    y_ref[...] = _ln_swish(y + dwb_ref[...][None], lnw_ref[...][None], lnb_ref[...][None])
    new_ref[:, 0:w - 1, :] = buf[:, 1:w, :]
    new_ref[:, w - 1:w, :] = a


def _conv_decode(buf, a, dw, dwb, lnw, lnb, bb=8):
    _, b, w, d = buf.shape
    blk = lambda r: pl.BlockSpec((bb, r, d), lambda i: (i, 0, 0))
    state = pl.BlockSpec((None, bb, w, d), lambda i: (0, i, 0, 0))
    const = lambda i: (0, 0)
    y, new = pl.pallas_call(
        _conv_decode_kernel,
        grid=(b // bb,),
        in_specs=[state, blk(1), pl.BlockSpec(dw.shape, const), pl.BlockSpec((1, d), const),
                  pl.BlockSpec((1, d), const), pl.BlockSpec((1, d), const)],
        out_specs=[blk(1), state],
        out_shape=[jax.ShapeDtypeStruct((b, 1, d), F32), jax.ShapeDtypeStruct((1, b, w, d), F32)],
        compiler_params=_params("parallel"),
        name="conv_decode",
    )(buf, a.reshape(b, 1, d), dw, dwb, lnw, lnb)
    return y.reshape(b, d), new


def _prep_weights(W):
    P = {}
    w_in = W["w_in"][0]
    kpe_cols = w_in[:, -MLA_ROPE:]
    P["w_in"] = jnp.concatenate([w_in[:, :-MLA_ROPE]] + [kpe_cols] * (LANES // MLA_ROPE), axis=1).astype(BF16)
    w_uq = W["w_uq"][0]
    nope = jnp.pad(w_uq[..., :MLA_NOPE], ((0, 0), (0, 0), (0, LANES - MLA_NOPE)))
    P["w_uq_nope"] = nope.reshape(Q_LORA, MLA_HEADS * LANES).astype(BF16)
    P["w_uq_pe"] = w_uq[..., MLA_NOPE:].reshape(Q_LORA, MLA_HEADS * MLA_ROPE).astype(BF16)
    w_ukv = W["w_ukv"][0]
    wk = jnp.transpose(w_ukv[..., :MLA_NOPE], (1, 2, 0))
    P["wk"] = jnp.pad(wk, ((0, 0), (0, LANES - MLA_NOPE), (0, 0))).astype(BF16)
    wv = jnp.transpose(w_ukv[..., MLA_NOPE:], (1, 0, 2))
    P["wv"] = wv.astype(BF16)
    eye = jnp.eye(MLA_HEADS, dtype=F32)
    P["wv_bd"] = (wv[:, :, None, :] * eye[:, None, :, None]).reshape(
        MLA_HEADS * KV_LORA, MLA_HEADS * MLA_V).astype(BF16)
    w_out = W["w_out"][0].astype(BF16)
    P["w_out_r"] = w_out[:RET_HEADS * RET_DV]
    P["w_out_m"] = w_out[RET_HEADS * RET_DV:]
    d = w_out.shape[1]
    for i in range(W["moe_w_group"].shape[0]):
        rw = jnp.concatenate([W["moe_w_group"][i], W["moe_w_router"][i].reshape(d, N_EXPERTS)], axis=1)
        rb = jnp.concatenate([W["moe_b_group"][i], W["moe_b_router"][i].reshape(N_EXPERTS)])
        padc = LANES - rw.shape[1]
        rw = jnp.pad(rw, ((0, 0), (0, padc)))
        rw_hi = rw.astype(BF16)
        P["router_w", i] = (rw_hi, (rw - rw_hi.astype(F32)).astype(BF16))
        P["router_b", i] = jnp.pad(rb, (0, padc)).reshape(1, LANES)
        P["ple_proj", i] = W["ple_proj"][i].astype(BF16)
        P["ple_gate", i] = W["ple_gate"][i].astype(BF16)
    P["pw1"] = W["conv_pw1"][0].astype(BF16)
    P["pw2"] = W["conv_pw2"][0].astype(BF16)
    P["dw"] = jnp.pad(W["conv_dw"][0], ((0, 1), (0, 0)))
    return P


def _vec(v):
    return v.reshape(1, -1)


def _mixer0(x, pos, W, P, tm, past):
    b, l, d = x.shape
    n = b * l
    h = x.reshape(n, d)
    tabs = _rope_tables(pos, RET_DK) + _rope_tables(pos, MLA_ROPE)
    rq, rk, rv, rg, qlat, qpe, c, kpe, kcat = _proj_in(
        h, _vec(W["norm_mix"][0]), P["w_in"], tabs, _vec(W["mla_q_norm"][0]), _vec(W["mla_kv_norm"][0]),
        P["w_uq_nope"], P["w_uq_pe"], P["wk"], tm)
    gnw = _vec(W["ret_gn_w"][0])
    if past is None:
        o_r, s_new = _retention_prompt(rq, rk, rv, rg, gnw, b, l)
        o_m = _attention_prompt(qlat, qpe, kcat, P["wv"], b, l)
        pre = None
    else:
        ret_state, lat_cache, rope_cache, page_table = past
        o_r, s_new = _retention_decode(rq, rk, rv, rg, gnw, ret_state)
        o_m = _attention_decode(qlat, qpe, c, kpe, lat_cache, rope_cache, page_table)
        o_m = o_m.reshape(n, MLA_HEADS * KV_LORA)
        pre = P["wv_bd"]
    routed = _mix_out([o_r, o_m], [P["w_out_r"], P["w_out_m"]], h, _vec(W["norm_ffn"][0]),
                      P["router_w", 0], P["router_b", 0], tm, pre=pre)
    return routed, (c.reshape(1, b, l, KV_LORA), kpe.reshape(1, b, l, MLA_ROPE), s_new)


def _mixer1(h2, p0, W, P, tm, b, l, conv_state):
    n, d = h2.shape
    h3, a = _ple(h2, p0.reshape(n, -1), P["ple_proj", 0], _vec(W["ple_norm"][0]), P["ple_gate", 0], tm,
                 "conv_in", [_vec(W["norm_mix"][1]), P["pw1"], _vec(W["conv_pw1_b"][0])])
    conv_vecs = (_vec(W["conv_dw_b"][0]), _vec(W["conv_ln_w"][0]), _vec(W["conv_ln_b"][0]))
    if conv_state is None:
        yact = _conv_prompt(a, P["dw"], *conv_vecs, b, l)
        new_state = a.reshape(b, l, d)[:, l - (CONV_WIDTH - 1):, :][None]
    else:
        yact, new_state = _conv_decode(conv_state, a, P["dw"], *conv_vecs)
    routed = _mix_out([yact], [P["pw2"]], h3, _vec(W["norm_ffn"][1]),
                      P["router_w", 1], P["router_b", 1], tm, bias=_vec(W["conv_pw2_b"][0]))
    return routed, new_state


def _final(h5, p1, W, P, tm, shape):
    n = h5.shape[0]
    (y,) = _ple(h5, p1.reshape(n, -1), P["ple_proj", 1], _vec(W["ple_norm"][1]), P["ple_gate", 1], tm,
                "final", [_vec(W["norm_final"])])
    return y.reshape(shape)


def kernel(x_prompt, x_sample, p_prompt, p_sample, cache_mla_latent, cache_mla_rope, state_retention, state_conv,
           page_table, norm_mix, norm_ffn, norm_final, w_in, ret_gn_w, mla_q_norm, mla_kv_norm, w_uq, w_ukv, w_out,
           conv_pw1, conv_pw1_b, conv_dw, conv_dw_b, conv_ln_w, conv_ln_b, conv_pw2, conv_pw2_b,
           moe_w_group, moe_b_group, moe_w_router, moe_b_router, moe_w_gate, moe_w_up, moe_w_down,
           ple_proj, ple_norm, ple_gate):
    W = dict(norm_mix=norm_mix, norm_ffn=norm_ffn, norm_final=norm_final, w_in=w_in, ret_gn_w=ret_gn_w,
             mla_q_norm=mla_q_norm, mla_kv_norm=mla_kv_norm, w_uq=w_uq, w_ukv=w_ukv, w_out=w_out,
             conv_pw1=conv_pw1, conv_pw1_b=conv_pw1_b, conv_dw=conv_dw, conv_dw_b=conv_dw_b, conv_ln_w=conv_ln_w,
             conv_ln_b=conv_ln_b, conv_pw2=conv_pw2, conv_pw2_b=conv_pw2_b, moe_w_group=moe_w_group,
             moe_b_group=moe_b_group, moe_w_router=moe_w_router, moe_b_router=moe_b_router, moe_w_gate=moe_w_gate,
             moe_w_up=moe_w_up, moe_w_down=moe_w_down, ple_proj=ple_proj, ple_norm=ple_norm, ple_gate=ple_gate)
    assert w_in.shape[0] == 1 and conv_pw1.shape[0] == 1, "one retention/attention layer, one conv layer"
    P = _prep_weights(W)
    past_len = page_table.shape[1] * cache_mla_latent.shape[2]
    pos_prompt = jnp.arange(x_prompt.shape[1])
    pos_sample = past_len + jnp.arange(x_sample.shape[1])
    bp, lp, _ = x_prompt.shape
    bs, ls, _ = x_sample.shape
    tm_p = min(512, lp)
    tm_s = bs * ls
    moe_w = lambda i: (moe_w_gate[i], moe_w_up[i], moe_w_down[i])
    routed_p, (lat_p, rope_p, ret_p) = _mixer0(x_prompt, pos_prompt, W, P, tm_p, None)
    routed_s, (lat_s, rope_s, ret_s) = _mixer0(
        x_sample, pos_sample, W, P, tm_s, (state_retention, cache_mla_latent, cache_mla_rope, page_table))
    h2_p, h2_s = _moe([routed_p + (tm_p,), routed_s + (tm_s,)], *moe_w(0))
    routed_p, conv_p = _mixer1(h2_p, p_prompt[0], W, P, tm_p, bp, lp, None)
    routed_s, conv_s = _mixer1(h2_s, p_sample[0], W, P, tm_s, bs, ls, state_conv)
    h5_p, h5_s = _moe([routed_p + (tm_p,), routed_s + (tm_s,)], *moe_w(1))
    y_p = _final(h5_p, p_prompt[1], W, P, tm_p, x_prompt.shape)
    y_s = _final(h5_s, p_sample[1], W, P, tm_s, x_sample.shape)
    return (y_p, y_s, lat_p, rope_p, ret_p, conv_p, lat_s, rope_s, ret_s, conv_s)
```

```python
import functools

import jax
import jax.numpy as jnp
from jax import lax
from jax.experimental import pallas as pl
from jax.experimental.pallas import tpu as pltpu

F32 = jnp.float32
BF16 = jnp.bfloat16
I32 = jnp.int32

EPS = 1e-6
ROPE_THETA = 10000.0
RET_HEADS = 8
RET_DK = 64
RET_DV = 64
RET_CHUNK = 128
MLA_HEADS = 8
MLA_NOPE = 64
MLA_ROPE = 32
MLA_V = 64
Q_LORA = 256
KV_LORA = 128
CONV_WIDTH = 31
N_GROUPS = 4
EXPERTS_PER_GROUP = 8
N_EXPERTS = N_GROUPS * EXPERTS_PER_GROUP
LANES = 128
Q_BLOCK = 128
KV_BLOCK = 256
VMEM_LIMIT_BYTES = 56 * 1024 * 1024


def _params(*sem):
    return pltpu.CompilerParams(dimension_semantics=sem, vmem_limit_bytes=VMEM_LIMIT_BYTES)


def _rms(x, w):
    return x * lax.rsqrt(jnp.mean(x * x, axis=-1, keepdims=True) + EPS) * w


def _sigmoid(x):
    return 1.0 / (1.0 + jnp.exp(-x))


def _dot(a, b):
    return jnp.dot(a, b, preferred_element_type=F32)


def _dot_nt(a, b):
    return lax.dot_general(a, b, (((1,), (1,)), ((), ())), preferred_element_type=F32)


def _dot_tn(a, b):
    return lax.dot_general(a, b, (((0,), (0,)), ((), ())), preferred_element_type=F32)


def _rope(x, cos_t, sin_t, period):
    half = period // 2
    lane = lax.broadcasted_iota(I32, x.shape, 1)
    first = (lane % period) < half
    rot = jnp.where(first, pltpu.roll(x, LANES - half, 1), pltpu.roll(x, half, 1))
    return x * cos_t + rot * sin_t


def _rope_tables(pos, period):
    half = period // 2
    inv = ROPE_THETA ** (-2.0 * jnp.arange(half, dtype=F32) / period)
    ang = pos.astype(F32)[:, None] * inv[None, :]
    cos = jnp.cos(ang)
    sin = jnp.sin(ang)
    cos_p = jnp.concatenate([cos, cos], axis=1)
    sin_p = jnp.concatenate([-sin, sin], axis=1)
    reps = LANES // period
    return jnp.tile(cos_p, (1, reps)), jnp.tile(sin_p, (1, reps))


def _proj_in_kernel(x_ref, nw_ref, win_ref, cr_ref, sr_ref, cp_ref, sp_ref, qn_ref, kvn_ref,
                    wuqn_ref, wuqp_ref, wk_ref,
                    rq_ref, rk_ref, rv_ref, rg_ref, qlat_ref, qpe_ref, c_ref, kpe_ref, kcat_ref):
    hn = _rms(x_ref[...], nw_ref[...]).astype(BF16)
    cr, sr, cp, sp = cr_ref[...], sr_ref[...], cp_ref[...], sp_ref[...]
    hk = RET_HEADS * RET_DK
    hv = RET_HEADS * RET_DV

    def mm(lo, width):
        return _dot(hn, win_ref[:, lo:lo + width])

    u = mm(0, hk)
    rq_ref[...] = jnp.concatenate(
        [_rope(u[:, g:g + LANES], cr, sr, RET_DK) for g in range(0, hk, LANES)], axis=1).astype(BF16)
    u = mm(hk, hk)
    rk_ref[...] = jnp.concatenate(
        [_rope(u[:, g:g + LANES], cr, sr, RET_DK) * (RET_DK ** -0.5) for g in range(0, hk, LANES)],
        axis=1).astype(BF16)
    rv_ref[...] = mm(2 * hk, hv).astype(BF16)
    rg_ref[...] = mm(2 * hk + hv, hv)
    off = 2 * hk + 2 * hv
    cqn = _rms(mm(off, Q_LORA), qn_ref[...]).astype(BF16)
    c = _rms(mm(off + Q_LORA, KV_LORA), kvn_ref[...])
    c_ref[...] = c
    kp = _rope(mm(off + Q_LORA + KV_LORA, LANES), cp, sp, MLA_ROPE)
    kpe_ref[...] = kp[:, :MLA_ROPE]
    kcat_ref[...] = jnp.concatenate([c, kp], axis=1).astype(BF16)
    scale = (MLA_NOPE + MLA_ROPE) ** -0.5
    qn = _dot(cqn, wuqn_ref[...])
    for h in range(MLA_HEADS):
        ql = _dot(qn[:, h * LANES:(h + 1) * LANES].astype(BF16), wk_ref[h]) * scale
        qlat_ref[:, h * KV_LORA:(h + 1) * KV_LORA] = ql.astype(BF16)
    qp = _dot(cqn, wuqp_ref[...])
    qpe_ref[...] = jnp.concatenate(
        [_rope(qp[:, g:g + LANES], cp, sp, MLA_ROPE) * scale
         for g in range(0, MLA_HEADS * MLA_ROPE, LANES)], axis=1).astype(BF16)


def _proj_in(x2, nw, win, tabs, qn, kvn, wuqn, wuqp, wk, tm):
    n, d = x2.shape
    cr, sr, cp, sp = tabs
    ltab = cr.shape[0]
    nblk = ltab // tm if ltab >= tm else 0
    tab_rows = tm if nblk else ltab
    tab_map = (lambda i: (i % nblk, 0)) if nblk else (lambda i: (0, 0))
    row = lambda i: (i, 0)
    const = lambda i: (0, 0)
    hk = RET_HEADS * RET_DK
    hv = RET_HEADS * RET_DV
    tab_spec = pl.BlockSpec((tab_rows, LANES), tab_map)
    out_cols = [(hk, BF16), (hk, BF16), (hv, BF16), (hv, F32), (MLA_HEADS * KV_LORA, BF16),
                (MLA_HEADS * MLA_ROPE, BF16), (KV_LORA, F32), (MLA_ROPE, F32), (2 * LANES, BF16)]
    return pl.pallas_call(
        _proj_in_kernel,
        grid=(n // tm,),
        in_specs=[pl.BlockSpec((tm, d), row), pl.BlockSpec((1, d), const),
                  pl.BlockSpec(win.shape, const), tab_spec, tab_spec, tab_spec, tab_spec,
                  pl.BlockSpec((1, Q_LORA), const), pl.BlockSpec((1, KV_LORA), const),
                  pl.BlockSpec(wuqn.shape, const), pl.BlockSpec(wuqp.shape, const),
                  pl.BlockSpec(wk.shape, lambda i: (0, 0, 0))],
        out_specs=[pl.BlockSpec((tm, w), row) for w, _ in out_cols],
        out_shape=[jax.ShapeDtypeStruct((n, w), dt) for w, dt in out_cols],
        compiler_params=_params("parallel"),
        name="proj_in",
    )(x2, nw, win, cr, sr, cp, sp, qn, kvn, wuqn, wuqp, wk)


def _ret_kernel(rq_ref, rk_ref, rv_ref, rg_ref, gnw_ref, decay_ref, qdec_ref, kdec_ref, cdec_ref,
                o_ref, s_out_ref, s_scr):
    ci = pl.program_id(1)

    @pl.when(ci == 0)
    def _():
        s_scr[...] = jnp.zeros_like(s_scr)

    outs = []
    for h in range(RET_HEADS):
        sl = slice(h * RET_DK, (h + 1) * RET_DK)
        q = rq_ref[:, sl]
        k = rk_ref[:, sl]
        v = rv_ref[:, sl]
        s = s_scr[h]
        sc = _dot_nt(q, k) * decay_ref[h]
        o = _dot(sc.astype(BF16), v) + _dot(q, s.astype(BF16)) * qdec_ref[:, sl]
        kd = (k.astype(F32) * kdec_ref[:, sl]).astype(BF16)
        s_scr[h] = s * cdec_ref[:, sl] + _dot_tn(kd, v)
        mu = jnp.mean(o, axis=-1, keepdims=True)
        var = jnp.mean(jnp.square(o - mu), axis=-1, keepdims=True)
        outs.append((o - mu) * lax.rsqrt(var + EPS))
    g = rg_ref[...]
    o_ref[...] = (jnp.concatenate(outs, axis=1) * gnw_ref[...] * (g * _sigmoid(g))).astype(BF16)

    @pl.when(ci == pl.num_programs(1) - 1)
    def _():
        s_out_ref[...] = s_scr[...]


def _ret_tables(chunk):
    h = RET_HEADS
    log_g = jnp.log1p(-jnp.exp2(-5.0 - jnp.arange(h, dtype=F32)))
    idx = jnp.arange(chunk, dtype=F32)
    rel = idx[:, None] - idx[None, :]
    decay = jnp.where(rel[None] >= 0, jnp.exp(jnp.maximum(rel, 0.0)[None] * log_g[:, None, None]), 0.0)
    q_dec = jnp.exp((idx[:, None] + 1.0) * log_g[None, :])
    k_dec = jnp.exp((chunk - 1.0 - idx)[:, None] * log_g[None, :])
    c_dec = jnp.exp(chunk * log_g)
    rep = lambda t: jnp.repeat(t, RET_DK, axis=-1)
    return decay, rep(q_dec), rep(k_dec), rep(c_dec[None, :])


def _retention_prompt(rq, rk, rv, rg, gnw, b, l):
    chunk = RET_CHUNK if l % RET_CHUNK == 0 else l
    decay, qdec, kdec, cdec = _ret_tables(chunk)
    hk = RET_HEADS * RET_DK
    n_chunks = l // chunk
    row = lambda bi, ci: (bi * n_chunks + ci, 0)
    const = lambda bi, ci: (0, 0)
    return pl.pallas_call(
        _ret_kernel,
        grid=(b, n_chunks),
        in_specs=[pl.BlockSpec((chunk, hk), row)] * 4 + [
            pl.BlockSpec((1, hk), const),
            pl.BlockSpec(decay.shape, lambda bi, ci: (0, 0, 0)),
            pl.BlockSpec((chunk, hk), const), pl.BlockSpec((chunk, hk), const),
            pl.BlockSpec((1, hk), const)],
        out_specs=[pl.BlockSpec((chunk, hk), row),
                   pl.BlockSpec((None, None, RET_HEADS, RET_DK, RET_DV), lambda bi, ci: (0, bi, 0, 0, 0))],
        out_shape=[jax.ShapeDtypeStruct((b * l, hk), BF16),
                   jax.ShapeDtypeStruct((1, b, RET_HEADS, RET_DK, RET_DV), F32)],
        scratch_shapes=[pltpu.VMEM((RET_HEADS, RET_DK, RET_DV), F32)],
        compiler_params=_params("parallel", "arbitrary"),
        name="retention_prompt",
    )(rq, rk, rv, rg, gnw, decay, qdec, kdec, cdec)


def _ret_decode_kernel(q_ref, k_ref, v_ref, g_ref, gnw_ref, gam_ref, s_ref, o_ref, s_out_ref):
    gam = gam_ref[...]
    q = q_ref[...]
    k = k_ref[...]
    v = v_ref[...]
    qg = q * gam
    cross = jnp.zeros_like(v)
    for d in range(q.shape[0]):
        s_d = s_ref[d]
        cross = cross + qg[d:d + 1, :] * s_d
        s_out_ref[d] = s_d * gam + k[d:d + 1, :] * v
    o = jnp.sum(q * k, axis=0, keepdims=True) * v + cross
    mu = jnp.mean(o, axis=0, keepdims=True)
    var = jnp.mean(jnp.square(o - mu), axis=0, keepdims=True)
    g = g_ref[...]
    o_ref[...] = (o - mu) * lax.rsqrt(var + EPS) * gnw_ref[...] * (g * _sigmoid(g))


def _retention_decode(rq, rk, rv, rg, gnw, state):
    b = rq.shape[0]
    h, dk, dv = RET_HEADS, RET_DK, RET_DV
    gam = jnp.exp(jnp.log1p(-jnp.exp2(-5.0 - jnp.arange(h, dtype=F32)))).reshape(h, 1, 1)
    tr = lambda t: t.astype(F32).T
    vec = lambda d: pl.BlockSpec((d, b), lambda i: (i, 0))
    sspec = pl.BlockSpec((None, None, dk, dv, b), lambda i: (0, i, 0, 0, 0))
    o, s_new = pl.pallas_call(
        _ret_decode_kernel,
        grid=(h,),
        in_specs=[vec(dk), vec(dk), vec(dv), vec(dv), pl.BlockSpec((dv, 1), lambda i: (i, 0)),
                  pl.BlockSpec((None, 1, 1), lambda i: (i, 0, 0)), sspec],
        out_specs=[vec(dv), sspec],
        out_shape=[jax.ShapeDtypeStruct((h * dv, b), F32),
                   jax.ShapeDtypeStruct((1, h, dk, dv, b), F32)],
        compiler_params=_params("parallel"),
        name="retention_decode",
    )(tr(rq), tr(rk), tr(rv), tr(rg), gnw.reshape(h * dv, 1), gam, jnp.transpose(state, (0, 2, 3, 4, 1)))
    return o.T, jnp.transpose(s_new, (0, 4, 1, 2, 3))


def _attn_kernel(qlat_ref, qpe_ref, kcat_ref, wv_ref, o_ref, q_scr, m_scr, acc_scr):
    i = pl.program_id(1)
    lane = lax.broadcasted_iota(I32, (Q_BLOCK, LANES), 1)
    per_group = LANES // MLA_ROPE
    parts = []
    for h in range(MLA_HEADS):
        g, r = divmod(h, per_group)
        pe = qpe_ref[:, g * LANES:(g + 1) * LANES]
        keep = (lane >= r * MLA_ROPE) & (lane < (r + 1) * MLA_ROPE)
        pe = jnp.where(keep, pe, jnp.zeros_like(pe))
        parts.append(jnp.concatenate([qlat_ref[:, h * KV_LORA:(h + 1) * KV_LORA], pe], axis=1))
    q_scr[...] = jnp.concatenate(parts, axis=0)
    rows = MLA_HEADS * Q_BLOCK
    m_scr[...] = jnp.full_like(m_scr, -jnp.inf)
    acc_scr[...] = jnp.zeros_like(acc_scr)
    ones = jnp.ones((KV_BLOCK, LANES), BF16)

    def block(j, masked):
        k0 = pl.multiple_of(j * KV_BLOCK, KV_BLOCK)
        kc = kcat_ref[pl.ds(k0, KV_BLOCK), :]
        s = _dot_nt(q_scr[...], kc)
        if masked:
            qpos = i * Q_BLOCK + lax.broadcasted_iota(I32, (rows, KV_BLOCK), 0) % Q_BLOCK
            kpos = k0 + lax.broadcasted_iota(I32, (rows, KV_BLOCK), 1)
            s = jnp.where(kpos <= qpos, s, -jnp.inf)
        s0, s1 = s[:, :LANES], s[:, LANES:]
        m_old = m_scr[...]
        m_new = jnp.maximum(m_old, jnp.max(jnp.maximum(s0, s1), axis=-1, keepdims=True))
        alpha = jnp.exp(m_old - m_new)
        p = jnp.concatenate([jnp.exp(s0 - m_new), jnp.exp(s1 - m_new)], axis=1).astype(BF16)
        vext = jnp.concatenate([kc[:, :KV_LORA], ones], axis=1)
        acc_scr[...] = jnp.concatenate([alpha, alpha], axis=1) * acc_scr[...] + _dot(p, vext)
        m_scr[...] = m_new

    n_full = (i * Q_BLOCK) // KV_BLOCK

    def body(j, carry):
        block(j, False)
        return carry

    lax.fori_loop(0, n_full, body, 0)
    block(n_full, True)
    o = acc_scr[:, :KV_LORA] / acc_scr[:, KV_LORA:]
    outs = [_dot(o[h * Q_BLOCK:(h + 1) * Q_BLOCK].astype(BF16), wv_ref[h]) for h in range(MLA_HEADS)]
    o_ref[...] = jnp.concatenate(outs, axis=1).astype(BF16)


def _attention_prompt(qlat, qpe, kcat, wv, b, l):
    nq = l // Q_BLOCK
    rows = MLA_HEADS * Q_BLOCK
    qmap = lambda bi, i: (bi * nq + i, 0)
    return pl.pallas_call(
        _attn_kernel,
        grid=(b, nq),
        in_specs=[pl.BlockSpec((Q_BLOCK, qlat.shape[1]), qmap),
                  pl.BlockSpec((Q_BLOCK, qpe.shape[1]), qmap),
                  pl.BlockSpec((l, kcat.shape[1]), lambda bi, i: (bi, 0)),
                  pl.BlockSpec(wv.shape, lambda bi, i: (0, 0, 0))],
        out_specs=pl.BlockSpec((Q_BLOCK, MLA_HEADS * MLA_V), qmap),
        out_shape=jax.ShapeDtypeStruct((b * l, MLA_HEADS * MLA_V), BF16),
        scratch_shapes=[pltpu.VMEM((rows, 2 * LANES), BF16), pltpu.VMEM((rows, LANES), F32),
                        pltpu.VMEM((rows, 2 * KV_LORA), F32)],
        compiler_params=_params("parallel", "arbitrary"),
        name="attention_prompt",
    )(qlat, qpe, kcat, wv)


def _attn_decode_kernel(pt_ref, ql_ref, qp_ref, cn_ref, kn_ref, lat_hbm, rope_hbm, o_ref,
                        kbuf, rbuf, sem):
    b = pl.program_id(0)
    nb = pl.num_programs(0)
    n_pages = pt_ref.shape[1]
    page = lat_hbm.shape[2]

    def copies(seq, slot, p):
        phys = pt_ref[seq, p]
        dst = pl.ds(p * page, page)
        return (pltpu.make_async_copy(lat_hbm.at[0, phys], kbuf.at[slot, dst], sem.at[slot, 0]),
                pltpu.make_async_copy(rope_hbm.at[0, phys], rbuf.at[slot, :, dst], sem.at[slot, 1]))

    def start_all(seq, slot):
        def body(p, carry):
            for cp in copies(seq, slot, p):
                cp.start()
            return carry
        lax.fori_loop(0, n_pages, body, 0)

    def wait_all(seq, slot):
        def body(p, carry):
            for cp in copies(seq, slot, p):
                cp.wait()
            return carry
        lax.fori_loop(0, n_pages, body, 0)

    slot = b % 2

    @pl.when(b == 0)
    def _():
        start_all(0, 0)

    @pl.when(b + 1 < nb)
    def _():
        start_all(b + 1, 1 - slot)

    wait_all(b, slot)
    ql = ql_ref[...]
    qp = qp_ref[...]
    kc = kbuf[slot].astype(BF16)
    kr = rbuf[slot].astype(BF16)
    s_past = _dot_nt(ql, kc) + _dot(qp, kr)
    cn = cn_ref[...].astype(BF16).astype(F32)
    kn = kn_ref[...].astype(BF16).astype(F32)
    s_new = (jnp.sum(ql.astype(F32) * cn, axis=-1, keepdims=True)
             + jnp.sum(qp.astype(F32) * kn, axis=-1, keepdims=True))
    m = jnp.maximum(jnp.max(s_past, axis=-1, keepdims=True), s_new)
    p_past = jnp.exp(s_past - m)
    p_new = jnp.exp(s_new - m)
    denom = jnp.sum(p_past, axis=-1, keepdims=True) + p_new
    o = _dot(p_past.astype(BF16), kc) + p_new.astype(BF16).astype(F32) * cn
    o_ref[...] = o / denom


def _attention_decode(qlat, qpe, c_new, kpe_new, lat_cache, rope_cache, page_table):
    b = qlat.shape[0]
    h = MLA_HEADS
    n_pages = page_table.shape[1]
    page = lat_cache.shape[2]
    past = n_pages * page
    blk = lambda w: pl.BlockSpec((None, h, w), lambda i, pt: (i, 0, 0))
    one = lambda w: pl.BlockSpec((None, 1, w), lambda i, pt: (i, 0, 0))
    grid_spec = pltpu.PrefetchScalarGridSpec(
        num_scalar_prefetch=1,
        grid=(b,),
        in_specs=[blk(KV_LORA), blk(MLA_ROPE), one(KV_LORA), one(MLA_ROPE),
                  pl.BlockSpec(memory_space=pl.ANY), pl.BlockSpec(memory_space=pl.ANY)],
        out_specs=blk(KV_LORA),
        scratch_shapes=[pltpu.VMEM((2, past, KV_LORA), F32), pltpu.VMEM((2, MLA_ROPE, past), F32),
                        pltpu.SemaphoreType.DMA((2, 2))],
    )
    return pl.pallas_call(
        _attn_decode_kernel,
        grid_spec=grid_spec,
        out_shape=jax.ShapeDtypeStruct((b, h, KV_LORA), F32),
        compiler_params=_params("arbitrary"),
        name="attention_decode",
    )(page_table, qlat.reshape(b, h, KV_LORA), qpe.reshape(b, h, MLA_ROPE),
      c_new.reshape(b, 1, KV_LORA), kpe_new.reshape(b, 1, MLA_ROPE), lat_cache,
      jnp.swapaxes(rope_cache, 2, 3))


def _route(logits):
    lane = lax.broadcasted_iota(I32, logits.shape, 1)
    lane_f = lane.astype(F32)
    neg = -jnp.inf
    big = float(LANES)
    gmask = lane < N_GROUPS
    gl = jnp.where(gmask, logits, neg)
    gmax = jnp.max(gl, axis=-1, keepdims=True)
    g_sel = jnp.min(jnp.where(gl == gmax, lane_f, big), axis=-1, keepdims=True)
    gsum = jnp.sum(jnp.where(gmask, jnp.exp(gl - gmax), 0.0), axis=-1, keepdims=True)
    g_w = 1.0 / gsum
    lo = N_GROUPS + EXPERTS_PER_GROUP * g_sel
    emask = (lane_f >= lo) & (lane_f < lo + EXPERTS_PER_GROUP)
    el = jnp.where(emask, logits, neg)
    v1 = jnp.max(el, axis=-1, keepdims=True)
    i1 = jnp.min(jnp.where(el == v1, lane_f, big), axis=-1, keepdims=True)
    el2 = jnp.where(lane_f == i1, neg, el)
    v2 = jnp.max(el2, axis=-1, keepdims=True)
    i2 = jnp.min(jnp.where(el2 == v2, lane_f, big), axis=-1, keepdims=True)
    e2 = jnp.exp(v2 - v1)
    w1 = g_w / (1.0 + e2)
    w2 = g_w * e2 / (1.0 + e2)
    return i1 - N_GROUPS, i2 - N_GROUPS, w1, w2


def _bf16_parts(w):
    hi = w.astype(BF16).astype(F32)
    mid = (w - hi).astype(BF16).astype(F32)
    return hi, mid, w - hi - mid


def _lane_pack(shape, cols):
    lane = lax.broadcasted_iota(I32, shape, 1)
    out = jnp.zeros(shape, F32)
    for k, col in enumerate(cols):
        out = jnp.where(lane == k, col, out)
    return out


def _mix_out_kernel(*refs, n_act, has_pre, has_bias):
    refs = list(refs)
    acts = [refs.pop(0) for _ in range(n_act)]
    ws = [refs.pop(0) for _ in range(n_act)]
    pre = refs.pop(0) if has_pre else None
    bias = refs.pop(0) if has_bias else None
    h_ref, nw_ref, rwh_ref, rwl_ref, rb_ref, h1_ref, xe_ref, rid_ref, cnt_ref = refs
    out = None
    for k, (a_ref, w_ref) in enumerate(zip(acts, ws)):
        a = a_ref[...]
        if has_pre and k == n_act - 1:
            a = _dot(a.astype(BF16), pre[...])
        t = _dot(a.astype(BF16), w_ref[...])
        out = t if out is None else out + t
    if has_bias:
        out = out + bias[...]
    h1 = h_ref[...] + out
    h1_ref[...] = h1
    xn = _rms(h1, nw_ref[...])
    xh = xn.astype(BF16)
    xl = (xn - xh.astype(F32)).astype(BF16)
    logits = _dot(xh, rwh_ref[...]) + _dot(xl, rwh_ref[...]) + _dot(xh, rwl_ref[...]) + rb_ref[...]
    e1, e2, w1, w2 = _route(logits)
    meta = _lane_pack(logits.shape, (e1, e2) + _bf16_parts(w1) + _bf16_parts(w2))
    xe_ref[...] = jnp.concatenate([xn, meta], axis=1).astype(BF16)
    rid_ref[...] = _lane_pack(logits.shape, (e1, e2)).astype(I32)
    lane_f = lax.broadcasted_iota(I32, logits.shape, 1).astype(F32)
    hit = jnp.where((lane_f == e1) | (lane_f == e2), 1.0, 0.0)
    cnt_ref[...] = jnp.sum(hit, axis=0, keepdims=True)


def _mix_out(acts, ws, h, nw, rw, rb, tm, pre=None, bias=None):
    n, d = h.shape
    row = lambda i: (i, 0)
    const = lambda i: (0, 0)
    ins = list(acts) + list(ws)
    specs = [pl.BlockSpec((tm, a.shape[1]), row) for a in acts] + [pl.BlockSpec(w.shape, const) for w in ws]
    if pre is not None:
        ins.append(pre)
        specs.append(pl.BlockSpec(pre.shape, const))
    if bias is not None:
        ins.append(bias)
        specs.append(pl.BlockSpec(bias.shape, const))
    ins += [h, nw, rw[0], rw[1], rb]
    specs += [pl.BlockSpec((tm, d), row), pl.BlockSpec((1, d), const),
              pl.BlockSpec(rw[0].shape, const), pl.BlockSpec(rw[1].shape, const), pl.BlockSpec(rb.shape, const)]
    kern = functools.partial(_mix_out_kernel, n_act=len(acts), has_pre=pre is not None,
                             has_bias=bias is not None)
    outs = pl.pallas_call(
        kern,
        grid=(n // tm,),
        in_specs=specs,
        out_specs=[pl.BlockSpec((tm, d), row), pl.BlockSpec((tm, d + LANES), row),
                   pl.BlockSpec((tm, LANES), row), pl.BlockSpec((None, 1, LANES), lambda i: (i, 0, 0))],
        out_shape=[jax.ShapeDtypeStruct((n, d), F32), jax.ShapeDtypeStruct((n, d + LANES), BF16),
                   jax.ShapeDtypeStruct((n, LANES), I32), jax.ShapeDtypeStruct((n // tm, 1, LANES), F32)],
        compiler_params=_params("parallel"),
        name="mix_out_route",
    )(*ins)
    return tuple(outs)


SEG_ALIGN = 16
FFN_ROWS = 256


def _seg_bits(t):
    bits = []
    b = SEG_ALIGN
    while b <= t:
        bits.append(b)
        b *= 2
    return tuple(reversed(bits))


def _slot_cap(t):
    cap = 2 * t + N_EXPERTS * (SEG_ALIGN - 1)
    return -(-cap // FFN_ROWS) * FFN_ROWS


def _moe_meta(counts, tile_tokens):
    cnt = counts[:, :N_EXPERTS].astype(I32)
    ell = ((cnt + SEG_ALIGN - 1) // SEG_ALIGN) * SEG_ALIGN
    off = jnp.cumsum(ell, axis=1) - ell
    tot = jnp.sum(ell, axis=0)
    region = ((tot + FFN_ROWS - 1) // FFN_ROWS) * FFN_ROWS
    gend = jnp.cumsum(region)
    gbase = gend - region
    gseg = gbase[None, :] + jnp.cumsum(ell, axis=0) - ell
    n_pairs = 2 * sum(tile_tokens)
    n_rows_max = n_pairs + len(tile_tokens) * N_EXPERTS * (SEG_ALIGN - 1) + N_EXPERTS * (FFN_ROWS - 1)
    n_steps = -(-n_rows_max // FFN_ROWS)
    starts = jnp.arange(n_steps, dtype=I32) * FFN_ROWS
    step_expert = jnp.minimum(jnp.sum((gend[None, :] <= starts[:, None]).astype(I32), axis=1), N_EXPERTS - 1)
    n_used = (gend[-1] // FFN_ROWS).reshape(1)
    off_f = off.astype(F32)
    return dict(off=off, ell=ell, gseg=gseg, tail_start=gbase + tot, tail_len=region - tot, end=gend[-1:],
                step_expert=step_expert, n_used=n_used, n_rows=n_steps * FFN_ROWS,
                off_col=off_f[:, :, None], off_row=jnp.pad(off_f, ((0, 0), (0, LANES - N_EXPERTS)))[:, None, :])


def _chunk_copies(n, bits, make):
    for b in bits:
        hi = n & (-(2 * b))

        @pl.when((n & b) != 0)
        def _():
            make(hi, b)


def _pair_slots_rows(ids_t, off_col, t):
    sub = lax.broadcasted_iota(I32, (N_EXPERTS, t), 0).astype(F32)
    e1 = jnp.where(sub == ids_t[0:1, :], 1.0, 0.0)
    e2 = jnp.where(sub == ids_t[1:2, :], 1.0, 0.0)
    before = jnp.where(lax.broadcasted_iota(I32, (t, t), 0) < lax.broadcasted_iota(I32, (t, t), 1), 1.0, 0.0)
    before = before.astype(BF16)
    cum1 = _dot(e1.astype(BF16), before)
    cum2 = _dot(e2.astype(BF16), before)
    cnt1 = jnp.sum(e1, axis=1, keepdims=True)
    slot0 = jnp.sum(e1 * (off_col + cum1), axis=0, keepdims=True)
    slot1 = jnp.sum(e2 * (off_col + cnt1 + cum2), axis=0, keepdims=True)
    return slot0, slot1


def _pair_slots_cols(ids, off_row, t):
    lane = lax.broadcasted_iota(I32, (t, LANES), 1).astype(F32)
    idf = ids.astype(F32)
    e1 = jnp.where(lane == idf[:, 0:1], 1.0, 0.0)
    e2 = jnp.where(lane == idf[:, 1:2], 1.0, 0.0)
    before = jnp.where(lax.broadcasted_iota(I32, (t, t), 1) < lax.broadcasted_iota(I32, (t, t), 0), 1.0, 0.0)
    before = before.astype(BF16)
    cum1 = _dot(before, e1.astype(BF16))
    cum2 = _dot(before, e2.astype(BF16))
    cnt1 = jnp.sum(e1, axis=0, keepdims=True)
    slot0 = jnp.sum(e1 * (off_row + cum1), axis=1, keepdims=True)
    slot1 = jnp.sum(e2 * (off_row + cnt1 + cum2), axis=1, keepdims=True)
    return slot0, slot1


def _permute_kernel(off_ref, ell_ref, gseg_ref, tstart_ref, tlen_ref, end_ref, xe_ref, ids_ref, offcol_ref, *rest,
                    tile0, zero_tiles, bits):
    xg_hbm, xs_scr, zero_scr, sem = rest[-4:]
    s = pl.program_id(0)
    last = pl.num_programs(0) - 1
    t = xe_ref.shape[0]
    cap = xs_scr.shape[1]
    slot = s % 2

    def seg_start(tile, buf):
        def body(e, carry):
            o = off_ref[tile, e]
            g = gseg_ref[tile, e]

            def make(hi, b):
                src = xs_scr.at[buf, pl.ds(pl.multiple_of(o + hi, SEG_ALIGN), b)]
                dst = xg_hbm.at[pl.ds(pl.multiple_of(g + hi, SEG_ALIGN), b)]
                pltpu.make_async_copy(src, dst, sem.at[buf]).start()

            _chunk_copies(ell_ref[tile, e], bits, make)
            return carry

        lax.fori_loop(0, N_EXPERTS, body, 0)

    def seg_wait(tile, buf):
        def make(hi, b):
            pltpu.make_async_copy(xs_scr.at[buf, pl.ds(0, b)], xg_hbm.at[pl.ds(0, b)], sem.at[buf]).wait()

        _chunk_copies(off_ref[tile, N_EXPERTS - 1] + ell_ref[tile, N_EXPERTS - 1], _seg_bits(cap), make)

    if zero_tiles is not None:
        zrows = zero_scr.shape[0]

        def zero_fill(g, n, fn):
            def make(hi, b):
                dst = xg_hbm.at[pl.ds(pl.multiple_of(g + hi, SEG_ALIGN), b)]
                fn(pltpu.make_async_copy(zero_scr.at[pl.ds(0, b)], dst, sem.at[2]))

            _chunk_copies(n, _seg_bits(zrows), make)

        def zero_copies(fn):
            def body(e, carry):
                zero_fill(tstart_ref[e], tlen_ref[e], fn)
                for zt in zero_tiles:
                    zero_fill(gseg_ref[zt, e], ell_ref[zt, e], fn)
                return carry

            lax.fori_loop(0, N_EXPERTS, body, 0)

            def trailing(k, carry):
                dst = xg_hbm.at[pl.ds(pl.multiple_of(end_ref[0] + k * zrows, SEG_ALIGN), zrows)]
                fn(pltpu.make_async_copy(zero_scr, dst, sem.at[2]))
                return carry

            n_trailing = lax.shift_right_logical(xg_hbm.shape[0] - end_ref[0], zrows.bit_length() - 1)
            lax.fori_loop(0, n_trailing, trailing, 0)

        @pl.when(s == 0)
        def _():
            zero_scr[...] = jnp.zeros_like(zero_scr)
            zero_copies(lambda cp: cp.start())
            zero_copies(lambda cp: cp.wait())

    @pl.when(s >= 2)
    def _():
        seg_wait(tile0 + s - 2, slot)

    tile = tile0 + s
    ids_t = ids_ref[...].astype(F32).T
    slot0, slot1 = _pair_slots_rows(ids_t, offcol_ref[...], t)
    used = off_ref[tile, N_EXPERTS - 1] + ell_ref[tile, N_EXPERTS - 1]
    xe = xe_ref[...]
    for r0 in range(0, cap, FFN_ROWS):
        @pl.when(r0 < used)
        def _():
            rows = (lax.broadcasted_iota(I32, (FFN_ROWS, t), 0) + r0).astype(F32)
            pm = jnp.where((rows == slot0) | (rows == slot1), 1.0, 0.0).astype(BF16)
            xs_scr[slot, pl.ds(r0, FFN_ROWS), :] = _dot(pm, xe).astype(BF16)

    seg_start(tile, slot)

    @pl.when(s == last)
    def _():
        @pl.when(s >= 1)
        def _():
            seg_wait(tile - 1, 1 - slot)

        seg_wait(tile, slot)


def _moe_permute(meta, xe, ids, t, tile0, xg_prev=None):
    n, de = xe.shape
    cap = _slot_cap(t)
    n_tiles = meta["off"].shape[0]
    zero_tiles = None if xg_prev is not None else tuple(range(tile0 + n // t, n_tiles))
    prefetch = [meta["off"], meta["ell"], meta["gseg"], meta["tail_start"], meta["tail_len"], meta["end"]]
    imap = lambda s, *_: (s, 0)
    in_specs = [pl.BlockSpec((t, de), imap), pl.BlockSpec((t, LANES), imap),
                pl.BlockSpec((None, N_EXPERTS, 1), lambda s, *_: (tile0 + s, 0, 0))]
    ins = [xe, ids, meta["off_col"]]
    aliases = {}
    if xg_prev is not None:
        in_specs.append(pl.BlockSpec(memory_space=pl.ANY))
        ins.append(xg_prev)
        aliases = {len(prefetch) + len(ins) - 1: 0}
    grid_spec = pltpu.PrefetchScalarGridSpec(
        num_scalar_prefetch=len(prefetch),
        grid=(n // t,),
        in_specs=in_specs,
        out_specs=pl.BlockSpec(memory_space=pl.ANY),
        scratch_shapes=[pltpu.VMEM((2, cap, de), BF16), pltpu.VMEM((FFN_ROWS // 2, de), BF16),
                        pltpu.SemaphoreType.DMA((3,))],
    )
    return pl.pallas_call(
        functools.partial(_permute_kernel, tile0=tile0, zero_tiles=zero_tiles, bits=_seg_bits(t)),
        grid_spec=grid_spec,
        out_shape=jax.ShapeDtypeStruct((meta["n_rows"], de), BF16),
        input_output_aliases=aliases,
        compiler_params=_params("arbitrary"),
        name="moe_permute",
    )(*prefetch, *ins)


def _ffn_kernel(te_ref, nu_ref, x_ref, wg_ref, wu_ref, wd_ref, o_ref, wg_b, wu_b, wd_b):
    i = pl.program_id(0)
    e = te_ref[i]
    d = o_ref.shape[1]

    @pl.when((i == 0) | (e != te_ref[jnp.maximum(i - 1, 0)]))
    def _():
        wg_b[...] = wg_ref[...].astype(BF16)
        wu_b[...] = wu_ref[...].astype(BF16)
        wd_b[...] = wd_ref[...].astype(BF16)

    @pl.when(i < nu_ref[0])
    def _():
        x = x_ref[:, :d]
        meta = x_ref[:, d:].astype(F32)
        w1 = meta[:, 2:3] + meta[:, 3:4] + meta[:, 4:5]
        w2 = meta[:, 5:6] + meta[:, 6:7] + meta[:, 7:8]
        w = jnp.where(meta[:, 0:1] == e.astype(F32), w1, w2)
        g = _dot(x, wg_b[...])
        u = _dot(x, wu_b[...])
        hmid = (g * _sigmoid(g) * u).astype(BF16)
        o_ref[...] = (_dot(hmid, wd_b[...]) * w).astype(BF16)

    @pl.when(i >= nu_ref[0])
    def _():
        o_ref[...] = jnp.zeros_like(o_ref)


def _moe_ffn(xs, tile_expert, n_used, wg, wu, wd, tr, layer):
    n_rows, d = xs.shape[0], wg.shape[2]
    de = wg.shape[3]
    wmap = lambda i, te, nu: (layer, te[i], 0, 0)
    grid_spec = pltpu.PrefetchScalarGridSpec(
        num_scalar_prefetch=2,
        grid=(n_rows // tr,),
        in_specs=[pl.BlockSpec((tr, xs.shape[1]), lambda i, te, nu: (i, 0)),
                  pl.BlockSpec((None, None, d, de), wmap),
                  pl.BlockSpec((None, None, d, de), wmap),
                  pl.BlockSpec((None, None, de, d), wmap)],
        out_specs=pl.BlockSpec((tr, d), lambda i, te, nu: (i, 0)),
        scratch_shapes=[pltpu.VMEM((d, de), BF16), pltpu.VMEM((d, de), BF16), pltpu.VMEM((de, d), BF16)],
    )
    return pl.pallas_call(
        _ffn_kernel,
        grid_spec=grid_spec,
        out_shape=jax.ShapeDtypeStruct((n_rows, d), BF16),
        compiler_params=_params("arbitrary"),
        name="moe_ffn",
    )(tile_expert, n_used, xs, wg, wu, wd)


def _unpermute_kernel(off_ref, ell_ref, gseg_ref, ids_ref, offrow_ref, h_ref, yg_hbm, o_ref, ys_scr, sem,
                      *, tile0, bits):
    s = pl.program_id(0)
    t = h_ref.shape[0]
    cap = ys_scr.shape[1]
    slot = s % 2

    def seg_start(tile, buf):
        def body(e, carry):
            o = off_ref[tile, e]
            g = gseg_ref[tile, e]

            def make(hi, b):
                src = yg_hbm.at[pl.ds(pl.multiple_of(g + hi, SEG_ALIGN), b)]
                dst = ys_scr.at[buf, pl.ds(pl.multiple_of(o + hi, SEG_ALIGN), b)]
                pltpu.make_async_copy(src, dst, sem.at[buf]).start()

            _chunk_copies(ell_ref[tile, e], bits, make)
            return carry

        lax.fori_loop(0, N_EXPERTS, body, 0)

    def seg_wait(tile, buf):
        def make(hi, b):
            pltpu.make_async_copy(yg_hbm.at[pl.ds(0, b)], ys_scr.at[buf, pl.ds(0, b)], sem.at[buf]).wait()

        _chunk_copies(off_ref[tile, N_EXPERTS - 1] + ell_ref[tile, N_EXPERTS - 1], _seg_bits(cap), make)

    tile = tile0 + s

    @pl.when(s == 0)
    def _():
        ys_scr[...] = jnp.zeros_like(ys_scr)
        seg_start(tile, slot)

    @pl.when(s + 1 < pl.num_programs(0))
    def _():
        seg_start(tile + 1, 1 - slot)

    seg_wait(tile, slot)
    slot0, slot1 = _pair_slots_cols(ids_ref[...], offrow_ref[...], t)
    cols = lax.broadcasted_iota(I32, (t, cap), 1).astype(F32)
    pu = jnp.where((cols == slot0) | (cols == slot1), 1.0, 0.0).astype(BF16)
    o_ref[...] = h_ref[...] + _dot(pu, ys_scr[slot])


def _moe_unpermute(meta, h, ids, yg, t, tile0):
    n, d = h.shape
    cap = _slot_cap(t)
    prefetch = [meta["off"], meta["ell"], meta["gseg"]]
    imap = lambda s, *_: (s, 0)
    grid_spec = pltpu.PrefetchScalarGridSpec(
        num_scalar_prefetch=len(prefetch),
        grid=(n // t,),
        in_specs=[pl.BlockSpec((t, LANES), imap),
                  pl.BlockSpec((None, 1, LANES), lambda s, *_: (tile0 + s, 0, 0)),
                  pl.BlockSpec((t, d), imap), pl.BlockSpec(memory_space=pl.ANY)],
        out_specs=pl.BlockSpec((t, d), imap),
        scratch_shapes=[pltpu.VMEM((2, cap, d), BF16), pltpu.SemaphoreType.DMA((2,))],
    )
    return pl.pallas_call(
        functools.partial(_unpermute_kernel, tile0=tile0, bits=_seg_bits(t)),
        grid_spec=grid_spec,
        out_shape=jax.ShapeDtypeStruct((n, d), F32),
        compiler_params=_params("arbitrary"),
        name="moe_unpermute",
    )(*prefetch, ids, meta["off_row"], h, yg)


def _moe(groups, wg, wu, wd, layer):
    tiles = [t for g in groups for t in [g[4]] * (g[0].shape[0] // g[4])]
    meta = _moe_meta(jnp.concatenate([g[3].reshape(-1, LANES) for g in groups], axis=0), tiles)
    xg = None
    tile0 = 0
    for h1, xe, ids, _, t in groups:
        xg = _moe_permute(meta, xe, ids, t, tile0, xg)
        tile0 += h1.shape[0] // t
    yg = _moe_ffn(xg, meta["step_expert"], meta["n_used"], wg, wu, wd, FFN_ROWS, layer)
    outs = []
    tile0 = 0
    for h1, xe, ids, _, t in groups:
        outs.append(_moe_unpermute(meta, h1, ids, yg, t, tile0))
        tile0 += h1.shape[0] // t
    return outs


def _ple_kernel(*refs, tail):
    h_ref, p_ref, proj_ref, pn_ref, gate_ref = refs[:5]
    h = h_ref[...]
    pp = _dot(p_ref[...].astype(BF16), proj_ref[...])
    g = _dot(_rms(h, pn_ref[...]).astype(BF16), gate_ref[...])
    h3 = h + pp * _sigmoid(g)
    if tail == "conv_in":
        nm_ref, pw1_ref, pb1_ref, h3_ref, a_ref = refs[5:]
        h3_ref[...] = h3
        d = h3.shape[1]
        hn = _rms(h3, nm_ref[...]).astype(BF16)
        ua = _dot(hn, pw1_ref[:, :d]) + pb1_ref[:, :d]
        ub = _dot(hn, pw1_ref[:, d:]) + pb1_ref[:, d:]
        a_ref[...] = ua * _sigmoid(ub)
    else:
        nf_ref, y_ref = refs[5:]
        y_ref[...] = _rms(h3, nf_ref[...])


def _ple(h, p, layer, proj, pn, gate, tm, tail, extra):
    n, d = h.shape
    row = lambda i: (i, 0)
    const = lambda i: (0, 0)
    ins = [h, p, proj, pn, gate] + list(extra)
    specs = [pl.BlockSpec((tm, d), row), pl.BlockSpec((None, tm, p.shape[2]), lambda i: (layer, i, 0)),
             pl.BlockSpec(proj.shape, const), pl.BlockSpec((1, d), const),
             pl.BlockSpec(gate.shape, const)] + [pl.BlockSpec(e.shape, const) for e in extra]
    n_out = 2 if tail == "conv_in" else 1
    return pl.pallas_call(
        functools.partial(_ple_kernel, tail=tail),
        grid=(n // tm,),
        in_specs=specs,
        out_specs=[pl.BlockSpec((tm, d), row)] * n_out,
        out_shape=[jax.ShapeDtypeStruct((n, d), F32)] * n_out,
        compiler_params=_params("parallel"),
        name="ple_" + tail,
    )(*ins)


def _ln_swish(y, w, b):
    mu = jnp.mean(y, axis=-1, keepdims=True)
    var = jnp.mean(jnp.square(y - mu), axis=-1, keepdims=True)
    z = (y - mu) * lax.rsqrt(var + EPS) * w + b
    return z * _sigmoid(z)


def _conv_kernel(a_ref, dw_ref, dwb_ref, lnw_ref, lnb_ref, o_ref, ext, shifted, acc):
    t = pl.program_id(1)
    tl = a_ref.shape[0]
    halo = ext.shape[0] - tl
    sub = 8

    @pl.when(t == 0)
    def _():
        ext[0:halo, :] = jnp.zeros((halo, ext.shape[1]), F32)

    ext[halo:, :] = a_ref[...]
    first = halo - (CONV_WIDTH - 1)
    srows = shifted.shape[1]
    for c0 in range(0, ext.shape[1], LANES):
        cs = slice(c0, c0 + LANES)
        for r in range(1, sub):
            shifted[r - 1] = ext[r:r + srows, cs]
        s = None
        for j in range(CONV_WIDTH):
            r = (first + j) % sub
            base = first + j - r
            win = ext[base:base + tl, cs] if r == 0 else shifted[r - 1, base:base + tl, :]
            term = win * dw_ref[j:j + 1, cs]
            s = term if s is None else s + term
        acc[:, cs] = s
    ext[0:halo, :] = ext[tl:tl + halo, :]
    o_ref[...] = _ln_swish(acc[...] + dwb_ref[...], lnw_ref[...], lnb_ref[...]).astype(BF16)


def _conv_prompt(a, dw, dwb, lnw, lnb, b, l, tl=256):
    d = a.shape[1]
    nt = l // tl
    halo = 32
    row = lambda bi, t: (bi * nt + t, 0)
    const = lambda bi, t: (0, 0)
    return pl.pallas_call(
        _conv_kernel,
        grid=(b, nt),
        in_specs=[pl.BlockSpec((tl, d), row), pl.BlockSpec(dw.shape, const),
                  pl.BlockSpec((1, d), const), pl.BlockSpec((1, d), const), pl.BlockSpec((1, d), const)],
        out_specs=pl.BlockSpec((tl, d), row),
        out_shape=jax.ShapeDtypeStruct((b * l, d), BF16),
        scratch_shapes=[pltpu.VMEM((tl + halo, d), F32), pltpu.VMEM((7, tl + halo - 8, LANES), F32),
                        pltpu.VMEM((tl, d), F32)],
        compiler_params=_params("parallel", "arbitrary"),
        name="conv_prompt",
    )(a, dw, dwb, lnw, lnb)


def _conv_decode_kernel(buf_ref, a_ref, dw_ref, dwb_ref, lnw_ref, lnb_ref, y_ref, new_ref):
    w = CONV_WIDTH - 1
    a = a_ref[...]
    y = a * dw_ref[w:w + 1, :]
    for j in range(w):
        y = y + buf_ref[j] * dw_ref[j:j + 1, :]
    y_ref[...] = _ln_swish(y + dwb_ref[...], lnw_ref[...], lnb_ref[...])
    for j in range(w - 1):
        new_ref[j] = buf_ref[j + 1]
    new_ref[w - 1] = a


def _conv_decode(state, a, dw, dwb, lnw, lnb, bb=8):
    _, b, w, d = state.shape
    row = pl.BlockSpec((bb, d), lambda i: (i, 0))
    sspec = pl.BlockSpec((None, w, bb, d), lambda i: (0, 0, i, 0))
    const = lambda i: (0, 0)
    y, new = pl.pallas_call(
        _conv_decode_kernel,
        grid=(b // bb,),
        in_specs=[sspec, row, pl.BlockSpec(dw.shape, const), pl.BlockSpec((1, d), const),
                  pl.BlockSpec((1, d), const), pl.BlockSpec((1, d), const)],
        out_specs=[row, sspec],
        out_shape=[jax.ShapeDtypeStruct((b, d), F32), jax.ShapeDtypeStruct((1, w, b, d), F32)],
        compiler_params=_params("parallel"),
        name="conv_decode",
    )(jnp.transpose(state, (0, 2, 1, 3)), a, dw, dwb, lnw, lnb)
    return y, jnp.transpose(new, (0, 2, 1, 3))


def _prep_weights(W):
    P = {}
    w_in = W["w_in"][0]
    kpe_cols = w_in[:, -MLA_ROPE:]
    P["w_in"] = jnp.concatenate([w_in[:, :-MLA_ROPE]] + [kpe_cols] * (LANES // MLA_ROPE), axis=1).astype(BF16)
    w_uq = W["w_uq"][0]
    nope = jnp.pad(w_uq[..., :MLA_NOPE], ((0, 0), (0, 0), (0, LANES - MLA_NOPE)))
    P["w_uq_nope"] = nope.reshape(Q_LORA, MLA_HEADS * LANES).astype(BF16)
    P["w_uq_pe"] = w_uq[..., MLA_NOPE:].reshape(Q_LORA, MLA_HEADS * MLA_ROPE).astype(BF16)
    w_ukv = W["w_ukv"][0]
    wk = jnp.transpose(w_ukv[..., :MLA_NOPE], (1, 2, 0))
    P["wk"] = jnp.pad(wk, ((0, 0), (0, LANES - MLA_NOPE), (0, 0))).astype(BF16)
    wv = jnp.transpose(w_ukv[..., MLA_NOPE:], (1, 0, 2))
    P["wv"] = wv.astype(BF16)
    eye = jnp.eye(MLA_HEADS, dtype=F32)
    P["wv_bd"] = (wv[:, :, None, :] * eye[:, None, :, None]).reshape(
        MLA_HEADS * KV_LORA, MLA_HEADS * MLA_V).astype(BF16)
    w_out = W["w_out"][0].astype(BF16)
    P["w_out_r"] = w_out[:RET_HEADS * RET_DV]
    P["w_out_m"] = w_out[RET_HEADS * RET_DV:]
    d = w_out.shape[1]
    for i in range(W["moe_w_group"].shape[0]):
        rw = jnp.concatenate([W["moe_w_group"][i], W["moe_w_router"][i].reshape(d, N_EXPERTS)], axis=1)
        rb = jnp.concatenate([W["moe_b_group"][i], W["moe_b_router"][i].reshape(N_EXPERTS)])
        padc = LANES - rw.shape[1]
        rw = jnp.pad(rw, ((0, 0), (0, padc)))
        rw_hi = rw.astype(BF16)
        P["router_w", i] = (rw_hi, (rw - rw_hi.astype(F32)).astype(BF16))
        P["router_b", i] = jnp.pad(rb, (0, padc)).reshape(1, LANES)
        P["ple_proj", i] = W["ple_proj"][i].astype(BF16)
        P["ple_gate", i] = W["ple_gate"][i].astype(BF16)
    P["pw1"] = W["conv_pw1"][0].astype(BF16)
    P["pw2"] = W["conv_pw2"][0].astype(BF16)
    P["dw"] = jnp.pad(W["conv_dw"][0], ((0, 1), (0, 0)))
    return P


def _vec(v):
    return v.reshape(1, -1)


def _mixer0(x, pos, W, P, tm, past):
    b, l, d = x.shape
    n = b * l
    h = x.reshape(n, d)
    tabs = _rope_tables(pos, RET_DK) + _rope_tables(pos, MLA_ROPE)
    rq, rk, rv, rg, qlat, qpe, c, kpe, kcat = _proj_in(
        h, _vec(W["norm_mix"][0]), P["w_in"], tabs, _vec(W["mla_q_norm"][0]), _vec(W["mla_kv_norm"][0]),
        P["w_uq_nope"], P["w_uq_pe"], P["wk"], tm)
    gnw = _vec(W["ret_gn_w"][0])
    if past is None:
        o_r, s_new = _retention_prompt(rq, rk, rv, rg, gnw, b, l)
        o_m = _attention_prompt(qlat, qpe, kcat, P["wv"], b, l)
        pre = None
    else:
        ret_state, lat_cache, rope_cache, page_table = past
        o_r, s_new = _retention_decode(rq, rk, rv, rg, gnw, ret_state)
        o_m = _attention_decode(qlat, qpe, c, kpe, lat_cache, rope_cache, page_table)
        o_m = o_m.reshape(n, MLA_HEADS * KV_LORA)
        pre = P["wv_bd"]
    routed = _mix_out([o_r, o_m], [P["w_out_r"], P["w_out_m"]], h, _vec(W["norm_ffn"][0]),
                      P["router_w", 0], P["router_b", 0], tm, pre=pre)
    return routed, (c.reshape(1, b, l, KV_LORA), kpe.reshape(1, b, l, MLA_ROPE), s_new)


def _mixer1(h2, p, W, P, tm, b, l, conv_state):
    n, d = h2.shape
    h3, a = _ple(h2, p.reshape(p.shape[0], n, -1), 0, P["ple_proj", 0], _vec(W["ple_norm"][0]), P["ple_gate", 0],
                 tm, "conv_in", [_vec(W["norm_mix"][1]), P["pw1"], _vec(W["conv_pw1_b"][0])])
    conv_vecs = (_vec(W["conv_dw_b"][0]), _vec(W["conv_ln_w"][0]), _vec(W["conv_ln_b"][0]))
    if conv_state is None:
        yact = _conv_prompt(a, P["dw"], *conv_vecs, b, l)
        new_state = a.reshape(b, l, d)[:, l - (CONV_WIDTH - 1):, :][None]
    else:
        yact, new_state = _conv_decode(conv_state, a, P["dw"], *conv_vecs)
    routed = _mix_out([yact], [P["pw2"]], h3, _vec(W["norm_ffn"][1]),
                      P["router_w", 1], P["router_b", 1], tm, bias=_vec(W["conv_pw2_b"][0]))
    return routed, new_state


def _final(h5, p, W, P, tm, shape):
    n = h5.shape[0]
    (y,) = _ple(h5, p.reshape(p.shape[0], n, -1), 1, P["ple_proj", 1], _vec(W["ple_norm"][1]), P["ple_gate", 1],
                tm, "final", [_vec(W["norm_final"])])
    return y.reshape(shape)


def kernel(x_prompt, x_sample, p_prompt, p_sample, cache_mla_latent, cache_mla_rope, state_retention, state_conv,
           page_table, norm_mix, norm_ffn, norm_final, w_in, ret_gn_w, mla_q_norm, mla_kv_norm, w_uq, w_ukv, w_out,
           conv_pw1, conv_pw1_b, conv_dw, conv_dw_b, conv_ln_w, conv_ln_b, conv_pw2, conv_pw2_b,
           moe_w_group, moe_b_group, moe_w_router, moe_b_router, moe_w_gate, moe_w_up, moe_w_down,
           ple_proj, ple_norm, ple_gate):
    W = dict(norm_mix=norm_mix, norm_ffn=norm_ffn, norm_final=norm_final, w_in=w_in, ret_gn_w=ret_gn_w,
             mla_q_norm=mla_q_norm, mla_kv_norm=mla_kv_norm, w_uq=w_uq, w_ukv=w_ukv, w_out=w_out,
             conv_pw1=conv_pw1, conv_pw1_b=conv_pw1_b, conv_dw=conv_dw, conv_dw_b=conv_dw_b, conv_ln_w=conv_ln_w,
             conv_ln_b=conv_ln_b, conv_pw2=conv_pw2, conv_pw2_b=conv_pw2_b, moe_w_group=moe_w_group,
             moe_b_group=moe_b_group, moe_w_router=moe_w_router, moe_b_router=moe_b_router, moe_w_gate=moe_w_gate,
             moe_w_up=moe_w_up, moe_w_down=moe_w_down, ple_proj=ple_proj, ple_norm=ple_norm, ple_gate=ple_gate)
    assert w_in.shape[0] == 1 and conv_pw1.shape[0] == 1, "one retention/attention layer, one conv layer"
    P = _prep_weights(W)
    past_len = page_table.shape[1] * cache_mla_latent.shape[2]
    pos_prompt = jnp.arange(x_prompt.shape[1])
    pos_sample = past_len + jnp.arange(x_sample.shape[1])
    bp, lp, _ = x_prompt.shape
    bs, ls, _ = x_sample.shape
    tm_p = min(512, lp)
    tm_s = bs * ls
    moe_w = lambda i: (moe_w_gate, moe_w_up, moe_w_down, i)
    routed_p, (lat_p, rope_p, ret_p) = _mixer0(x_prompt, pos_prompt, W, P, tm_p, None)
    routed_s, (lat_s, rope_s, ret_s) = _mixer0(
        x_sample, pos_sample, W, P, tm_s, (state_retention, cache_mla_latent, cache_mla_rope, page_table))
    h2_p, h2_s = _moe([routed_p + (tm_p,), routed_s + (tm_s,)], *moe_w(0))
    routed_p, conv_p = _mixer1(h2_p, p_prompt, W, P, tm_p, bp, lp, None)
    routed_s, conv_s = _mixer1(h2_s, p_sample, W, P, tm_s, bs, ls, state_conv)
    h5_p, h5_s = _moe([routed_p + (tm_p,), routed_s + (tm_s,)], *moe_w(1))
    y_p = _final(h5_p, p_prompt, W, P, tm_p, x_prompt.shape)
    y_s = _final(h5_s, p_sample, W, P, tm_s, x_sample.shape)
    return (y_p, y_s, lat_p, rope_p, ret_p, conv_p, lat_s, rope_s, ret_s, conv_s)
```

```python
import functools

import jax
import jax.numpy as jnp
from jax import lax
from jax.experimental import pallas as pl
from jax.experimental.pallas import tpu as pltpu

F32 = jnp.float32
BF16 = jnp.bfloat16
I32 = jnp.int32

EPS = 1e-6
ROPE_THETA = 10000.0
RET_HEADS = 8
RET_DK = 64
RET_DV = 64
RET_CHUNK = 128
MLA_HEADS = 8
MLA_NOPE = 64
MLA_ROPE = 32
MLA_V = 64
Q_LORA = 256
KV_LORA = 128
CONV_WIDTH = 31
N_GROUPS = 4
EXPERTS_PER_GROUP = 8
N_EXPERTS = N_GROUPS * EXPERTS_PER_GROUP
LANES = 128
Q_BLOCK = 256
KV_BLOCK = 256
VMEM_LIMIT_BYTES = 56 * 1024 * 1024


def _params(*sem):
    return pltpu.CompilerParams(dimension_semantics=sem, vmem_limit_bytes=VMEM_LIMIT_BYTES)


def _rms(x, w):
    return x * lax.rsqrt(jnp.mean(x * x, axis=-1, keepdims=True) + EPS) * w


def _sigmoid(x):
    return 1.0 / (1.0 + jnp.exp(-x))


def _dot(a, b):
    return jnp.dot(a, b, preferred_element_type=F32)


def _dot_nt(a, b):
    return lax.dot_general(a, b, (((1,), (1,)), ((), ())), preferred_element_type=F32)


def _dot_tn(a, b):
    return lax.dot_general(a, b, (((0,), (0,)), ((), ())), preferred_element_type=F32)


def _rope(x, cos_t, sin_t, period):
    half = period // 2
    lane = lax.broadcasted_iota(I32, x.shape, 1)
    first = (lane % period) < half
    rot = jnp.where(first, pltpu.roll(x, LANES - half, 1), pltpu.roll(x, half, 1))
    return x * cos_t + rot * sin_t


def _rope_tables(pos, period):
    half = period // 2
    inv = ROPE_THETA ** (-2.0 * jnp.arange(half, dtype=F32) / period)
    ang = pos.astype(F32)[:, None] * inv[None, :]
    cos = jnp.cos(ang)
    sin = jnp.sin(ang)
    cos_p = jnp.concatenate([cos, cos], axis=1)
    sin_p = jnp.concatenate([-sin, sin], axis=1)
    reps = LANES // period
    return jnp.tile(cos_p, (1, reps)), jnp.tile(sin_p, (1, reps))


def _proj_in_kernel(x_ref, nw_ref, win_ref, cr_ref, sr_ref, cp_ref, sp_ref, qn_ref, kvn_ref,
                    wuqn_ref, wuqp_ref, wk_ref,
                    rq_ref, rk_ref, rv_ref, rg_ref, qlat_ref, qpe_ref, c_ref, kpe_ref, kcat_ref):
    hn = _rms(x_ref[...], nw_ref[...]).astype(BF16)
    cr, sr, cp, sp = cr_ref[...], sr_ref[...], cp_ref[...], sp_ref[...]
    hk = RET_HEADS * RET_DK
    hv = RET_HEADS * RET_DV

    def mm(lo, width):
        return _dot(hn, win_ref[:, lo:lo + width])

    u = mm(0, hk)
    rq_ref[...] = jnp.concatenate(
        [_rope(u[:, g:g + LANES], cr, sr, RET_DK) for g in range(0, hk, LANES)], axis=1).astype(BF16)
    u = mm(hk, hk)
    rk_ref[...] = jnp.concatenate(
        [_rope(u[:, g:g + LANES], cr, sr, RET_DK) * (RET_DK ** -0.5) for g in range(0, hk, LANES)],
        axis=1).astype(BF16)
    rv_ref[...] = mm(2 * hk, hv).astype(BF16)
    rg_ref[...] = mm(2 * hk + hv, hv)
    off = 2 * hk + 2 * hv
    cqn = _rms(mm(off, Q_LORA), qn_ref[...]).astype(BF16)
    c = _rms(mm(off + Q_LORA, KV_LORA), kvn_ref[...])
    c_ref[...] = c
    kp = _rope(mm(off + Q_LORA + KV_LORA, LANES), cp, sp, MLA_ROPE)
    kpe_ref[...] = kp[:, :MLA_ROPE]
    kcat_ref[...] = jnp.concatenate([c, kp], axis=1).astype(BF16)
    scale = (MLA_NOPE + MLA_ROPE) ** -0.5
    qn = _dot(cqn, wuqn_ref[...])
    for h in range(MLA_HEADS):
        ql = _dot(qn[:, h * LANES:(h + 1) * LANES].astype(BF16), wk_ref[h]) * scale
        qlat_ref[:, h * KV_LORA:(h + 1) * KV_LORA] = ql.astype(BF16)
    qp = _dot(cqn, wuqp_ref[...])
    qpe_ref[...] = jnp.concatenate(
        [_rope(qp[:, g:g + LANES], cp, sp, MLA_ROPE) * scale
         for g in range(0, MLA_HEADS * MLA_ROPE, LANES)], axis=1).astype(BF16)


def _proj_in(x2, nw, win, tabs, qn, kvn, wuqn, wuqp, wk, tm):
    n, d = x2.shape
    cr, sr, cp, sp = tabs
    ltab = cr.shape[0]
    nblk = ltab // tm if ltab >= tm else 0
    tab_rows = tm if nblk else ltab
    tab_map = (lambda i: (i % nblk, 0)) if nblk else (lambda i: (0, 0))
    row = lambda i: (i, 0)
    const = lambda i: (0, 0)
    hk = RET_HEADS * RET_DK
    hv = RET_HEADS * RET_DV
    tab_spec = pl.BlockSpec((tab_rows, LANES), tab_map)
    out_cols = [(hk, BF16), (hk, BF16), (hv, BF16), (hv, F32), (MLA_HEADS * KV_LORA, BF16),
                (MLA_HEADS * MLA_ROPE, BF16), (KV_LORA, F32), (MLA_ROPE, F32), (2 * LANES, BF16)]
    return pl.pallas_call(
        _proj_in_kernel,
        grid=(n // tm,),
        in_specs=[pl.BlockSpec((tm, d), row), pl.BlockSpec((1, d), const),
                  pl.BlockSpec(win.shape, const), tab_spec, tab_spec, tab_spec, tab_spec,
                  pl.BlockSpec((1, Q_LORA), const), pl.BlockSpec((1, KV_LORA), const),
                  pl.BlockSpec(wuqn.shape, const), pl.BlockSpec(wuqp.shape, const),
                  pl.BlockSpec(wk.shape, lambda i: (0, 0, 0))],
        out_specs=[pl.BlockSpec((tm, w), row) for w, _ in out_cols],
        out_shape=[jax.ShapeDtypeStruct((n, w), dt) for w, dt in out_cols],
        compiler_params=_params("parallel"),
        name="proj_in",
    )(x2, nw, win, cr, sr, cp, sp, qn, kvn, wuqn, wuqp, wk)


def _ret_kernel(rq_ref, rk_ref, rv_ref, rg_ref, gnw_ref, decay_ref, qdec_ref, kdec_ref, cdec_ref,
                o_ref, s_out_ref, s_scr):
    ci = pl.program_id(1)

    @pl.when(ci == 0)
    def _():
        s_scr[...] = jnp.zeros_like(s_scr)

    outs = []
    for h in range(RET_HEADS):
        sl = slice(h * RET_DK, (h + 1) * RET_DK)
        q = rq_ref[:, sl]
        k = rk_ref[:, sl]
        v = rv_ref[:, sl]
        s = s_scr[h]
        sc = _dot_nt(q, k) * decay_ref[h]
        o = _dot(sc.astype(BF16), v) + _dot(q, s.astype(BF16)) * qdec_ref[:, sl]
        kd = (k.astype(F32) * kdec_ref[:, sl]).astype(BF16)
        s_scr[h] = s * cdec_ref[:, sl] + _dot_tn(kd, v)
        mu = jnp.mean(o, axis=-1, keepdims=True)
        var = jnp.mean(jnp.square(o - mu), axis=-1, keepdims=True)
        outs.append((o - mu) * lax.rsqrt(var + EPS))
    g = rg_ref[...]
    o_ref[...] = (jnp.concatenate(outs, axis=1) * gnw_ref[...] * (g * _sigmoid(g))).astype(BF16)

    @pl.when(ci == pl.num_programs(1) - 1)
    def _():
        s_out_ref[...] = s_scr[...]


def _ret_tables(chunk):
    h = RET_HEADS
    log_g = jnp.log1p(-jnp.exp2(-5.0 - jnp.arange(h, dtype=F32)))
    idx = jnp.arange(chunk, dtype=F32)
    rel = idx[:, None] - idx[None, :]
    decay = jnp.where(rel[None] >= 0, jnp.exp(jnp.maximum(rel, 0.0)[None] * log_g[:, None, None]), 0.0)
    q_dec = jnp.exp((idx[:, None] + 1.0) * log_g[None, :])
    k_dec = jnp.exp((chunk - 1.0 - idx)[:, None] * log_g[None, :])
    c_dec = jnp.exp(chunk * log_g)
    rep = lambda t: jnp.repeat(t, RET_DK, axis=-1)
    return decay, rep(q_dec), rep(k_dec), rep(c_dec[None, :])


def _retention_prompt(rq, rk, rv, rg, gnw, b, l):
    chunk = RET_CHUNK if l % RET_CHUNK == 0 else l
    decay, qdec, kdec, cdec = _ret_tables(chunk)
    hk = RET_HEADS * RET_DK
    n_chunks = l // chunk
    row = lambda bi, ci: (bi * n_chunks + ci, 0)
    const = lambda bi, ci: (0, 0)
    return pl.pallas_call(
        _ret_kernel,
        grid=(b, n_chunks),
        in_specs=[pl.BlockSpec((chunk, hk), row)] * 4 + [
            pl.BlockSpec((1, hk), const),
            pl.BlockSpec(decay.shape, lambda bi, ci: (0, 0, 0)),
            pl.BlockSpec((chunk, hk), const), pl.BlockSpec((chunk, hk), const),
            pl.BlockSpec((1, hk), const)],
        out_specs=[pl.BlockSpec((chunk, hk), row),
                   pl.BlockSpec((None, None, RET_HEADS, RET_DK, RET_DV), lambda bi, ci: (0, bi, 0, 0, 0))],
        out_shape=[jax.ShapeDtypeStruct((b * l, hk), BF16),
                   jax.ShapeDtypeStruct((1, b, RET_HEADS, RET_DK, RET_DV), F32)],
        scratch_shapes=[pltpu.VMEM((RET_HEADS, RET_DK, RET_DV), F32)],
        compiler_params=_params("parallel", "arbitrary"),
        name="retention_prompt",
    )(rq, rk, rv, rg, gnw, decay, qdec, kdec, cdec)


def _ret_decode_kernel(q_ref, k_ref, v_ref, g_ref, gnw_ref, gam_ref, s_ref, o_ref, s_out_ref):
    gam = gam_ref[...]
    q = q_ref[...]
    k = k_ref[...]
    v = v_ref[...]
    qg = q * gam
    cross = jnp.zeros_like(v)
    for d in range(q.shape[0]):
        s_d = s_ref[d]
        cross = cross + qg[d:d + 1, :] * s_d
        s_out_ref[d] = s_d * gam + k[d:d + 1, :] * v
    o = jnp.sum(q * k, axis=0, keepdims=True) * v + cross
    mu = jnp.mean(o, axis=0, keepdims=True)
    var = jnp.mean(jnp.square(o - mu), axis=0, keepdims=True)
    g = g_ref[...]
    o_ref[...] = (o - mu) * lax.rsqrt(var + EPS) * gnw_ref[...] * (g * _sigmoid(g))


def _retention_decode(rq, rk, rv, rg, gnw, state):
    b = rq.shape[0]
    h, dk, dv = RET_HEADS, RET_DK, RET_DV
    gam = jnp.exp(jnp.log1p(-jnp.exp2(-5.0 - jnp.arange(h, dtype=F32)))).reshape(h, 1, 1)
    tr = lambda t: t.astype(F32).T
    vec = lambda d: pl.BlockSpec((d, b), lambda i: (i, 0))
    sspec = pl.BlockSpec((None, None, dk, dv, b), lambda i: (0, i, 0, 0, 0))
    o, s_new = pl.pallas_call(
        _ret_decode_kernel,
        grid=(h,),
        in_specs=[vec(dk), vec(dk), vec(dv), vec(dv), pl.BlockSpec((dv, 1), lambda i: (i, 0)),
                  pl.BlockSpec((None, 1, 1), lambda i: (i, 0, 0)), sspec],
        out_specs=[vec(dv), sspec],
        out_shape=[jax.ShapeDtypeStruct((h * dv, b), F32),
                   jax.ShapeDtypeStruct((1, h, dk, dv, b), F32)],
        compiler_params=_params("parallel"),
        name="retention_decode",
    )(tr(rq), tr(rk), tr(rv), tr(rg), gnw.reshape(h * dv, 1), gam, jnp.transpose(state, (0, 2, 3, 4, 1)))
    return o.T, jnp.transpose(s_new, (0, 4, 1, 2, 3))


def _attn_kernel(qlat_ref, qpe_ref, kcat_ref, wv_ref, o_ref, q_scr, m_scr, acc_scr, s_scr, p_scr, a_scr):
    i = pl.program_id(1)
    lane = lax.broadcasted_iota(I32, (Q_BLOCK, LANES), 1)
    per_group = LANES // MLA_ROPE
    parts = []
    for h in range(MLA_HEADS):
        g, r = divmod(h, per_group)
        pe = qpe_ref[:, g * LANES:(g + 1) * LANES]
        keep = (lane >= r * MLA_ROPE) & (lane < (r + 1) * MLA_ROPE)
        pe = jnp.where(keep, pe, jnp.zeros_like(pe))
        parts.append(jnp.concatenate([qlat_ref[:, h * KV_LORA:(h + 1) * KV_LORA], pe], axis=1))
    q_scr[...] = jnp.concatenate(parts, axis=0)
    rows = MLA_HEADS * Q_BLOCK
    m_scr[...] = jnp.full_like(m_scr, -jnp.inf)
    acc_scr[...] = jnp.zeros_like(acc_scr)
    ones = jnp.ones((KV_BLOCK, LANES), BF16)

    def keys(j):
        return kcat_ref[pl.ds(pl.multiple_of(j * KV_BLOCK, KV_BLOCK), KV_BLOCK), :]

    def scores(j, par):
        s_scr[par] = _dot_nt(q_scr[...], keys(j))

    def softmax(j, par, masked):
        s = s_scr[par]
        if masked:
            qpos = i * Q_BLOCK + lax.broadcasted_iota(I32, (rows, KV_BLOCK), 0) % Q_BLOCK
            kpos = j * KV_BLOCK + lax.broadcasted_iota(I32, (rows, KV_BLOCK), 1)
            s = jnp.where(kpos <= qpos, s, -jnp.inf)
        s0, s1 = s[:, :LANES], s[:, LANES:]
        m_old = m_scr[...]
        m_new = jnp.maximum(m_old, jnp.max(jnp.maximum(s0, s1), axis=-1, keepdims=True))
        a_scr[par] = jnp.exp(m_old - m_new)
        p_scr[par] = jnp.concatenate([jnp.exp(s0 - m_new), jnp.exp(s1 - m_new)], axis=1).astype(BF16)
        m_scr[...] = m_new

    def accumulate(j, par):
        alpha = a_scr[par]
        vext = jnp.concatenate([keys(j)[:, :KV_LORA], ones], axis=1)
        acc_scr[...] = jnp.concatenate([alpha, alpha], axis=1) * acc_scr[...] + _dot(p_scr[par], vext)

    def step(t, par):
        scores(t + 1, 1 - par)
        softmax(t, par, False)
        accumulate(t - 1, 1 - par)

    n_full = (i * Q_BLOCK) // KV_BLOCK
    scores(0, 0)

    @pl.when(n_full >= 1)
    def _():
        scores(1, 1)
        softmax(0, 0, False)

    def body(u, carry):
        step(2 * u + 1, 1)
        step(2 * u + 2, 0)
        return carry

    n_pairs = jnp.maximum(n_full - 1, 0) // 2
    lax.fori_loop(0, n_pairs, body, 0)
    even = n_full % 2 == 0

    @pl.when(even & (n_full >= 2))
    def _():
        step(n_full - 1, 1)

    @pl.when(even)
    def _():
        softmax(n_full, 0, True)

        @pl.when(n_full >= 1)
        def _():
            accumulate(n_full - 1, 1)

        accumulate(n_full, 0)

    @pl.when(jnp.logical_not(even))
    def _():
        softmax(n_full, 1, True)
        accumulate(n_full - 1, 0)
        accumulate(n_full, 1)

    o = acc_scr[:, :KV_LORA] / acc_scr[:, KV_LORA:]
    outs = [_dot(o[h * Q_BLOCK:(h + 1) * Q_BLOCK].astype(BF16), wv_ref[h]) for h in range(MLA_HEADS)]
    o_ref[...] = jnp.concatenate(outs, axis=1).astype(BF16)


def _attention_prompt(qlat, qpe, kcat, wv, b, l):
    nq = l // Q_BLOCK
    rows = MLA_HEADS * Q_BLOCK
    qmap = lambda bi, i: (bi * nq + i, 0)
    return pl.pallas_call(
        _attn_kernel,
        grid=(b, nq),
        in_specs=[pl.BlockSpec((Q_BLOCK, qlat.shape[1]), qmap),
                  pl.BlockSpec((Q_BLOCK, qpe.shape[1]), qmap),
                  pl.BlockSpec((l, kcat.shape[1]), lambda bi, i: (bi, 0)),
                  pl.BlockSpec(wv.shape, lambda bi, i: (0, 0, 0))],
        out_specs=pl.BlockSpec((Q_BLOCK, MLA_HEADS * MLA_V), qmap),
        out_shape=jax.ShapeDtypeStruct((b * l, MLA_HEADS * MLA_V), BF16),
        scratch_shapes=[pltpu.VMEM((rows, 2 * LANES), BF16), pltpu.VMEM((rows, LANES), F32),
                        pltpu.VMEM((rows, 2 * KV_LORA), F32), pltpu.VMEM((2, rows, KV_BLOCK), F32),
                        pltpu.VMEM((2, rows, KV_BLOCK), BF16), pltpu.VMEM((2, rows, LANES), F32)],
        compiler_params=_params("parallel", "arbitrary"),
        name="attention_prompt",
    )(qlat, qpe, kcat, wv)


def _attn_decode_kernel(pt_ref, ql_ref, qp_ref, cn_ref, kn_ref, lat_hbm, rope_hbm, o_ref,
                        kbuf, rbuf, sem):
    b = pl.program_id(0)
    nb = pl.num_programs(0)
    n_pages = pt_ref.shape[1]
    page = lat_hbm.shape[2]

    def copies(seq, slot, p):
        phys = pt_ref[seq, p]
        dst = pl.ds(p * page, page)
        return (pltpu.make_async_copy(lat_hbm.at[0, phys], kbuf.at[slot, dst], sem.at[slot, 0]),
                pltpu.make_async_copy(rope_hbm.at[0, phys], rbuf.at[slot, :, dst], sem.at[slot, 1]))

    def start_all(seq, slot):
        def body(p, carry):
            for cp in copies(seq, slot, p):
                cp.start()
            return carry
        lax.fori_loop(0, n_pages, body, 0)

    def wait_all(seq, slot):
        def body(p, carry):
            for cp in copies(seq, slot, p):
                cp.wait()
            return carry
        lax.fori_loop(0, n_pages, body, 0)

    slot = b % 2

    @pl.when(b == 0)
    def _():
        start_all(0, 0)

    @pl.when(b + 1 < nb)
    def _():
        start_all(b + 1, 1 - slot)

    wait_all(b, slot)
    ql = ql_ref[...]
    qp = qp_ref[...]
    kc = kbuf[slot].astype(BF16)
    kr = rbuf[slot].astype(BF16)
    s_past = _dot_nt(ql, kc) + _dot(qp, kr)
    cn = cn_ref[...].astype(BF16).astype(F32)
    kn = kn_ref[...].astype(BF16).astype(F32)
    s_new = (jnp.sum(ql.astype(F32) * cn, axis=-1, keepdims=True)
             + jnp.sum(qp.astype(F32) * kn, axis=-1, keepdims=True))
    m = jnp.maximum(jnp.max(s_past, axis=-1, keepdims=True), s_new)
    p_past = jnp.exp(s_past - m)
    p_new = jnp.exp(s_new - m)
    denom = jnp.sum(p_past, axis=-1, keepdims=True) + p_new
    o = _dot(p_past.astype(BF16), kc) + p_new.astype(BF16).astype(F32) * cn
    o_ref[...] = o / denom


def _attention_decode(qlat, qpe, c_new, kpe_new, lat_cache, rope_cache, page_table):
    b = qlat.shape[0]
    h = MLA_HEADS
    n_pages = page_table.shape[1]
    page = lat_cache.shape[2]
    past = n_pages * page
    blk = lambda w: pl.BlockSpec((None, h, w), lambda i, pt: (i, 0, 0))
    one = lambda w: pl.BlockSpec((None, 1, w), lambda i, pt: (i, 0, 0))
    grid_spec = pltpu.PrefetchScalarGridSpec(
        num_scalar_prefetch=1,
        grid=(b,),
        in_specs=[blk(KV_LORA), blk(MLA_ROPE), one(KV_LORA), one(MLA_ROPE),
                  pl.BlockSpec(memory_space=pl.ANY), pl.BlockSpec(memory_space=pl.ANY)],
        out_specs=blk(KV_LORA),
        scratch_shapes=[pltpu.VMEM((2, past, KV_LORA), F32), pltpu.VMEM((2, MLA_ROPE, past), F32),
                        pltpu.SemaphoreType.DMA((2, 2))],
    )
    return pl.pallas_call(
        _attn_decode_kernel,
        grid_spec=grid_spec,
        out_shape=jax.ShapeDtypeStruct((b, h, KV_LORA), F32),
        compiler_params=_params("arbitrary"),
        name="attention_decode",
    )(page_table, qlat.reshape(b, h, KV_LORA), qpe.reshape(b, h, MLA_ROPE),
      c_new.reshape(b, 1, KV_LORA), kpe_new.reshape(b, 1, MLA_ROPE), lat_cache,
      jnp.swapaxes(rope_cache, 2, 3))


def _route(logits):
    lane = lax.broadcasted_iota(I32, logits.shape, 1)
    lane_f = lane.astype(F32)
    neg = -jnp.inf
    big = float(LANES)
    gmask = lane < N_GROUPS
    gl = jnp.where(gmask, logits, neg)
    gmax = jnp.max(gl, axis=-1, keepdims=True)
    g_sel = jnp.min(jnp.where(gl == gmax, lane_f, big), axis=-1, keepdims=True)
    gsum = jnp.sum(jnp.where(gmask, jnp.exp(gl - gmax), 0.0), axis=-1, keepdims=True)
    g_w = 1.0 / gsum
    lo = N_GROUPS + EXPERTS_PER_GROUP * g_sel
    emask = (lane_f >= lo) & (lane_f < lo + EXPERTS_PER_GROUP)
    el = jnp.where(emask, logits, neg)
    v1 = jnp.max(el, axis=-1, keepdims=True)
    i1 = jnp.min(jnp.where(el == v1, lane_f, big), axis=-1, keepdims=True)
    el2 = jnp.where(lane_f == i1, neg, el)
    v2 = jnp.max(el2, axis=-1, keepdims=True)
    i2 = jnp.min(jnp.where(el2 == v2, lane_f, big), axis=-1, keepdims=True)
    e2 = jnp.exp(v2 - v1)
    w1 = g_w / (1.0 + e2)
    w2 = g_w * e2 / (1.0 + e2)
    return i1 - N_GROUPS, i2 - N_GROUPS, w1, w2


def _bf16_parts(w):
    hi = w.astype(BF16).astype(F32)
    mid = (w - hi).astype(BF16).astype(F32)
    return hi, mid, w - hi - mid


def _lane_pack(shape, cols):
    lane = lax.broadcasted_iota(I32, shape, 1)
    out = jnp.zeros(shape, F32)
    for k, col in enumerate(cols):
        out = jnp.where(lane == k, col, out)
    return out


def _mix_out_kernel(*refs, n_act, has_pre, has_bias):
    refs = list(refs)
    acts = [refs.pop(0) for _ in range(n_act)]
    ws = [refs.pop(0) for _ in range(n_act)]
    pre = refs.pop(0) if has_pre else None
    bias = refs.pop(0) if has_bias else None
    h_ref, nw_ref, rwh_ref, rwl_ref, rb_ref, h1_ref, xe_ref, rid_ref, cnt_ref = refs
    out = None
    for k, (a_ref, w_ref) in enumerate(zip(acts, ws)):
        a = a_ref[...]
        if has_pre and k == n_act - 1:
            a = _dot(a.astype(BF16), pre[...])
        t = _dot(a.astype(BF16), w_ref[...])
        out = t if out is None else out + t
    if has_bias:
        out = out + bias[...]
    h1 = h_ref[...] + out
    h1_ref[...] = h1
    xn = _rms(h1, nw_ref[...])
    xh = xn.astype(BF16)
    xl = (xn - xh.astype(F32)).astype(BF16)
    logits = _dot(xh, rwh_ref[...]) + _dot(xl, rwh_ref[...]) + _dot(xh, rwl_ref[...]) + rb_ref[...]
    e1, e2, w1, w2 = _route(logits)
    meta = _lane_pack(logits.shape, (e1, e2) + _bf16_parts(w1) + _bf16_parts(w2))
    xe_ref[...] = jnp.concatenate([xn, meta], axis=1).astype(BF16)
    rid_ref[...] = _lane_pack(logits.shape, (e1, e2)).astype(I32)
    lane_f = lax.broadcasted_iota(I32, logits.shape, 1).astype(F32)
    hit = jnp.where((lane_f == e1) | (lane_f == e2), 1.0, 0.0)
    cnt_ref[...] = jnp.sum(hit, axis=0, keepdims=True)


def _mix_out(acts, ws, h, nw, rw, rb, tm, pre=None, bias=None):
    n, d = h.shape
    row = lambda i: (i, 0)
    const = lambda i: (0, 0)
    ins = list(acts) + list(ws)
    specs = [pl.BlockSpec((tm, a.shape[1]), row) for a in acts] + [pl.BlockSpec(w.shape, const) for w in ws]
    if pre is not None:
        ins.append(pre)
        specs.append(pl.BlockSpec(pre.shape, const))
    if bias is not None:
        ins.append(bias)
        specs.append(pl.BlockSpec(bias.shape, const))
    ins += [h, nw, rw[0], rw[1], rb]
    specs += [pl.BlockSpec((tm, d), row), pl.BlockSpec((1, d), const),
              pl.BlockSpec(rw[0].shape, const), pl.BlockSpec(rw[1].shape, const), pl.BlockSpec(rb.shape, const)]
    kern = functools.partial(_mix_out_kernel, n_act=len(acts), has_pre=pre is not None,
                             has_bias=bias is not None)
    outs = pl.pallas_call(
        kern,
        grid=(n // tm,),
        in_specs=specs,
        out_specs=[pl.BlockSpec((tm, d), row), pl.BlockSpec((tm, d + LANES), row),
                   pl.BlockSpec((tm, LANES), row), pl.BlockSpec((None, 1, LANES), lambda i: (i, 0, 0))],
        out_shape=[jax.ShapeDtypeStruct((n, d), F32), jax.ShapeDtypeStruct((n, d + LANES), BF16),
                   jax.ShapeDtypeStruct((n, LANES), I32), jax.ShapeDtypeStruct((n // tm, 1, LANES), F32)],
        compiler_params=_params("parallel"),
        name="mix_out_route",
    )(*ins)
    return tuple(outs)


SEG_ALIGN = 16
FFN_ROWS = 512
PERM_ROWS = 256


def _seg_bits(t):
    bits = []
    b = SEG_ALIGN
    while b <= t:
        bits.append(b)
        b *= 2
    return tuple(reversed(bits))


def _slot_cap(t):
    cap = 2 * t + N_EXPERTS * (SEG_ALIGN - 1)
    return -(-cap // PERM_ROWS) * PERM_ROWS


def _moe_meta(counts, tile_tokens):
    cnt = counts[:, :N_EXPERTS].astype(I32)
    ell = ((cnt + SEG_ALIGN - 1) // SEG_ALIGN) * SEG_ALIGN
    off = jnp.cumsum(ell, axis=1) - ell
    tot = jnp.sum(ell, axis=0)
    region = ((tot + FFN_ROWS - 1) // FFN_ROWS) * FFN_ROWS
    gend = jnp.cumsum(region)
    gbase = gend - region
    gseg = gbase[None, :] + jnp.cumsum(ell, axis=0) - ell
    n_pairs = 2 * sum(tile_tokens)
    n_rows_max = n_pairs + len(tile_tokens) * N_EXPERTS * (SEG_ALIGN - 1) + N_EXPERTS * (FFN_ROWS - 1)
    n_steps = -(-n_rows_max // FFN_ROWS)
    starts = jnp.arange(n_steps, dtype=I32) * FFN_ROWS
    step_expert = jnp.minimum(jnp.sum((gend[None, :] <= starts[:, None]).astype(I32), axis=1), N_EXPERTS - 1)
    n_used = (gend[-1] // FFN_ROWS).reshape(1)
    off_f = off.astype(F32)
    return dict(off=off, ell=ell, gseg=gseg, tail_start=gbase + tot, tail_len=region - tot, end=gend[-1:],
                step_expert=step_expert, n_used=n_used, n_rows=n_steps * FFN_ROWS,
                off_col=off_f[:, :, None], off_row=jnp.pad(off_f, ((0, 0), (0, LANES - N_EXPERTS)))[:, None, :])


def _chunk_copies(n, bits, make):
    for b in bits:
        hi = n & (-(2 * b))

        @pl.when((n & b) != 0)
        def _():
            make(hi, b)


def _segment_copies(n, make):
    big = 2 * SEG_ALIGN
    shift = big.bit_length() - 1
    n_big = lax.shift_right_logical(n, shift)

    def body(k, carry):
        make(lax.shift_left(k, shift), big)
        return carry

    lax.fori_loop(0, n_big, body, 0)

    @pl.when((n & SEG_ALIGN) != 0)
    def _():
        make(lax.shift_left(n_big, shift), SEG_ALIGN)


def _pair_slots_rows(ids_t, off_col, t):
    sub = lax.broadcasted_iota(I32, (N_EXPERTS, t), 0).astype(F32)
    e1 = jnp.where(sub == ids_t[0:1, :], 1.0, 0.0)
    e2 = jnp.where(sub == ids_t[1:2, :], 1.0, 0.0)
    before = jnp.where(lax.broadcasted_iota(I32, (t, t), 0) < lax.broadcasted_iota(I32, (t, t), 1), 1.0, 0.0)
    before = before.astype(BF16)
    cum1 = _dot(e1.astype(BF16), before)
    cum2 = _dot(e2.astype(BF16), before)
    cnt1 = jnp.sum(e1, axis=1, keepdims=True)
    slot0 = jnp.sum(e1 * (off_col + cum1), axis=0, keepdims=True)
    slot1 = jnp.sum(e2 * (off_col + cnt1 + cum2), axis=0, keepdims=True)
    return slot0, slot1


def _pair_slots_cols(ids, off_row, t):
    lane = lax.broadcasted_iota(I32, (t, LANES), 1).astype(F32)
    idf = ids.astype(F32)
    e1 = jnp.where(lane == idf[:, 0:1], 1.0, 0.0)
    e2 = jnp.where(lane == idf[:, 1:2], 1.0, 0.0)
    before = jnp.where(lax.broadcasted_iota(I32, (t, t), 1) < lax.broadcasted_iota(I32, (t, t), 0), 1.0, 0.0)
    before = before.astype(BF16)
    cum1 = _dot(before, e1.astype(BF16))
    cum2 = _dot(before, e2.astype(BF16))
    cnt1 = jnp.sum(e1, axis=0, keepdims=True)
    slot0 = jnp.sum(e1 * (off_row + cum1), axis=1, keepdims=True)
    slot1 = jnp.sum(e2 * (off_row + cnt1 + cum2), axis=1, keepdims=True)
    return slot0, slot1


def _permute_kernel(off_ref, ell_ref, gseg_ref, tstart_ref, tlen_ref, end_ref, xe_ref, ids_ref, offcol_ref, *rest,
                    tile0, zero_tiles):
    xg_hbm, xs_scr, zero_scr, sem = rest[-4:]
    s = pl.program_id(0)
    last = pl.num_programs(0) - 1
    t = xe_ref.shape[0]
    cap = xs_scr.shape[1]
    slot = s % 2

    def seg_start(tile, buf):
        def body(e, carry):
            o = off_ref[tile, e]
            g = gseg_ref[tile, e]

            def make(hi, b):
                src = xs_scr.at[buf, pl.ds(pl.multiple_of(o + hi, SEG_ALIGN), b)]
                dst = xg_hbm.at[pl.ds(pl.multiple_of(g + hi, SEG_ALIGN), b)]
                pltpu.make_async_copy(src, dst, sem.at[buf]).start()

            _segment_copies(ell_ref[tile, e], make)
            return carry

        lax.fori_loop(0, N_EXPERTS, body, 0)

    def seg_wait(tile, buf):
        def make(hi, b):
            pltpu.make_async_copy(xs_scr.at[buf, pl.ds(0, b)], xg_hbm.at[pl.ds(0, b)], sem.at[buf]).wait()

        _chunk_copies(off_ref[tile, N_EXPERTS - 1] + ell_ref[tile, N_EXPERTS - 1], _seg_bits(cap), make)

    if zero_tiles is not None:
        zrows = zero_scr.shape[0]

        def zero_fill(g, n, fn):
            def make(hi, b):
                dst = xg_hbm.at[pl.ds(pl.multiple_of(g + hi, SEG_ALIGN), b)]
                fn(pltpu.make_async_copy(zero_scr.at[pl.ds(0, b)], dst, sem.at[2]))

            _chunk_copies(n, _seg_bits(zrows), make)

        def zero_copies(fn):
            def body(e, carry):
                zero_fill(tstart_ref[e], tlen_ref[e], fn)
                for zt in zero_tiles:
                    zero_fill(gseg_ref[zt, e], ell_ref[zt, e], fn)
                return carry

            lax.fori_loop(0, N_EXPERTS, body, 0)

            def trailing(k, carry):
                dst = xg_hbm.at[pl.ds(pl.multiple_of(end_ref[0] + k * zrows, SEG_ALIGN), zrows)]
                fn(pltpu.make_async_copy(zero_scr, dst, sem.at[2]))
                return carry

            n_trailing = lax.shift_right_logical(xg_hbm.shape[0] - end_ref[0], zrows.bit_length() - 1)
            lax.fori_loop(0, n_trailing, trailing, 0)

        @pl.when(s == 0)
        def _():
            zero_scr[...] = jnp.zeros_like(zero_scr)
            zero_copies(lambda cp: cp.start())
            zero_copies(lambda cp: cp.wait())

    @pl.when(s >= 2)
    def _():
        seg_wait(tile0 + s - 2, slot)

    tile = tile0 + s
    ids_t = ids_ref[...].astype(F32).T
    slot0, slot1 = _pair_slots_rows(ids_t, offcol_ref[...], t)
    used = off_ref[tile, N_EXPERTS - 1] + ell_ref[tile, N_EXPERTS - 1]
    xe = xe_ref[...]
    for r0 in range(0, cap, PERM_ROWS):
        @pl.when(r0 < used)
        def _():
            rows = (lax.broadcasted_iota(I32, (PERM_ROWS, t), 0) + r0).astype(F32)
            pm = jnp.where((rows == slot0) | (rows == slot1), 1.0, 0.0).astype(BF16)
            xs_scr[slot, pl.ds(r0, PERM_ROWS), :] = _dot(pm, xe).astype(BF16)

    seg_start(tile, slot)

    @pl.when(s == last)
    def _():
        @pl.when(s >= 1)
        def _():
            seg_wait(tile - 1, 1 - slot)

        seg_wait(tile, slot)


def _moe_permute(meta, xe, ids, t, tile0, xg_prev=None):
    n, de = xe.shape
    cap = _slot_cap(t)
    n_tiles = meta["off"].shape[0]
    zero_tiles = None if xg_prev is not None else tuple(range(tile0 + n // t, n_tiles))
    prefetch = [meta["off"], meta["ell"], meta["gseg"], meta["tail_start"], meta["tail_len"], meta["end"]]
    imap = lambda s, *_: (s, 0)
    in_specs = [pl.BlockSpec((t, de), imap), pl.BlockSpec((t, LANES), imap),
                pl.BlockSpec((None, N_EXPERTS, 1), lambda s, *_: (tile0 + s, 0, 0))]
    ins = [xe, ids, meta["off_col"]]
    aliases = {}
    if xg_prev is not None:
        in_specs.append(pl.BlockSpec(memory_space=pl.ANY))
        ins.append(xg_prev)
        aliases = {len(prefetch) + len(ins) - 1: 0}
    grid_spec = pltpu.PrefetchScalarGridSpec(
        num_scalar_prefetch=len(prefetch),
        grid=(n // t,),
        in_specs=in_specs,
        out_specs=pl.BlockSpec(memory_space=pl.ANY),
        scratch_shapes=[pltpu.VMEM((2, cap, de), BF16), pltpu.VMEM((FFN_ROWS // 2, de), BF16),
                        pltpu.SemaphoreType.DMA((3,))],
    )
    return pl.pallas_call(
        functools.partial(_permute_kernel, tile0=tile0, zero_tiles=zero_tiles),
        grid_spec=grid_spec,
        out_shape=jax.ShapeDtypeStruct((meta["n_rows"], de), BF16),
        input_output_aliases=aliases,
        compiler_params=_params("arbitrary"),
        name="moe_permute",
    )(*prefetch, *ins)


def _ffn_kernel(te_ref, nu_ref, x_ref, wg_ref, wu_ref, wd_ref, o_ref, wg_b, wu_b, wd_b):
    i = pl.program_id(0)
    e = te_ref[i]
    d = o_ref.shape[1]

    @pl.when((i == 0) | (e != te_ref[jnp.maximum(i - 1, 0)]))
    def _():
        wg_b[...] = wg_ref[...].astype(BF16)
        wu_b[...] = wu_ref[...].astype(BF16)
        wd_b[...] = wd_ref[...].astype(BF16)

    @pl.when(i < nu_ref[0])
    def _():
        x = x_ref[:, :d]
        meta = x_ref[:, d:].astype(F32)
        w1 = meta[:, 2:3] + meta[:, 3:4] + meta[:, 4:5]
        w2 = meta[:, 5:6] + meta[:, 6:7] + meta[:, 7:8]
        w = jnp.where(meta[:, 0:1] == e.astype(F32), w1, w2)
        g = _dot(x, wg_b[...])
        u = _dot(x, wu_b[...])
        hmid = (g * _sigmoid(g) * u).astype(BF16)
        o_ref[...] = (_dot(hmid, wd_b[...]) * w).astype(BF16)

    @pl.when(i >= nu_ref[0])
    def _():
        o_ref[...] = jnp.zeros_like(o_ref)


def _moe_ffn(xs, tile_expert, n_used, wg, wu, wd, tr, layer):
    n_rows, d = xs.shape[0], wg.shape[2]
    de = wg.shape[3]
    wmap = lambda i, te, nu: (layer, te[i], 0, 0)
    grid_spec = pltpu.PrefetchScalarGridSpec(
        num_scalar_prefetch=2,
        grid=(n_rows // tr,),
        in_specs=[pl.BlockSpec((tr, xs.shape[1]), lambda i, te, nu: (i, 0)),
                  pl.BlockSpec((None, None, d, de), wmap),
                  pl.BlockSpec((None, None, d, de), wmap),
                  pl.BlockSpec((None, None, de, d), wmap)],
        out_specs=pl.BlockSpec((tr, d), lambda i, te, nu: (i, 0)),
        scratch_shapes=[pltpu.VMEM((d, de), BF16), pltpu.VMEM((d, de), BF16), pltpu.VMEM((de, d), BF16)],
    )
    return pl.pallas_call(
        _ffn_kernel,
        grid_spec=grid_spec,
        out_shape=jax.ShapeDtypeStruct((n_rows, d), BF16),
        compiler_params=_params("arbitrary"),
        name="moe_ffn",
    )(tile_expert, n_used, xs, wg, wu, wd)


def _unpermute_kernel(off_ref, ell_ref, gseg_ref, ids_ref, offrow_ref, h_ref, yg_hbm, o_ref, ys_scr, sem,
                      *, tile0):
    s = pl.program_id(0)
    t = h_ref.shape[0]
    cap = ys_scr.shape[1]
    slot = s % 2

    def seg_start(tile, buf):
        def body(e, carry):
            o = off_ref[tile, e]
            g = gseg_ref[tile, e]

            def make(hi, b):
                src = yg_hbm.at[pl.ds(pl.multiple_of(g + hi, SEG_ALIGN), b)]
                dst = ys_scr.at[buf, pl.ds(pl.multiple_of(o + hi, SEG_ALIGN), b)]
                pltpu.make_async_copy(src, dst, sem.at[buf]).start()

            _segment_copies(ell_ref[tile, e], make)
            return carry

        lax.fori_loop(0, N_EXPERTS, body, 0)

    def seg_wait(tile, buf):
        def make(hi, b):
            pltpu.make_async_copy(yg_hbm.at[pl.ds(0, b)], ys_scr.at[buf, pl.ds(0, b)], sem.at[buf]).wait()

        _chunk_copies(off_ref[tile, N_EXPERTS - 1] + ell_ref[tile, N_EXPERTS - 1], _seg_bits(cap), make)

    tile = tile0 + s

    @pl.when(s == 0)
    def _():
        ys_scr[...] = jnp.zeros_like(ys_scr)
        seg_start(tile, slot)

    @pl.when(s + 1 < pl.num_programs(0))
    def _():
        seg_start(tile + 1, 1 - slot)

    seg_wait(tile, slot)
    slot0, slot1 = _pair_slots_cols(ids_ref[...], offrow_ref[...], t)
    cols = lax.broadcasted_iota(I32, (t, cap), 1).astype(F32)
    pu = jnp.where((cols == slot0) | (cols == slot1), 1.0, 0.0).astype(BF16)
    o_ref[...] = h_ref[...] + _dot(pu, ys_scr[slot])


def _moe_unpermute(meta, h, ids, yg, t, tile0):
    n, d = h.shape
    cap = _slot_cap(t)
    prefetch = [meta["off"], meta["ell"], meta["gseg"]]
    imap = lambda s, *_: (s, 0)
    grid_spec = pltpu.PrefetchScalarGridSpec(
        num_scalar_prefetch=len(prefetch),
        grid=(n // t,),
        in_specs=[pl.BlockSpec((t, LANES), imap),
                  pl.BlockSpec((None, 1, LANES), lambda s, *_: (tile0 + s, 0, 0)),
                  pl.BlockSpec((t, d), imap), pl.BlockSpec(memory_space=pl.ANY)],
        out_specs=pl.BlockSpec((t, d), imap),
        scratch_shapes=[pltpu.VMEM((2, cap, d), BF16), pltpu.SemaphoreType.DMA((2,))],
    )
    return pl.pallas_call(
        functools.partial(_unpermute_kernel, tile0=tile0),
        grid_spec=grid_spec,
        out_shape=jax.ShapeDtypeStruct((n, d), F32),
        compiler_params=_params("arbitrary"),
        name="moe_unpermute",
    )(*prefetch, ids, meta["off_row"], h, yg)


def _moe(groups, wg, wu, wd, layer):
    tiles = [t for g in groups for t in [g[4]] * (g[0].shape[0] // g[4])]
    meta = _moe_meta(jnp.concatenate([g[3].reshape(-1, LANES) for g in groups], axis=0), tiles)
    xg = None
    tile0 = 0
    for h1, xe, ids, _, t in groups:
        xg = _moe_permute(meta, xe, ids, t, tile0, xg)
        tile0 += h1.shape[0] // t
    yg = _moe_ffn(xg, meta["step_expert"], meta["n_used"], wg, wu, wd, FFN_ROWS, layer)
    outs = []
    tile0 = 0
    for h1, xe, ids, _, t in groups:
        outs.append(_moe_unpermute(meta, h1, ids, yg, t, tile0))
        tile0 += h1.shape[0] // t
    return outs


def _ple_kernel(*refs, tail):
    h_ref, p_ref, proj_ref, pn_ref, gate_ref = refs[:5]
    h = h_ref[...]
    pp = _dot(p_ref[...].astype(BF16), proj_ref[...])
    g = _dot(_rms(h, pn_ref[...]).astype(BF16), gate_ref[...])
    h3 = h + pp * _sigmoid(g)
    if tail == "conv_in":
        nm_ref, pw1_ref, pb1_ref, h3_ref, a_ref = refs[5:]
        h3_ref[...] = h3
        d = h3.shape[1]
        hn = _rms(h3, nm_ref[...]).astype(BF16)
        ua = _dot(hn, pw1_ref[:, :d]) + pb1_ref[:, :d]
        ub = _dot(hn, pw1_ref[:, d:]) + pb1_ref[:, d:]
        a_ref[...] = ua * _sigmoid(ub)
    else:
        nf_ref, y_ref = refs[5:]
        y_ref[...] = _rms(h3, nf_ref[...])


def _ple(h, p, layer, proj, pn, gate, tm, tail, extra):
    n, d = h.shape
    row = lambda i: (i, 0)
    const = lambda i: (0, 0)
    ins = [h, p, proj, pn, gate] + list(extra)
    specs = [pl.BlockSpec((tm, d), row), pl.BlockSpec((None, tm, p.shape[2]), lambda i: (layer, i, 0)),
             pl.BlockSpec(proj.shape, const), pl.BlockSpec((1, d), const),
             pl.BlockSpec(gate.shape, const)] + [pl.BlockSpec(e.shape, const) for e in extra]
    n_out = 2 if tail == "conv_in" else 1
    return pl.pallas_call(
        functools.partial(_ple_kernel, tail=tail),
        grid=(n // tm,),
        in_specs=specs,
        out_specs=[pl.BlockSpec((tm, d), row)] * n_out,
        out_shape=[jax.ShapeDtypeStruct((n, d), F32)] * n_out,
        compiler_params=_params("parallel"),
        name="ple_" + tail,
    )(*ins)


def _ln_swish(y, w, b):
    mu = jnp.mean(y, axis=-1, keepdims=True)
    var = jnp.mean(jnp.square(y - mu), axis=-1, keepdims=True)
    z = (y - mu) * lax.rsqrt(var + EPS) * w + b
    return z * _sigmoid(z)


def _conv_kernel(a_ref, dw_ref, dwb_ref, lnw_ref, lnb_ref, o_ref, ext, shifted, acc):
    t = pl.program_id(1)
    tl = a_ref.shape[0]
    halo = ext.shape[0] - tl
    sub = 8

    @pl.when(t == 0)
    def _():
        ext[0:halo, :] = jnp.zeros((halo, ext.shape[1]), F32)

    ext[halo:, :] = a_ref[...]
    first = halo - (CONV_WIDTH - 1)
    srows = shifted.shape[1]
    for c0 in range(0, ext.shape[1], LANES):
        cs = slice(c0, c0 + LANES)
        for r in range(1, sub):
            shifted[r - 1] = ext[r:r + srows, cs]
        s = None
        for j in range(CONV_WIDTH):
            r = (first + j) % sub
            base = first + j - r
            win = ext[base:base + tl, cs] if r == 0 else shifted[r - 1, base:base + tl, :]
            term = win * dw_ref[j:j + 1, cs]
            s = term if s is None else s + term
        acc[:, cs] = s
    ext[0:halo, :] = ext[tl:tl + halo, :]
    o_ref[...] = _ln_swish(acc[...] + dwb_ref[...], lnw_ref[...], lnb_ref[...]).astype(BF16)


def _conv_prompt(a, dw, dwb, lnw, lnb, b, l, tl=256):
    d = a.shape[1]
    nt = l // tl
    halo = 32
    row = lambda bi, t: (bi * nt + t, 0)
    const = lambda bi, t: (0, 0)
    return pl.pallas_call(
        _conv_kernel,
        grid=(b, nt),
        in_specs=[pl.BlockSpec((tl, d), row), pl.BlockSpec(dw.shape, const),
                  pl.BlockSpec((1, d), const), pl.BlockSpec((1, d), const), pl.BlockSpec((1, d), const)],
        out_specs=pl.BlockSpec((tl, d), row),
        out_shape=jax.ShapeDtypeStruct((b * l, d), BF16),
        scratch_shapes=[pltpu.VMEM((tl + halo, d), F32), pltpu.VMEM((7, tl + halo - 8, LANES), F32),
                        pltpu.VMEM((tl, d), F32)],
        compiler_params=_params("parallel", "arbitrary"),
        name="conv_prompt",
    )(a, dw, dwb, lnw, lnb)


def _conv_decode_kernel(buf_ref, a_ref, dw_ref, dwb_ref, lnw_ref, lnb_ref, y_ref, new_ref):
    w = CONV_WIDTH - 1
    a = a_ref[...]
    y = a * dw_ref[w:w + 1, :]
    for j in range(w):
        y = y + buf_ref[j] * dw_ref[j:j + 1, :]
    y_ref[...] = _ln_swish(y + dwb_ref[...], lnw_ref[...], lnb_ref[...])
    for j in range(w - 1):
        new_ref[j] = buf_ref[j + 1]
    new_ref[w - 1] = a


def _conv_decode(state, a, dw, dwb, lnw, lnb, bb=8):
    _, b, w, d = state.shape
    row = pl.BlockSpec((bb, d), lambda i: (i, 0))
    sspec = pl.BlockSpec((None, w, bb, d), lambda i: (0, 0, i, 0))
    const = lambda i: (0, 0)
    y, new = pl.pallas_call(
        _conv_decode_kernel,
        grid=(b // bb,),
        in_specs=[sspec, row, pl.BlockSpec(dw.shape, const), pl.BlockSpec((1, d), const),
                  pl.BlockSpec((1, d), const), pl.BlockSpec((1, d), const)],
        out_specs=[row, sspec],
        out_shape=[jax.ShapeDtypeStruct((b, d), F32), jax.ShapeDtypeStruct((1, w, b, d), F32)],
        compiler_params=_params("parallel"),
        name="conv_decode",
    )(jnp.transpose(state, (0, 2, 1, 3)), a, dw, dwb, lnw, lnb)
    return y, jnp.transpose(new, (0, 2, 1, 3))


def _prep_weights(W):
    P = {}
    w_in = W["w_in"][0]
    kpe_cols = w_in[:, -MLA_ROPE:]
    P["w_in"] = jnp.concatenate([w_in[:, :-MLA_ROPE]] + [kpe_cols] * (LANES // MLA_ROPE), axis=1).astype(BF16)
    w_uq = W["w_uq"][0]
    nope = jnp.pad(w_uq[..., :MLA_NOPE], ((0, 0), (0, 0), (0, LANES - MLA_NOPE)))
    P["w_uq_nope"] = nope.reshape(Q_LORA, MLA_HEADS * LANES).astype(BF16)
    P["w_uq_pe"] = w_uq[..., MLA_NOPE:].reshape(Q_LORA, MLA_HEADS * MLA_ROPE).astype(BF16)
    w_ukv = W["w_ukv"][0]
    wk = jnp.transpose(w_ukv[..., :MLA_NOPE], (1, 2, 0))
    P["wk"] = jnp.pad(wk, ((0, 0), (0, LANES - MLA_NOPE), (0, 0))).astype(BF16)
    wv = jnp.transpose(w_ukv[..., MLA_NOPE:], (1, 0, 2))
    P["wv"] = wv.astype(BF16)
    eye = jnp.eye(MLA_HEADS, dtype=F32)
    P["wv_bd"] = (wv[:, :, None, :] * eye[:, None, :, None]).reshape(
        MLA_HEADS * KV_LORA, MLA_HEADS * MLA_V).astype(BF16)
    w_out = W["w_out"][0].astype(BF16)
    P["w_out_r"] = w_out[:RET_HEADS * RET_DV]
    P["w_out_m"] = w_out[RET_HEADS * RET_DV:]
    d = w_out.shape[1]
    for i in range(W["moe_w_group"].shape[0]):
        rw = jnp.concatenate([W["moe_w_group"][i], W["moe_w_router"][i].reshape(d, N_EXPERTS)], axis=1)
        rb = jnp.concatenate([W["moe_b_group"][i], W["moe_b_router"][i].reshape(N_EXPERTS)])
        padc = LANES - rw.shape[1]
        rw = jnp.pad(rw, ((0, 0), (0, padc)))
        rw_hi = rw.astype(BF16)
        P["router_w", i] = (rw_hi, (rw - rw_hi.astype(F32)).astype(BF16))
        P["router_b", i] = jnp.pad(rb, (0, padc)).reshape(1, LANES)
        P["ple_proj", i] = W["ple_proj"][i].astype(BF16)
        P["ple_gate", i] = W["ple_gate"][i].astype(BF16)
    P["pw1"] = W["conv_pw1"][0].astype(BF16)
    P["pw2"] = W["conv_pw2"][0].astype(BF16)
    P["dw"] = jnp.pad(W["conv_dw"][0], ((0, 1), (0, 0)))
    return P


def _vec(v):
    return v.reshape(1, -1)


def _mixer0(x, pos, W, P, tm, past):
    b, l, d = x.shape
    n = b * l
    h = x.reshape(n, d)
    tabs = _rope_tables(pos, RET_DK) + _rope_tables(pos, MLA_ROPE)
    rq, rk, rv, rg, qlat, qpe, c, kpe, kcat = _proj_in(
        h, _vec(W["norm_mix"][0]), P["w_in"], tabs, _vec(W["mla_q_norm"][0]), _vec(W["mla_kv_norm"][0]),
        P["w_uq_nope"], P["w_uq_pe"], P["wk"], tm)
    gnw = _vec(W["ret_gn_w"][0])
    if past is None:
        o_r, s_new = _retention_prompt(rq, rk, rv, rg, gnw, b, l)
        o_m = _attention_prompt(qlat, qpe, kcat, P["wv"], b, l)
        pre = None
    else:
        ret_state, lat_cache, rope_cache, page_table = past
        o_r, s_new = _retention_decode(rq, rk, rv, rg, gnw, ret_state)
        o_m = _attention_decode(qlat, qpe, c, kpe, lat_cache, rope_cache, page_table)
        o_m = o_m.reshape(n, MLA_HEADS * KV_LORA)
        pre = P["wv_bd"]
    routed = _mix_out([o_r, o_m], [P["w_out_r"], P["w_out_m"]], h, _vec(W["norm_ffn"][0]),
                      P["router_w", 0], P["router_b", 0], tm, pre=pre)
    return routed, (c.reshape(1, b, l, KV_LORA), kpe.reshape(1, b, l, MLA_ROPE), s_new)


def _mixer1(h2, p, W, P, tm, b, l, conv_state):
    n, d = h2.shape
    h3, a = _ple(h2, p.reshape(p.shape[0], n, -1), 0, P["ple_proj", 0], _vec(W["ple_norm"][0]), P["ple_gate", 0],
                 tm, "conv_in", [_vec(W["norm_mix"][1]), P["pw1"], _vec(W["conv_pw1_b"][0])])
    conv_vecs = (_vec(W["conv_dw_b"][0]), _vec(W["conv_ln_w"][0]), _vec(W["conv_ln_b"][0]))
    if conv_state is None:
        yact = _conv_prompt(a, P["dw"], *conv_vecs, b, l)
        new_state = a.reshape(b, l, d)[:, l - (CONV_WIDTH - 1):, :][None]
    else:
        yact, new_state = _conv_decode(conv_state, a, P["dw"], *conv_vecs)
    routed = _mix_out([yact], [P["pw2"]], h3, _vec(W["norm_ffn"][1]),
                      P["router_w", 1], P["router_b", 1], tm, bias=_vec(W["conv_pw2_b"][0]))
    return routed, new_state


def _final(h5, p, W, P, tm, shape):
    n = h5.shape[0]
    (y,) = _ple(h5, p.reshape(p.shape[0], n, -1), 1, P["ple_proj", 1], _vec(W["ple_norm"][1]), P["ple_gate", 1],
                tm, "final", [_vec(W["norm_final"])])
    return y.reshape(shape)


def kernel(x_prompt, x_sample, p_prompt, p_sample, cache_mla_latent, cache_mla_rope, state_retention, state_conv,
           page_table, norm_mix, norm_ffn, norm_final, w_in, ret_gn_w, mla_q_norm, mla_kv_norm, w_uq, w_ukv, w_out,
           conv_pw1, conv_pw1_b, conv_dw, conv_dw_b, conv_ln_w, conv_ln_b, conv_pw2, conv_pw2_b,
           moe_w_group, moe_b_group, moe_w_router, moe_b_router, moe_w_gate, moe_w_up, moe_w_down,
           ple_proj, ple_norm, ple_gate):
    W = dict(norm_mix=norm_mix, norm_ffn=norm_ffn, norm_final=norm_final, w_in=w_in, ret_gn_w=ret_gn_w,
             mla_q_norm=mla_q_norm, mla_kv_norm=mla_kv_norm, w_uq=w_uq, w_ukv=w_ukv, w_out=w_out,
             conv_pw1=conv_pw1, conv_pw1_b=conv_pw1_b, conv_dw=conv_dw, conv_dw_b=conv_dw_b, conv_ln_w=conv_ln_w,
             conv_ln_b=conv_ln_b, conv_pw2=conv_pw2, conv_pw2_b=conv_pw2_b, moe_w_group=moe_w_group,
             moe_b_group=moe_b_group, moe_w_router=moe_w_router, moe_b_router=moe_b_router, moe_w_gate=moe_w_gate,
             moe_w_up=moe_w_up, moe_w_down=moe_w_down, ple_proj=ple_proj, ple_norm=ple_norm, ple_gate=ple_gate)
    assert w_in.shape[0] == 1 and conv_pw1.shape[0] == 1, "one retention/attention layer, one conv layer"
    P = _prep_weights(W)
    past_len = page_table.shape[1] * cache_mla_latent.shape[2]
    pos_prompt = jnp.arange(x_prompt.shape[1])
    pos_sample = past_len + jnp.arange(x_sample.shape[1])
    bp, lp, _ = x_prompt.shape
    bs, ls, _ = x_sample.shape
    tm_p = min(512, lp)
    tm_s = bs * ls
    moe_w = lambda i: (moe_w_gate, moe_w_up, moe_w_down, i)
    routed_p, (lat_p, rope_p, ret_p) = _mixer0(x_prompt, pos_prompt, W, P, tm_p, None)
    routed_s, (lat_s, rope_s, ret_s) = _mixer0(
        x_sample, pos_sample, W, P, tm_s, (state_retention, cache_mla_latent, cache_mla_rope, page_table))
    h2_p, h2_s = _moe([routed_p + (tm_p,), routed_s + (tm_s,)], *moe_w(0))
    routed_p, conv_p = _mixer1(h2_p, p_prompt, W, P, tm_p, bp, lp, None)
    routed_s, conv_s = _mixer1(h2_s, p_sample, W, P, tm_s, bs, ls, state_conv)
    h5_p, h5_s = _moe([routed_p + (tm_p,), routed_s + (tm_s,)], *moe_w(1))
    y_p = _final(h5_p, p_prompt, W, P, tm_p, x_prompt.shape)
    y_s = _final(h5_s, p_sample, W, P, tm_s, x_sample.shape)
    return (y_p, y_s, lat_p, rope_p, ret_p, conv_p, lat_s, rope_s, ret_s, conv_s)
```

```python
import functools

import jax
import jax.numpy as jnp
from jax import lax
from jax.experimental import pallas as pl
from jax.experimental.pallas import tpu as pltpu

F32 = jnp.float32
BF16 = jnp.bfloat16
I32 = jnp.int32

EPS = 1e-6
ROPE_THETA = 10000.0
RET_HEADS = 8
RET_DK = 64
RET_DV = 64
RET_CHUNK = 128
MLA_HEADS = 8
MLA_NOPE = 64
MLA_ROPE = 32
MLA_V = 64
Q_LORA = 256
KV_LORA = 128
CONV_WIDTH = 31
N_GROUPS = 4
EXPERTS_PER_GROUP = 8
N_EXPERTS = N_GROUPS * EXPERTS_PER_GROUP
LANES = 128
Q_BLOCK = 256
KV_BLOCK = 256
VMEM_LIMIT_BYTES = 56 * 1024 * 1024


def _params(*sem):
    return pltpu.CompilerParams(dimension_semantics=sem, vmem_limit_bytes=VMEM_LIMIT_BYTES)


def _rms(x, w):
    return x * lax.rsqrt(jnp.mean(x * x, axis=-1, keepdims=True) + EPS) * w


def _sigmoid(x):
    return 1.0 / (1.0 + jnp.exp(-x))


def _dot(a, b):
    return jnp.dot(a, b, preferred_element_type=F32)


def _dot_nt(a, b):
    return lax.dot_general(a, b, (((1,), (1,)), ((), ())), preferred_element_type=F32)


def _dot_tn(a, b):
    return lax.dot_general(a, b, (((0,), (0,)), ((), ())), preferred_element_type=F32)


def _rope(x, cos_t, sin_t, period):
    half = period // 2
    lane = lax.broadcasted_iota(I32, x.shape, 1)
    first = (lane % period) < half
    rot = jnp.where(first, pltpu.roll(x, LANES - half, 1), pltpu.roll(x, half, 1))
    return x * cos_t + rot * sin_t


def _rope_tables(pos, period):
    half = period // 2
    inv = ROPE_THETA ** (-2.0 * jnp.arange(half, dtype=F32) / period)
    ang = pos.astype(F32)[:, None] * inv[None, :]
    cos = jnp.cos(ang)
    sin = jnp.sin(ang)
    cos_p = jnp.concatenate([cos, cos], axis=1)
    sin_p = jnp.concatenate([-sin, sin], axis=1)
    reps = LANES // period
    return jnp.tile(cos_p, (1, reps)), jnp.tile(sin_p, (1, reps))


def _proj_in_kernel(x_ref, nw_ref, win_ref, cr_ref, sr_ref, cp_ref, sp_ref, qn_ref, kvn_ref,
                    wuqn_ref, wuqp_ref, wk_ref,
                    rq_ref, rk_ref, rv_ref, rg_ref, qlat_ref, qpe_ref, c_ref, kpe_ref, kcat_ref):
    hn = _rms(x_ref[...], nw_ref[...]).astype(BF16)
    cr, sr, cp, sp = cr_ref[...], sr_ref[...], cp_ref[...], sp_ref[...]
    hk = RET_HEADS * RET_DK
    hv = RET_HEADS * RET_DV

    def mm(lo, width):
        return _dot(hn, win_ref[:, lo:lo + width])

    u = mm(0, hk)
    rq_ref[...] = jnp.concatenate(
        [_rope(u[:, g:g + LANES], cr, sr, RET_DK) for g in range(0, hk, LANES)], axis=1).astype(BF16)
    u = mm(hk, hk)
    rk_ref[...] = jnp.concatenate(
        [_rope(u[:, g:g + LANES], cr, sr, RET_DK) * (RET_DK ** -0.5) for g in range(0, hk, LANES)],
        axis=1).astype(BF16)
    rv_ref[...] = mm(2 * hk, hv).astype(BF16)
    rg_ref[...] = mm(2 * hk + hv, hv)
    off = 2 * hk + 2 * hv
    cqn = _rms(mm(off, Q_LORA), qn_ref[...]).astype(BF16)
    c = _rms(mm(off + Q_LORA, KV_LORA), kvn_ref[...])
    c_ref[...] = c
    kp = _rope(mm(off + Q_LORA + KV_LORA, LANES), cp, sp, MLA_ROPE)
    kpe_ref[...] = kp[:, :MLA_ROPE]
    kcat_ref[...] = jnp.concatenate([c, kp], axis=1).astype(BF16)
    scale = (MLA_NOPE + MLA_ROPE) ** -0.5
    qn = _dot(cqn, wuqn_ref[...])
    for h in range(MLA_HEADS):
        ql = _dot(qn[:, h * LANES:(h + 1) * LANES].astype(BF16), wk_ref[h]) * scale
        qlat_ref[:, h * KV_LORA:(h + 1) * KV_LORA] = ql.astype(BF16)
    qp = _dot(cqn, wuqp_ref[...])
    qpe_ref[...] = jnp.concatenate(
        [_rope(qp[:, g:g + LANES], cp, sp, MLA_ROPE) * scale
         for g in range(0, MLA_HEADS * MLA_ROPE, LANES)], axis=1).astype(BF16)


def _proj_in(x2, nw, win, tabs, qn, kvn, wuqn, wuqp, wk, tm):
    n, d = x2.shape
    cr, sr, cp, sp = tabs
    ltab = cr.shape[0]
    nblk = ltab // tm if ltab >= tm else 0
    tab_rows = tm if nblk else ltab
    tab_map = (lambda i: (i % nblk, 0)) if nblk else (lambda i: (0, 0))
    row = lambda i: (i, 0)
    const = lambda i: (0, 0)
    hk = RET_HEADS * RET_DK
    hv = RET_HEADS * RET_DV
    tab_spec = pl.BlockSpec((tab_rows, LANES), tab_map)
    out_cols = [(hk, BF16), (hk, BF16), (hv, BF16), (hv, F32), (MLA_HEADS * KV_LORA, BF16),
                (MLA_HEADS * MLA_ROPE, BF16), (KV_LORA, F32), (MLA_ROPE, F32), (2 * LANES, BF16)]
    return pl.pallas_call(
        _proj_in_kernel,
        grid=(n // tm,),
        in_specs=[pl.BlockSpec((tm, d), row), pl.BlockSpec((1, d), const),
                  pl.BlockSpec(win.shape, const), tab_spec, tab_spec, tab_spec, tab_spec,
                  pl.BlockSpec((1, Q_LORA), const), pl.BlockSpec((1, KV_LORA), const),
                  pl.BlockSpec(wuqn.shape, const), pl.BlockSpec(wuqp.shape, const),
                  pl.BlockSpec(wk.shape, lambda i: (0, 0, 0))],
        out_specs=[pl.BlockSpec((tm, w), row) for w, _ in out_cols],
        out_shape=[jax.ShapeDtypeStruct((n, w), dt) for w, dt in out_cols],
        compiler_params=_params("parallel"),
        name="proj_in",
    )(x2, nw, win, cr, sr, cp, sp, qn, kvn, wuqn, wuqp, wk)


def _ret_kernel(rq_ref, rk_ref, rv_ref, rg_ref, gnw_ref, decay_ref, qdec_ref, kdec_ref, cdec_ref,
                o_ref, s_out_ref, s_scr):
    ci = pl.program_id(1)

    @pl.when(ci == 0)
    def _():
        s_scr[...] = jnp.zeros_like(s_scr)

    c = rq_ref.shape[0]
    lo = lax.broadcasted_iota(I32, (c, LANES), 1) < RET_DK
    rr = lax.broadcasted_iota(I32, (LANES, LANES), 0) < RET_DK
    cc = lax.broadcasted_iota(I32, (LANES, LANES), 1) < RET_DK
    diag_blocks = rr == cc
    outs = []
    for g in range(RET_HEADS // 2):
        gl = slice(g * LANES, (g + 1) * LANES)
        q2 = rq_ref[:, gl]
        k2 = rk_ref[:, gl]
        v2 = rv_ref[:, gl]
        zero = jnp.zeros_like(q2)
        sc = _dot_nt(jnp.concatenate([jnp.where(lo, q2, zero), jnp.where(lo, zero, q2)], axis=0), k2)
        pv_a = _dot((sc[:c] * decay_ref[2 * g]).astype(BF16), v2)
        pv_b = _dot((sc[c:] * decay_ref[2 * g + 1]).astype(BF16), v2)
        s2 = s_scr[g]
        o = jnp.where(lo, pv_a, pv_b) + _dot(q2, s2.astype(BF16)) * qdec_ref[:, gl]
        kd = (k2.astype(F32) * kdec_ref[:, gl]).astype(BF16)
        s_scr[g] = s2 * cdec_ref[:, gl] + jnp.where(diag_blocks, _dot_tn(kd, v2), 0.0)

        def head_mean(x):
            a = jnp.sum(jnp.where(lo, x, 0.0), axis=-1, keepdims=True)
            b = jnp.sum(jnp.where(lo, 0.0, x), axis=-1, keepdims=True)
            return jnp.where(lo, a, b) * (1.0 / RET_DV)

        mu = head_mean(o)
        var = head_mean(jnp.square(o - mu))
        outs.append((o - mu) * lax.rsqrt(var + EPS))
    gate = rg_ref[...]
    o_ref[...] = (jnp.concatenate(outs, axis=1) * gnw_ref[...] * (gate * _sigmoid(gate))).astype(BF16)

    @pl.when(ci == pl.num_programs(1) - 1)
    def _():
        for g in range(RET_HEADS // 2):
            s2 = s_scr[g]
            s_out_ref[2 * g] = s2[:RET_DK, :RET_DV]
            s_out_ref[2 * g + 1] = s2[RET_DK:, RET_DV:]


def _ret_tables(chunk):
    h = RET_HEADS
    log_g = jnp.log1p(-jnp.exp2(-5.0 - jnp.arange(h, dtype=F32)))
    idx = jnp.arange(chunk, dtype=F32)
    rel = idx[:, None] - idx[None, :]
    decay = jnp.where(rel[None] >= 0, jnp.exp(jnp.maximum(rel, 0.0)[None] * log_g[:, None, None]), 0.0)
    q_dec = jnp.exp((idx[:, None] + 1.0) * log_g[None, :])
    k_dec = jnp.exp((chunk - 1.0 - idx)[:, None] * log_g[None, :])
    c_dec = jnp.exp(chunk * log_g)
    rep = lambda t: jnp.repeat(t, RET_DK, axis=-1)
    return decay, rep(q_dec), rep(k_dec), rep(c_dec[None, :])


def _retention_prompt(rq, rk, rv, rg, gnw, b, l):
    chunk = next((c for c in (2 * RET_CHUNK, RET_CHUNK) if l % c == 0), l)
    decay, qdec, kdec, cdec = _ret_tables(chunk)
    hk = RET_HEADS * RET_DK
    n_chunks = l // chunk
    row = lambda bi, ci: (bi * n_chunks + ci, 0)
    const = lambda bi, ci: (0, 0)
    return pl.pallas_call(
        _ret_kernel,
        grid=(b, n_chunks),
        in_specs=[pl.BlockSpec((chunk, hk), row)] * 4 + [
            pl.BlockSpec((1, hk), const),
            pl.BlockSpec(decay.shape, lambda bi, ci: (0, 0, 0)),
            pl.BlockSpec((chunk, hk), const), pl.BlockSpec((chunk, hk), const),
            pl.BlockSpec((1, hk), const)],
        out_specs=[pl.BlockSpec((chunk, hk), row),
                   pl.BlockSpec((None, None, RET_HEADS, RET_DK, RET_DV), lambda bi, ci: (0, bi, 0, 0, 0))],
        out_shape=[jax.ShapeDtypeStruct((b * l, hk), BF16),
                   jax.ShapeDtypeStruct((1, b, RET_HEADS, RET_DK, RET_DV), F32)],
        scratch_shapes=[pltpu.VMEM((RET_HEADS // 2, 2 * RET_DK, 2 * RET_DV), F32)],
        compiler_params=_params("parallel", "arbitrary"),
        name="retention_prompt",
    )(rq, rk, rv, rg, gnw, decay, qdec, kdec, cdec)


def _ret_decode_kernel(q_ref, k_ref, v_ref, g_ref, gnw_ref, gam_ref, s_ref, o_ref, s_out_ref):
    gam = gam_ref[...]
    q = q_ref[...]
    k = k_ref[...]
    v = v_ref[...]
    qg = q * gam
    cross = jnp.zeros_like(v)
    for d in range(q.shape[0]):
        s_d = s_ref[d]
        cross = cross + qg[d:d + 1, :] * s_d
        s_out_ref[d] = s_d * gam + k[d:d + 1, :] * v
    o = jnp.sum(q * k, axis=0, keepdims=True) * v + cross
    mu = jnp.mean(o, axis=0, keepdims=True)
    var = jnp.mean(jnp.square(o - mu), axis=0, keepdims=True)
    g = g_ref[...]
    o_ref[...] = (o - mu) * lax.rsqrt(var + EPS) * gnw_ref[...] * (g * _sigmoid(g))


def _retention_decode(rq, rk, rv, rg, gnw, state):
    b = rq.shape[0]
    h, dk, dv = RET_HEADS, RET_DK, RET_DV
    gam = jnp.exp(jnp.log1p(-jnp.exp2(-5.0 - jnp.arange(h, dtype=F32)))).reshape(h, 1, 1)
    tr = lambda t: t.astype(F32).T
    vec = lambda d: pl.BlockSpec((d, b), lambda i: (i, 0))
    sspec = pl.BlockSpec((None, None, dk, dv, b), lambda i: (0, i, 0, 0, 0))
    o, s_new = pl.pallas_call(
        _ret_decode_kernel,
        grid=(h,),
        in_specs=[vec(dk), vec(dk), vec(dv), vec(dv), pl.BlockSpec((dv, 1), lambda i: (i, 0)),
                  pl.BlockSpec((None, 1, 1), lambda i: (i, 0, 0)), sspec],
        out_specs=[vec(dv), sspec],
        out_shape=[jax.ShapeDtypeStruct((h * dv, b), F32),
                   jax.ShapeDtypeStruct((1, h, dk, dv, b), F32)],
        compiler_params=_params("parallel"),
        name="retention_decode",
    )(tr(rq), tr(rk), tr(rv), tr(rg), gnw.reshape(h * dv, 1), gam, jnp.transpose(state, (0, 2, 3, 4, 1)))
    return o.T, jnp.transpose(s_new, (0, 4, 1, 2, 3))


def _attn_kernel(qlat_ref, qpe_ref, kcat_ref, wv_ref, o_ref, q_scr, m_scr, acc_scr, s_scr, p_scr, a_scr):
    i = pl.program_id(1)
    lane = lax.broadcasted_iota(I32, (Q_BLOCK, LANES), 1)
    per_group = LANES // MLA_ROPE
    parts = []
    for h in range(MLA_HEADS):
        g, r = divmod(h, per_group)
        pe = qpe_ref[:, g * LANES:(g + 1) * LANES]
        keep = (lane >= r * MLA_ROPE) & (lane < (r + 1) * MLA_ROPE)
        pe = jnp.where(keep, pe, jnp.zeros_like(pe))
        parts.append(jnp.concatenate([qlat_ref[:, h * KV_LORA:(h + 1) * KV_LORA], pe], axis=1))
    q_scr[...] = jnp.concatenate(parts, axis=0)
    rows = MLA_HEADS * Q_BLOCK
    m_scr[...] = jnp.full_like(m_scr, -jnp.inf)
    acc_scr[...] = jnp.zeros_like(acc_scr)
    ones = jnp.ones((KV_BLOCK, LANES), BF16)

    def keys(j):
        return kcat_ref[pl.ds(pl.multiple_of(j * KV_BLOCK, KV_BLOCK), KV_BLOCK), :]

    def scores(j, par):
        s_scr[par] = _dot_nt(q_scr[...], keys(j))

    def softmax(j, par, masked):
        s = s_scr[par]
        if masked:
            qpos = i * Q_BLOCK + lax.broadcasted_iota(I32, (rows, KV_BLOCK), 0) % Q_BLOCK
            kpos = j * KV_BLOCK + lax.broadcasted_iota(I32, (rows, KV_BLOCK), 1)
            s = jnp.where(kpos <= qpos, s, -jnp.inf)
        s0, s1 = s[:, :LANES], s[:, LANES:]
        m_old = m_scr[...]
        m_new = jnp.maximum(m_old, jnp.max(jnp.maximum(s0, s1), axis=-1, keepdims=True))
        a_scr[par] = jnp.exp(m_old - m_new)
        p_scr[par] = jnp.concatenate([jnp.exp(s0 - m_new), jnp.exp(s1 - m_new)], axis=1).astype(BF16)
        m_scr[...] = m_new

    def accumulate(j, par):
        alpha = a_scr[par]
        vext = jnp.concatenate([keys(j)[:, :KV_LORA], ones], axis=1)
        acc_scr[...] = jnp.concatenate([alpha, alpha], axis=1) * acc_scr[...] + _dot(p_scr[par], vext)

    def step(t, par):
        scores(t + 1, 1 - par)
        softmax(t, par, False)
        accumulate(t - 1, 1 - par)

    n_full = (i * Q_BLOCK) // KV_BLOCK
    scores(0, 0)

    @pl.when(n_full >= 1)
    def _():
        scores(1, 1)
        softmax(0, 0, False)

    def body(u, carry):
        step(2 * u + 1, 1)
        step(2 * u + 2, 0)
        return carry

    n_pairs = jnp.maximum(n_full - 1, 0) // 2
    lax.fori_loop(0, n_pairs, body, 0)
    even = n_full % 2 == 0

    @pl.when(even & (n_full >= 2))
    def _():
        step(n_full - 1, 1)

    @pl.when(even)
    def _():
        softmax(n_full, 0, True)

        @pl.when(n_full >= 1)
        def _():
            accumulate(n_full - 1, 1)

        accumulate(n_full, 0)

    @pl.when(jnp.logical_not(even))
    def _():
        softmax(n_full, 1, True)
        accumulate(n_full - 1, 0)
        accumulate(n_full, 1)

    o = acc_scr[:, :KV_LORA] / acc_scr[:, KV_LORA:]
    outs = [_dot(o[h * Q_BLOCK:(h + 1) * Q_BLOCK].astype(BF16), wv_ref[h]) for h in range(MLA_HEADS)]
    o_ref[...] = jnp.concatenate(outs, axis=1).astype(BF16)


def _attention_prompt(qlat, qpe, kcat, wv, b, l):
    nq = l // Q_BLOCK
    rows = MLA_HEADS * Q_BLOCK
    qmap = lambda bi, i: (bi * nq + i, 0)
    return pl.pallas_call(
        _attn_kernel,
        grid=(b, nq),
        in_specs=[pl.BlockSpec((Q_BLOCK, qlat.shape[1]), qmap),
                  pl.BlockSpec((Q_BLOCK, qpe.shape[1]), qmap),
                  pl.BlockSpec((l, kcat.shape[1]), lambda bi, i: (bi, 0)),
                  pl.BlockSpec(wv.shape, lambda bi, i: (0, 0, 0))],
        out_specs=pl.BlockSpec((Q_BLOCK, MLA_HEADS * MLA_V), qmap),
        out_shape=jax.ShapeDtypeStruct((b * l, MLA_HEADS * MLA_V), BF16),
        scratch_shapes=[pltpu.VMEM((rows, 2 * LANES), BF16), pltpu.VMEM((rows, LANES), F32),
                        pltpu.VMEM((rows, 2 * KV_LORA), F32), pltpu.VMEM((2, rows, KV_BLOCK), F32),
                        pltpu.VMEM((2, rows, KV_BLOCK), BF16), pltpu.VMEM((2, rows, LANES), F32)],
        compiler_params=_params("parallel", "arbitrary"),
        name="attention_prompt",
    )(qlat, qpe, kcat, wv)


def _attn_decode_kernel(pt_ref, ql_ref, qp_ref, cn_ref, kn_ref, lat_hbm, rope_hbm, o_ref,
                        kbuf, rbuf, sem):
    b = pl.program_id(0)
    nb = pl.num_programs(0)
    n_pages = pt_ref.shape[1]
    page = lat_hbm.shape[2]

    def copies(seq, slot, p):
        phys = pt_ref[seq, p]
        dst = pl.ds(p * page, page)
        return (pltpu.make_async_copy(lat_hbm.at[0, phys], kbuf.at[slot, dst], sem.at[slot, 0]),
                pltpu.make_async_copy(rope_hbm.at[0, phys], rbuf.at[slot, :, dst], sem.at[slot, 1]))

    def start_all(seq, slot):
        def body(p, carry):
            for cp in copies(seq, slot, p):
                cp.start()
            return carry
        lax.fori_loop(0, n_pages, body, 0, unroll=4)

    def wait_all(slot):
        pltpu.make_async_copy(kbuf.at[slot], kbuf.at[slot], sem.at[slot, 0]).wait()
        pltpu.make_async_copy(rbuf.at[slot], rbuf.at[slot], sem.at[slot, 1]).wait()

    slot = b % 2

    @pl.when(b == 0)
    def _():
        start_all(0, 0)

    @pl.when(b + 1 < nb)
    def _():
        start_all(b + 1, 1 - slot)

    wait_all(slot)
    ql = ql_ref[...]
    qp = qp_ref[...]
    kc = kbuf[slot].astype(BF16)
    kr = rbuf[slot].astype(BF16)
    s_past = _dot_nt(ql, kc) + _dot(qp, kr)
    cn = cn_ref[...].astype(BF16).astype(F32)
    kn = kn_ref[...].astype(BF16).astype(F32)
    s_new = (jnp.sum(ql.astype(F32) * cn, axis=-1, keepdims=True)
             + jnp.sum(qp.astype(F32) * kn, axis=-1, keepdims=True))
    m = jnp.maximum(jnp.max(s_past, axis=-1, keepdims=True), s_new)
    p_past = jnp.exp(s_past - m)
    p_new = jnp.exp(s_new - m)
    denom = jnp.sum(p_past, axis=-1, keepdims=True) + p_new
    o = _dot(p_past.astype(BF16), kc) + p_new.astype(BF16).astype(F32) * cn
    o_ref[...] = o / denom


def _attention_decode(qlat, qpe, c_new, kpe_new, lat_cache, rope_cache, page_table):
    b = qlat.shape[0]
    h = MLA_HEADS
    n_pages = page_table.shape[1]
    page = lat_cache.shape[2]
    past = n_pages * page
    blk = lambda w: pl.BlockSpec((None, h, w), lambda i, pt: (i, 0, 0))
    one = lambda w: pl.BlockSpec((None, 1, w), lambda i, pt: (i, 0, 0))
    grid_spec = pltpu.PrefetchScalarGridSpec(
        num_scalar_prefetch=1,
        grid=(b,),
        in_specs=[blk(KV_LORA), blk(MLA_ROPE), one(KV_LORA), one(MLA_ROPE),
                  pl.BlockSpec(memory_space=pl.ANY), pl.BlockSpec(memory_space=pl.ANY)],
        out_specs=blk(KV_LORA),
        scratch_shapes=[pltpu.VMEM((2, past, KV_LORA), F32), pltpu.VMEM((2, MLA_ROPE, past), F32),
                        pltpu.SemaphoreType.DMA((2, 2))],
    )
    return pl.pallas_call(
        _attn_decode_kernel,
        grid_spec=grid_spec,
        out_shape=jax.ShapeDtypeStruct((b, h, KV_LORA), F32),
        compiler_params=_params("arbitrary"),
        name="attention_decode",
    )(page_table, qlat.reshape(b, h, KV_LORA), qpe.reshape(b, h, MLA_ROPE),
      c_new.reshape(b, 1, KV_LORA), kpe_new.reshape(b, 1, MLA_ROPE), lat_cache,
      jnp.swapaxes(rope_cache, 2, 3))


def _route(logits):
    lane = lax.broadcasted_iota(I32, logits.shape, 1)
    lane_f = lane.astype(F32)
    neg = -jnp.inf
    big = float(LANES)
    gmask = lane < N_GROUPS
    gl = jnp.where(gmask, logits, neg)
    gmax = jnp.max(gl, axis=-1, keepdims=True)
    g_sel = jnp.min(jnp.where(gl == gmax, lane_f, big), axis=-1, keepdims=True)
    gsum = jnp.sum(jnp.where(gmask, jnp.exp(gl - gmax), 0.0), axis=-1, keepdims=True)
    g_w = 1.0 / gsum
    lo = N_GROUPS + EXPERTS_PER_GROUP * g_sel
    emask = (lane_f >= lo) & (lane_f < lo + EXPERTS_PER_GROUP)
    el = jnp.where(emask, logits, neg)
    v1 = jnp.max(el, axis=-1, keepdims=True)
    i1 = jnp.min(jnp.where(el == v1, lane_f, big), axis=-1, keepdims=True)
    el2 = jnp.where(lane_f == i1, neg, el)
    v2 = jnp.max(el2, axis=-1, keepdims=True)
    i2 = jnp.min(jnp.where(el2 == v2, lane_f, big), axis=-1, keepdims=True)
    e2 = jnp.exp(v2 - v1)
    w1 = g_w / (1.0 + e2)
    w2 = g_w * e2 / (1.0 + e2)
    return i1 - N_GROUPS, i2 - N_GROUPS, w1, w2


def _bf16_parts(w):
    hi = w.astype(BF16).astype(F32)
    mid = (w - hi).astype(BF16).astype(F32)
    return hi, mid, w - hi - mid


def _lane_pack(shape, cols):
    lane = lax.broadcasted_iota(I32, shape, 1)
    out = jnp.zeros(shape, F32)
    for k, col in enumerate(cols):
        out = jnp.where(lane == k, col, out)
    return out


def _mix_out_kernel(*refs, n_act, has_pre, has_bias):
    refs = list(refs)
    acts = [refs.pop(0) for _ in range(n_act)]
    ws = [refs.pop(0) for _ in range(n_act)]
    pre = refs.pop(0) if has_pre else None
    bias = refs.pop(0) if has_bias else None
    h_ref, nw_ref, rwh_ref, rwl_ref, rb_ref, h1_ref, xe_ref, rid_ref, cnt_ref = refs
    out = None
    for k, (a_ref, w_ref) in enumerate(zip(acts, ws)):
        a = a_ref[...]
        if has_pre and k == n_act - 1:
            a = _dot(a.astype(BF16), pre[...])
        t = _dot(a.astype(BF16), w_ref[...])
        out = t if out is None else out + t
    if has_bias:
        out = out + bias[...]
    h1 = h_ref[...] + out
    h1_ref[...] = h1
    xn = _rms(h1, nw_ref[...])
    xh = xn.astype(BF16)
    xl = (xn - xh.astype(F32)).astype(BF16)
    tm = xn.shape[0]
    prod = _dot(jnp.concatenate([xh, xl], axis=0), jnp.concatenate([rwh_ref[...], rwl_ref[...]], axis=1))
    logits = (prod[:tm, :LANES] + prod[:tm, LANES:]) + (prod[tm:, :LANES] + prod[tm:, LANES:]) + rb_ref[...]
    e1, e2, w1, w2 = _route(logits)
    meta = _lane_pack(logits.shape, (e1, e2) + _bf16_parts(w1) + _bf16_parts(w2))
    xe_ref[...] = jnp.concatenate([xn, meta], axis=1).astype(BF16)
    rid_ref[...] = _lane_pack(logits.shape, (e1, e2)).astype(I32)
    lane_f = lax.broadcasted_iota(I32, logits.shape, 1).astype(F32)
    hit = jnp.where((lane_f == e1) | (lane_f == e2), 1.0, 0.0)
    cnt_ref[...] = jnp.sum(hit, axis=0, keepdims=True)


def _mix_out(acts, ws, h, nw, rw, rb, tm, pre=None, bias=None):
    n, d = h.shape
    row = lambda i: (i, 0)
    const = lambda i: (0, 0)
    ins = list(acts) + list(ws)
    specs = [pl.BlockSpec((tm, a.shape[1]), row) for a in acts] + [pl.BlockSpec(w.shape, const) for w in ws]
    if pre is not None:
        ins.append(pre)
        specs.append(pl.BlockSpec(pre.shape, const))
    if bias is not None:
        ins.append(bias)
        specs.append(pl.BlockSpec(bias.shape, const))
    ins += [h, nw, rw[0], rw[1], rb]
    specs += [pl.BlockSpec((tm, d), row), pl.BlockSpec((1, d), const),
              pl.BlockSpec(rw[0].shape, const), pl.BlockSpec(rw[1].shape, const), pl.BlockSpec(rb.shape, const)]
    kern = functools.partial(_mix_out_kernel, n_act=len(acts), has_pre=pre is not None,
                             has_bias=bias is not None)
    outs = pl.pallas_call(
        kern,
        grid=(n // tm,),
        in_specs=specs,
        out_specs=[pl.BlockSpec((tm, d), row), pl.BlockSpec((tm, d + LANES), row),
                   pl.BlockSpec((tm, LANES), row), pl.BlockSpec((None, 1, LANES), lambda i: (i, 0, 0))],
        out_shape=[jax.ShapeDtypeStruct((n, d), F32), jax.ShapeDtypeStruct((n, d + LANES), BF16),
                   jax.ShapeDtypeStruct((n, LANES), I32), jax.ShapeDtypeStruct((n // tm, 1, LANES), F32)],
        compiler_params=_params("parallel"),
        name="mix_out_route",
    )(*ins)
    return tuple(outs)


SEG_ALIGN = 16
FFN_ROWS = 512
PERM_ROWS = 256


def _seg_bits(t):
    bits = []
    b = SEG_ALIGN
    while b <= t:
        bits.append(b)
        b *= 2
    return tuple(reversed(bits))


def _slot_cap(t):
    cap = 2 * t + N_EXPERTS * (SEG_ALIGN - 1)
    return -(-cap // PERM_ROWS) * PERM_ROWS


def _moe_meta(counts, tile_tokens):
    cnt = counts[:, :N_EXPERTS].astype(I32)
    ell = ((cnt + SEG_ALIGN - 1) // SEG_ALIGN) * SEG_ALIGN
    off = jnp.cumsum(ell, axis=1) - ell
    tot = jnp.sum(ell, axis=0)
    region = ((tot + FFN_ROWS - 1) // FFN_ROWS) * FFN_ROWS
    gend = jnp.cumsum(region)
    gbase = gend - region
    gseg = gbase[None, :] + jnp.cumsum(ell, axis=0) - ell
    n_pairs = 2 * sum(tile_tokens)
    n_rows_max = n_pairs + len(tile_tokens) * N_EXPERTS * (SEG_ALIGN - 1) + N_EXPERTS * (FFN_ROWS - 1)
    n_steps = -(-n_rows_max // FFN_ROWS)
    starts = jnp.arange(n_steps, dtype=I32) * FFN_ROWS
    step_expert = jnp.minimum(jnp.sum((gend[None, :] <= starts[:, None]).astype(I32), axis=1), N_EXPERTS - 1)
    n_used = (gend[-1] // FFN_ROWS).reshape(1)
    off_f = off.astype(F32)
    return dict(off=off, ell=ell, gseg=gseg, tail_start=gbase + tot, tail_len=region - tot, end=gend[-1:],
                step_expert=step_expert, n_used=n_used, n_rows=n_steps * FFN_ROWS,
                off_col=off_f[:, :, None], off_row=jnp.pad(off_f, ((0, 0), (0, LANES - N_EXPERTS)))[:, None, :])


def _chunk_copies(n, bits, make):
    for b in bits:
        hi = n & (-(2 * b))

        @pl.when((n & b) != 0)
        def _():
            make(hi, b)


def _segment_copies(n, make):
    big = 2 * SEG_ALIGN
    shift = big.bit_length() - 1
    n_big = lax.shift_right_logical(n, shift)

    def body(k, carry):
        make(lax.shift_left(k, shift), big)
        return carry

    lax.fori_loop(0, n_big, body, 0)

    @pl.when((n & SEG_ALIGN) != 0)
    def _():
        make(lax.shift_left(n_big, shift), SEG_ALIGN)


def _pair_slots_rows(ids_t, off_col, t):
    sub = lax.broadcasted_iota(I32, (N_EXPERTS, t), 0).astype(F32)
    e1 = jnp.where(sub == ids_t[0:1, :], 1.0, 0.0)
    e2 = jnp.where(sub == ids_t[1:2, :], 1.0, 0.0)
    before = jnp.where(lax.broadcasted_iota(I32, (t, t), 0) < lax.broadcasted_iota(I32, (t, t), 1), 1.0, 0.0)
    before = before.astype(BF16)
    cum1 = _dot(e1.astype(BF16), before)
    cum2 = _dot(e2.astype(BF16), before)
    cnt1 = jnp.sum(e1, axis=1, keepdims=True)
    slot0 = jnp.sum(e1 * (off_col + cum1), axis=0, keepdims=True)
    slot1 = jnp.sum(e2 * (off_col + cnt1 + cum2), axis=0, keepdims=True)
    return slot0, slot1


def _pair_slots_cols(ids, off_row, t):
    lane = lax.broadcasted_iota(I32, (t, LANES), 1).astype(F32)
    idf = ids.astype(F32)
    e1 = jnp.where(lane == idf[:, 0:1], 1.0, 0.0)
    e2 = jnp.where(lane == idf[:, 1:2], 1.0, 0.0)
    before = jnp.where(lax.broadcasted_iota(I32, (t, t), 1) < lax.broadcasted_iota(I32, (t, t), 0), 1.0, 0.0)
    before = before.astype(BF16)
    cum1 = _dot(before, e1.astype(BF16))
    cum2 = _dot(before, e2.astype(BF16))
    cnt1 = jnp.sum(e1, axis=0, keepdims=True)
    slot0 = jnp.sum(e1 * (off_row + cum1), axis=1, keepdims=True)
    slot1 = jnp.sum(e2 * (off_row + cnt1 + cum2), axis=1, keepdims=True)
    return slot0, slot1


def _permute_kernel(off_ref, ell_ref, gseg_ref, tstart_ref, tlen_ref, end_ref, xe_ref, ids_ref, offcol_ref, *rest,
                    tile0, zero_tiles):
    xg_hbm, xs_scr, zero_scr, sem = rest[-4:]
    s = pl.program_id(0)
    last = pl.num_programs(0) - 1
    t = xe_ref.shape[0]
    cap = xs_scr.shape[1]
    slot = s % 2

    def seg_start(tile, buf):
        def body(e, carry):
            o = off_ref[tile, e]
            g = gseg_ref[tile, e]

            def make(hi, b):
                src = xs_scr.at[buf, pl.ds(pl.multiple_of(o + hi, SEG_ALIGN), b)]
                dst = xg_hbm.at[pl.ds(pl.multiple_of(g + hi, SEG_ALIGN), b)]
                pltpu.make_async_copy(src, dst, sem.at[buf]).start()

            _segment_copies(ell_ref[tile, e], make)
            return carry

        lax.fori_loop(0, N_EXPERTS, body, 0)

    def seg_wait(tile, buf):
        def make(hi, b):
            pltpu.make_async_copy(xs_scr.at[buf, pl.ds(0, b)], xg_hbm.at[pl.ds(0, b)], sem.at[buf]).wait()

        _chunk_copies(off_ref[tile, N_EXPERTS - 1] + ell_ref[tile, N_EXPERTS - 1], _seg_bits(cap), make)

    if zero_tiles is not None:
        zrows = zero_scr.shape[0]

        def zero_fill(g, n, fn):
            def make(hi, b):
                dst = xg_hbm.at[pl.ds(pl.multiple_of(g + hi, SEG_ALIGN), b)]
                fn(pltpu.make_async_copy(zero_scr.at[pl.ds(0, b)], dst, sem.at[2]))

            _chunk_copies(n, _seg_bits(zrows), make)

        def zero_copies(fn):
            def body(e, carry):
                zero_fill(tstart_ref[e], tlen_ref[e], fn)
                for zt in zero_tiles:
                    zero_fill(gseg_ref[zt, e], ell_ref[zt, e], fn)
                return carry

            lax.fori_loop(0, N_EXPERTS, body, 0)

            def trailing(k, carry):
                dst = xg_hbm.at[pl.ds(pl.multiple_of(end_ref[0] + k * zrows, SEG_ALIGN), zrows)]
                fn(pltpu.make_async_copy(zero_scr, dst, sem.at[2]))
                return carry

            n_trailing = lax.shift_right_logical(xg_hbm.shape[0] - end_ref[0], zrows.bit_length() - 1)
            lax.fori_loop(0, n_trailing, trailing, 0)

        @pl.when(s == 0)
        def _():
            zero_scr[...] = jnp.zeros_like(zero_scr)
            zero_copies(lambda cp: cp.start())
            zero_copies(lambda cp: cp.wait())

    @pl.when(s >= 2)
    def _():
        seg_wait(tile0 + s - 2, slot)

    tile = tile0 + s
    ids_t = ids_ref[...].astype(F32).T
    slot0, slot1 = _pair_slots_rows(ids_t, offcol_ref[...], t)
    used = off_ref[tile, N_EXPERTS - 1] + ell_ref[tile, N_EXPERTS - 1]
    xe = xe_ref[...]
    for r0 in range(0, cap, PERM_ROWS):
        @pl.when(r0 < used)
        def _():
            rows = (lax.broadcasted_iota(I32, (PERM_ROWS, t), 0) + r0).astype(F32)
            pm = jnp.where((rows == slot0) | (rows == slot1), 1.0, 0.0).astype(BF16)
            xs_scr[slot, pl.ds(r0, PERM_ROWS), :] = _dot(pm, xe).astype(BF16)

    seg_start(tile, slot)

    @pl.when(s == last)
    def _():
        @pl.when(s >= 1)
        def _():
            seg_wait(tile - 1, 1 - slot)

        seg_wait(tile, slot)


def _moe_permute(meta, xe, ids, t, tile0, xg_prev=None):
    n, de = xe.shape
    cap = _slot_cap(t)
    n_tiles = meta["off"].shape[0]
    zero_tiles = None if xg_prev is not None else tuple(range(tile0 + n // t, n_tiles))
    prefetch = [meta["off"], meta["ell"], meta["gseg"], meta["tail_start"], meta["tail_len"], meta["end"]]
    imap = lambda s, *_: (s, 0)
    in_specs = [pl.BlockSpec((t, de), imap), pl.BlockSpec((t, LANES), imap),
                pl.BlockSpec((None, N_EXPERTS, 1), lambda s, *_: (tile0 + s, 0, 0))]
    ins = [xe, ids, meta["off_col"]]
    aliases = {}
    if xg_prev is not None:
        in_specs.append(pl.BlockSpec(memory_space=pl.ANY))
        ins.append(xg_prev)
        aliases = {len(prefetch) + len(ins) - 1: 0}
    grid_spec = pltpu.PrefetchScalarGridSpec(
        num_scalar_prefetch=len(prefetch),
        grid=(n // t,),
        in_specs=in_specs,
        out_specs=pl.BlockSpec(memory_space=pl.ANY),
        scratch_shapes=[pltpu.VMEM((2, cap, de), BF16), pltpu.VMEM((FFN_ROWS // 2, de), BF16),
                        pltpu.SemaphoreType.DMA((3,))],
    )
    return pl.pallas_call(
        functools.partial(_permute_kernel, tile0=tile0, zero_tiles=zero_tiles),
        grid_spec=grid_spec,
        out_shape=jax.ShapeDtypeStruct((meta["n_rows"], de), BF16),
        input_output_aliases=aliases,
        compiler_params=_params("arbitrary"),
        name="moe_permute",
    )(*prefetch, *ins)


def _ffn_kernel(te_ref, nu_ref, x_ref, wg_ref, wu_ref, wd_ref, o_ref, wg_b, wu_b, wd_b):
    i = pl.program_id(0)
    e = te_ref[i]
    d = o_ref.shape[1]

    @pl.when((i == 0) | (e != te_ref[jnp.maximum(i - 1, 0)]))
    def _():
        wg_b[...] = wg_ref[...].astype(BF16)
        wu_b[...] = wu_ref[...].astype(BF16)
        wd_b[...] = wd_ref[...].astype(BF16)

    @pl.when(i < nu_ref[0])
    def _():
        x = x_ref[:, :d]
        meta = x_ref[:, d:].astype(F32)
        w1 = meta[:, 2:3] + meta[:, 3:4] + meta[:, 4:5]
        w2 = meta[:, 5:6] + meta[:, 6:7] + meta[:, 7:8]
        w = jnp.where(meta[:, 0:1] == e.astype(F32), w1, w2)
        g = _dot(x, wg_b[...])
        u = _dot(x, wu_b[...])
        hmid = (g * _sigmoid(g) * u).astype(BF16)
        o_ref[...] = (_dot(hmid, wd_b[...]) * w).astype(BF16)

    @pl.when(i >= nu_ref[0])
    def _():
        o_ref[...] = jnp.zeros_like(o_ref)


def _moe_ffn(xs, tile_expert, n_used, wg, wu, wd, tr, layer):
    n_rows, d = xs.shape[0], wg.shape[2]
    de = wg.shape[3]
    wmap = lambda i, te, nu: (layer, te[i], 0, 0)
    grid_spec = pltpu.PrefetchScalarGridSpec(
        num_scalar_prefetch=2,
        grid=(n_rows // tr,),
        in_specs=[pl.BlockSpec((tr, xs.shape[1]), lambda i, te, nu: (i, 0)),
                  pl.BlockSpec((None, None, d, de), wmap),
                  pl.BlockSpec((None, None, d, de), wmap),
                  pl.BlockSpec((None, None, de, d), wmap)],
        out_specs=pl.BlockSpec((tr, d), lambda i, te, nu: (i, 0)),
        scratch_shapes=[pltpu.VMEM((d, de), BF16), pltpu.VMEM((d, de), BF16), pltpu.VMEM((de, d), BF16)],
    )
    return pl.pallas_call(
        _ffn_kernel,
        grid_spec=grid_spec,
        out_shape=jax.ShapeDtypeStruct((n_rows, d), BF16),
        compiler_params=_params("arbitrary"),
        name="moe_ffn",
    )(tile_expert, n_used, xs, wg, wu, wd)


def _unpermute_kernel(off_ref, ell_ref, gseg_ref, ids_ref, offrow_ref, h_ref, yg_hbm, o_ref, ys_scr, sem,
                      *, tile0):
    s = pl.program_id(0)
    t = h_ref.shape[0]
    cap = ys_scr.shape[1]
    slot = s % 2

    def seg_start(tile, buf):
        def body(e, carry):
            o = off_ref[tile, e]
            g = gseg_ref[tile, e]

            def make(hi, b):
                src = yg_hbm.at[pl.ds(pl.multiple_of(g + hi, SEG_ALIGN), b)]
                dst = ys_scr.at[buf, pl.ds(pl.multiple_of(o + hi, SEG_ALIGN), b)]
                pltpu.make_async_copy(src, dst, sem.at[buf]).start()

            _segment_copies(ell_ref[tile, e], make)
            return carry

        lax.fori_loop(0, N_EXPERTS, body, 0)

    def seg_wait(tile, buf):
        def make(hi, b):
            pltpu.make_async_copy(yg_hbm.at[pl.ds(0, b)], ys_scr.at[buf, pl.ds(0, b)], sem.at[buf]).wait()

        _chunk_copies(off_ref[tile, N_EXPERTS - 1] + ell_ref[tile, N_EXPERTS - 1], _seg_bits(cap), make)

    tile = tile0 + s

    @pl.when(s == 0)
    def _():
        ys_scr[...] = jnp.zeros_like(ys_scr)
        seg_start(tile, slot)

    @pl.when(s + 1 < pl.num_programs(0))
    def _():
        seg_start(tile + 1, 1 - slot)

    seg_wait(tile, slot)
    slot0, slot1 = _pair_slots_cols(ids_ref[...], offrow_ref[...], t)
    cols = lax.broadcasted_iota(I32, (t, cap), 1).astype(F32)
    pu = jnp.where((cols == slot0) | (cols == slot1), 1.0, 0.0).astype(BF16)
    o_ref[...] = h_ref[...] + _dot(pu, ys_scr[slot])


def _moe_unpermute(meta, h, ids, yg, t, tile0):
    n, d = h.shape
    cap = _slot_cap(t)
    prefetch = [meta["off"], meta["ell"], meta["gseg"]]
    imap = lambda s, *_: (s, 0)
    grid_spec = pltpu.PrefetchScalarGridSpec(
        num_scalar_prefetch=len(prefetch),
        grid=(n // t,),
        in_specs=[pl.BlockSpec((t, LANES), imap),
                  pl.BlockSpec((None, 1, LANES), lambda s, *_: (tile0 + s, 0, 0)),
                  pl.BlockSpec((t, d), imap), pl.BlockSpec(memory_space=pl.ANY)],
        out_specs=pl.BlockSpec((t, d), imap),
        scratch_shapes=[pltpu.VMEM((2, cap, d), BF16), pltpu.SemaphoreType.DMA((2,))],
    )
    return pl.pallas_call(
        functools.partial(_unpermute_kernel, tile0=tile0),
        grid_spec=grid_spec,
        out_shape=jax.ShapeDtypeStruct((n, d), F32),
        compiler_params=_params("arbitrary"),
        name="moe_unpermute",
    )(*prefetch, ids, meta["off_row"], h, yg)


def _moe(groups, wg, wu, wd, layer):
    tiles = [t for g in groups for t in [g[4]] * (g[0].shape[0] // g[4])]
    meta = _moe_meta(jnp.concatenate([g[3].reshape(-1, LANES) for g in groups], axis=0), tiles)
    xg = None
    tile0 = 0
    for h1, xe, ids, _, t in groups:
        xg = _moe_permute(meta, xe, ids, t, tile0, xg)
        tile0 += h1.shape[0] // t
    yg = _moe_ffn(xg, meta["step_expert"], meta["n_used"], wg, wu, wd, FFN_ROWS, layer)
    outs = []
    tile0 = 0
    for h1, xe, ids, _, t in groups:
        outs.append(_moe_unpermute(meta, h1, ids, yg, t, tile0))
        tile0 += h1.shape[0] // t
    return outs


def _ple_kernel(*refs, tail):
    h_ref, p_ref, proj_ref, pn_ref, gate_ref = refs[:5]
    h = h_ref[...]
    pp = _dot(p_ref[...].astype(BF16), proj_ref[...])
    g = _dot(_rms(h, pn_ref[...]).astype(BF16), gate_ref[...])
    h3 = h + pp * _sigmoid(g)
    if tail == "conv_in":
        nm_ref, pw1_ref, pb1_ref, h3_ref, a_ref = refs[5:]
        h3_ref[...] = h3
        d = h3.shape[1]
        hn = _rms(h3, nm_ref[...]).astype(BF16)
        ua = _dot(hn, pw1_ref[:, :d]) + pb1_ref[:, :d]
        ub = _dot(hn, pw1_ref[:, d:]) + pb1_ref[:, d:]
        a_ref[...] = ua * _sigmoid(ub)
    else:
        nf_ref, y_ref = refs[5:]
        y_ref[...] = _rms(h3, nf_ref[...])


def _ple(h, p, layer, proj, pn, gate, tm, tail, extra):
    n, d = h.shape
    row = lambda i: (i, 0)
    const = lambda i: (0, 0)
    ins = [h, p, proj, pn, gate] + list(extra)
    specs = [pl.BlockSpec((tm, d), row), pl.BlockSpec((None, tm, p.shape[2]), lambda i: (layer, i, 0)),
             pl.BlockSpec(proj.shape, const), pl.BlockSpec((1, d), const),
             pl.BlockSpec(gate.shape, const)] + [pl.BlockSpec(e.shape, const) for e in extra]
    n_out = 2 if tail == "conv_in" else 1
    return pl.pallas_call(
        functools.partial(_ple_kernel, tail=tail),
        grid=(n // tm,),
        in_specs=specs,
        out_specs=[pl.BlockSpec((tm, d), row)] * n_out,
        out_shape=[jax.ShapeDtypeStruct((n, d), F32)] * n_out,
        compiler_params=_params("parallel"),
        name="ple_" + tail,
    )(*ins)


def _ln_swish(y, w, b):
    mu = jnp.mean(y, axis=-1, keepdims=True)
    var = jnp.mean(jnp.square(y - mu), axis=-1, keepdims=True)
    z = (y - mu) * lax.rsqrt(var + EPS) * w + b
    return z * _sigmoid(z)


def _conv_kernel(a_ref, dw_ref, dwb_ref, lnw_ref, lnb_ref, o_ref, ext, shifted, acc):
    t = pl.program_id(1)
    tl = a_ref.shape[0]
    halo = ext.shape[0] - tl
    sub = 8

    @pl.when(t == 0)
    def _():
        ext[0:halo, :] = jnp.zeros((halo, ext.shape[1]), F32)

    ext[halo:, :] = a_ref[...]
    first = halo - (CONV_WIDTH - 1)
    srows = shifted.shape[1]
    for c0 in range(0, ext.shape[1], LANES):
        cs = slice(c0, c0 + LANES)
        for r in range(1, sub):
            shifted[r - 1] = ext[r:r + srows, cs]
        s = None
        for j in range(CONV_WIDTH):
            r = (first + j) % sub
            base = first + j - r
            win = ext[base:base + tl, cs] if r == 0 else shifted[r - 1, base:base + tl, :]
            term = win * dw_ref[j:j + 1, cs]
            s = term if s is None else s + term
        acc[:, cs] = s
    ext[0:halo, :] = ext[tl:tl + halo, :]
    o_ref[...] = _ln_swish(acc[...] + dwb_ref[...], lnw_ref[...], lnb_ref[...]).astype(BF16)


def _conv_prompt(a, dw, dwb, lnw, lnb, b, l, tl=256):
    d = a.shape[1]
    nt = l // tl
    halo = 32
    row = lambda bi, t: (bi * nt + t, 0)
    const = lambda bi, t: (0, 0)
    return pl.pallas_call(
        _conv_kernel,
        grid=(b, nt),
        in_specs=[pl.BlockSpec((tl, d), row), pl.BlockSpec(dw.shape, const),
                  pl.BlockSpec((1, d), const), pl.BlockSpec((1, d), const), pl.BlockSpec((1, d), const)],
        out_specs=pl.BlockSpec((tl, d), row),
        out_shape=jax.ShapeDtypeStruct((b * l, d), BF16),
        scratch_shapes=[pltpu.VMEM((tl + halo, d), F32), pltpu.VMEM((7, tl + halo - 8, LANES), F32),
                        pltpu.VMEM((tl, d), F32)],
        compiler_params=_params("parallel", "arbitrary"),
        name="conv_prompt",
    )(a, dw, dwb, lnw, lnb)


def _conv_decode_kernel(buf_ref, a_ref, dw_ref, dwb_ref, lnw_ref, lnb_ref, y_ref, new_ref):
    w = CONV_WIDTH - 1
    a = a_ref[...]
    y = a * dw_ref[w:w + 1, :]
    for j in range(w):
        y = y + buf_ref[j] * dw_ref[j:j + 1, :]
    y_ref[...] = _ln_swish(y + dwb_ref[...], lnw_ref[...], lnb_ref[...])
    for j in range(w - 1):
        new_ref[j] = buf_ref[j + 1]
    new_ref[w - 1] = a


def _conv_decode(state, a, dw, dwb, lnw, lnb, bb=8):
    _, b, w, d = state.shape
    row = pl.BlockSpec((bb, d), lambda i: (i, 0))
    sspec = pl.BlockSpec((None, w, bb, d), lambda i: (0, 0, i, 0))
    const = lambda i: (0, 0)
    y, new = pl.pallas_call(
        _conv_decode_kernel,
        grid=(b // bb,),
        in_specs=[sspec, row, pl.BlockSpec(dw.shape, const), pl.BlockSpec((1, d), const),
                  pl.BlockSpec((1, d), const), pl.BlockSpec((1, d), const)],
        out_specs=[row, sspec],
        out_shape=[jax.ShapeDtypeStruct((b, d), F32), jax.ShapeDtypeStruct((1, w, b, d), F32)],
        compiler_params=_params("parallel"),
        name="conv_decode",
    )(jnp.transpose(state, (0, 2, 1, 3)), a, dw, dwb, lnw, lnb)
    return y, jnp.transpose(new, (0, 2, 1, 3))


def _prep_weights(W):
    P = {}
    w_in = W["w_in"][0]
    kpe_cols = w_in[:, -MLA_ROPE:]
    P["w_in"] = jnp.concatenate([w_in[:, :-MLA_ROPE]] + [kpe_cols] * (LANES // MLA_ROPE), axis=1).astype(BF16)
    w_uq = W["w_uq"][0]
    nope = jnp.pad(w_uq[..., :MLA_NOPE], ((0, 0), (0, 0), (0, LANES - MLA_NOPE)))
    P["w_uq_nope"] = nope.reshape(Q_LORA, MLA_HEADS * LANES).astype(BF16)
    P["w_uq_pe"] = w_uq[..., MLA_NOPE:].reshape(Q_LORA, MLA_HEADS * MLA_ROPE).astype(BF16)
    w_ukv = W["w_ukv"][0]
    wk = jnp.transpose(w_ukv[..., :MLA_NOPE], (1, 2, 0))
    P["wk"] = jnp.pad(wk, ((0, 0), (0, LANES - MLA_NOPE), (0, 0))).astype(BF16)
    wv = jnp.transpose(w_ukv[..., MLA_NOPE:], (1, 0, 2))
    P["wv"] = wv.astype(BF16)
    eye = jnp.eye(MLA_HEADS, dtype=F32)
    P["wv_bd"] = (wv[:, :, None, :] * eye[:, None, :, None]).reshape(
        MLA_HEADS * KV_LORA, MLA_HEADS * MLA_V).astype(BF16)
    w_out = W["w_out"][0].astype(BF16)
    P["w_out_r"] = w_out[:RET_HEADS * RET_DV]
    P["w_out_m"] = w_out[RET_HEADS * RET_DV:]
    d = w_out.shape[1]
    for i in range(W["moe_w_group"].shape[0]):
        rw = jnp.concatenate([W["moe_w_group"][i], W["moe_w_router"][i].reshape(d, N_EXPERTS)], axis=1)
        rb = jnp.concatenate([W["moe_b_group"][i], W["moe_b_router"][i].reshape(N_EXPERTS)])
        padc = LANES - rw.shape[1]
        rw = jnp.pad(rw, ((0, 0), (0, padc)))
        rw_hi = rw.astype(BF16)
        P["router_w", i] = (rw_hi, (rw - rw_hi.astype(F32)).astype(BF16))
        P["router_b", i] = jnp.pad(rb, (0, padc)).reshape(1, LANES)
        P["ple_proj", i] = W["ple_proj"][i].astype(BF16)
        P["ple_gate", i] = W["ple_gate"][i].astype(BF16)
    P["pw1"] = W["conv_pw1"][0].astype(BF16)
    P["pw2"] = W["conv_pw2"][0].astype(BF16)
    P["dw"] = jnp.pad(W["conv_dw"][0], ((0, 1), (0, 0)))
    return P


def _vec(v):
    return v.reshape(1, -1)


def _mixer0(x, pos, W, P, tm, past):
    b, l, d = x.shape
    n = b * l
    h = x.reshape(n, d)
    tabs = _rope_tables(pos, RET_DK) + _rope_tables(pos, MLA_ROPE)
    rq, rk, rv, rg, qlat, qpe, c, kpe, kcat = _proj_in(
        h, _vec(W["norm_mix"][0]), P["w_in"], tabs, _vec(W["mla_q_norm"][0]), _vec(W["mla_kv_norm"][0]),
        P["w_uq_nope"], P["w_uq_pe"], P["wk"], tm)
    gnw = _vec(W["ret_gn_w"][0])
    if past is None:
        o_r, s_new = _retention_prompt(rq, rk, rv, rg, gnw, b, l)
        o_m = _attention_prompt(qlat, qpe, kcat, P["wv"], b, l)
        pre = None
    else:
        ret_state, lat_cache, rope_cache, page_table = past
        o_r, s_new = _retention_decode(rq, rk, rv, rg, gnw, ret_state)
        o_m = _attention_decode(qlat, qpe, c, kpe, lat_cache, rope_cache, page_table)
        o_m = o_m.reshape(n, MLA_HEADS * KV_LORA)
        pre = P["wv_bd"]
    routed = _mix_out([o_r, o_m], [P["w_out_r"], P["w_out_m"]], h, _vec(W["norm_ffn"][0]),
                      P["router_w", 0], P["router_b", 0], tm, pre=pre)
    return routed, (c.reshape(1, b, l, KV_LORA), kpe.reshape(1, b, l, MLA_ROPE), s_new)


def _mixer1(h2, p, W, P, tm, b, l, conv_state):
    n, d = h2.shape
    h3, a = _ple(h2, p.reshape(p.shape[0], n, -1), 0, P["ple_proj", 0], _vec(W["ple_norm"][0]), P["ple_gate", 0],
                 tm, "conv_in", [_vec(W["norm_mix"][1]), P["pw1"], _vec(W["conv_pw1_b"][0])])
    conv_vecs = (_vec(W["conv_dw_b"][0]), _vec(W["conv_ln_w"][0]), _vec(W["conv_ln_b"][0]))
    if conv_state is None:
        yact = _conv_prompt(a, P["dw"], *conv_vecs, b, l)
        new_state = a.reshape(b, l, d)[:, l - (CONV_WIDTH - 1):, :][None]
    else:
        yact, new_state = _conv_decode(conv_state, a, P["dw"], *conv_vecs)
    routed = _mix_out([yact], [P["pw2"]], h3, _vec(W["norm_ffn"][1]),
                      P["router_w", 1], P["router_b", 1], tm, bias=_vec(W["conv_pw2_b"][0]))
    return routed, new_state


def _final(h5, p, W, P, tm, shape):
    n = h5.shape[0]
    (y,) = _ple(h5, p.reshape(p.shape[0], n, -1), 1, P["ple_proj", 1], _vec(W["ple_norm"][1]), P["ple_gate", 1],
                tm, "final", [_vec(W["norm_final"])])
    return y.reshape(shape)


def kernel(x_prompt, x_sample, p_prompt, p_sample, cache_mla_latent, cache_mla_rope, state_retention, state_conv,
           page_table, norm_mix, norm_ffn, norm_final, w_in, ret_gn_w, mla_q_norm, mla_kv_norm, w_uq, w_ukv, w_out,
           conv_pw1, conv_pw1_b, conv_dw, conv_dw_b, conv_ln_w, conv_ln_b, conv_pw2, conv_pw2_b,
           moe_w_group, moe_b_group, moe_w_router, moe_b_router, moe_w_gate, moe_w_up, moe_w_down,
           ple_proj, ple_norm, ple_gate):
    W = dict(norm_mix=norm_mix, norm_ffn=norm_ffn, norm_final=norm_final, w_in=w_in, ret_gn_w=ret_gn_w,
             mla_q_norm=mla_q_norm, mla_kv_norm=mla_kv_norm, w_uq=w_uq, w_ukv=w_ukv, w_out=w_out,
             conv_pw1=conv_pw1, conv_pw1_b=conv_pw1_b, conv_dw=conv_dw, conv_dw_b=conv_dw_b, conv_ln_w=conv_ln_w,
             conv_ln_b=conv_ln_b, conv_pw2=conv_pw2, conv_pw2_b=conv_pw2_b, moe_w_group=moe_w_group,
             moe_b_group=moe_b_group, moe_w_router=moe_w_router, moe_b_router=moe_b_router, moe_w_gate=moe_w_gate,
             moe_w_up=moe_w_up, moe_w_down=moe_w_down, ple_proj=ple_proj, ple_norm=ple_norm, ple_gate=ple_gate)
    assert w_in.shape[0] == 1 and conv_pw1.shape[0] == 1, "one retention/attention layer, one conv layer"
    P = _prep_weights(W)
    past_len = page_table.shape[1] * cache_mla_latent.shape[2]
    pos_prompt = jnp.arange(x_prompt.shape[1])
    pos_sample = past_len + jnp.arange(x_sample.shape[1])
    bp, lp, _ = x_prompt.shape
    bs, ls, _ = x_sample.shape
    tm_p = min(512, lp)
    tm_s = bs * ls
    moe_w = lambda i: (moe_w_gate, moe_w_up, moe_w_down, i)
    routed_p, (lat_p, rope_p, ret_p) = _mixer0(x_prompt, pos_prompt, W, P, tm_p, None)
    routed_s, (lat_s, rope_s, ret_s) = _mixer0(
        x_sample, pos_sample, W, P, tm_s, (state_retention, cache_mla_latent, cache_mla_rope, page_table))
    h2_p, h2_s = _moe([routed_p + (tm_p,), routed_s + (tm_s,)], *moe_w(0))
    routed_p, conv_p = _mixer1(h2_p, p_prompt, W, P, tm_p, bp, lp, None)
    routed_s, conv_s = _mixer1(h2_s, p_sample, W, P, tm_s, bs, ls, state_conv)
    h5_p, h5_s = _moe([routed_p + (tm_p,), routed_s + (tm_s,)], *moe_w(1))
    y_p = _final(h5_p, p_prompt, W, P, tm_p, x_prompt.shape)
    y_s = _final(h5_s, p_sample, W, P, tm_s, x_sample.shape)
    return (y_p, y_s, lat_p, rope_p, ret_p, conv_p, lat_s, rope_s, ret_s, conv_s)
```

```python
import functools

import jax
import jax.numpy as jnp
from jax import lax
from jax.experimental import pallas as pl
from jax.experimental.pallas import tpu as pltpu

F32 = jnp.float32
BF16 = jnp.bfloat16
I32 = jnp.int32

EPS = 1e-6
ROPE_THETA = 10000.0
RET_HEADS = 8
RET_DK = 64
RET_DV = 64
RET_CHUNK = 128
MLA_HEADS = 8
MLA_NOPE = 64
MLA_ROPE = 32
MLA_V = 64
Q_LORA = 256
KV_LORA = 128
CONV_WIDTH = 31
N_GROUPS = 4
EXPERTS_PER_GROUP = 8
N_EXPERTS = N_GROUPS * EXPERTS_PER_GROUP
LANES = 128
Q_BLOCK = 256
KV_BLOCK = 256
VMEM_LIMIT_BYTES = 56 * 1024 * 1024


def _params(*sem):
    return pltpu.CompilerParams(dimension_semantics=sem, vmem_limit_bytes=VMEM_LIMIT_BYTES)


def _rms(x, w):
    return x * lax.rsqrt(jnp.mean(x * x, axis=-1, keepdims=True) + EPS) * w


def _sigmoid(x):
    return 1.0 / (1.0 + jnp.exp(-x))


def _dot(a, b):
    return jnp.dot(a, b, preferred_element_type=F32)


def _dot_nt(a, b):
    return lax.dot_general(a, b, (((1,), (1,)), ((), ())), preferred_element_type=F32)


def _dot_tn(a, b):
    return lax.dot_general(a, b, (((0,), (0,)), ((), ())), preferred_element_type=F32)


def _rope(x, cos_t, sin_t, period):
    half = period // 2
    lane = lax.broadcasted_iota(I32, x.shape, 1)
    first = (lane % period) < half
    rot = jnp.where(first, pltpu.roll(x, LANES - half, 1), pltpu.roll(x, half, 1))
    return x * cos_t + rot * sin_t


def _rope_tables(pos, period):
    half = period // 2
    inv = ROPE_THETA ** (-2.0 * jnp.arange(half, dtype=F32) / period)
    ang = pos.astype(F32)[:, None] * inv[None, :]
    cos = jnp.cos(ang)
    sin = jnp.sin(ang)
    cos_p = jnp.concatenate([cos, cos], axis=1)
    sin_p = jnp.concatenate([-sin, sin], axis=1)
    reps = LANES // period
    return jnp.tile(cos_p, (1, reps)), jnp.tile(sin_p, (1, reps))


def _proj_in_kernel(x_ref, nw_ref, win_ref, cr_ref, sr_ref, cp_ref, sp_ref, qn_ref, kvn_ref,
                    wuqn_ref, wuqp_ref, wk_ref,
                    rq_ref, rk_ref, rv_ref, rg_ref, qlat_ref, qpe_ref, c_ref, kpe_ref, kcat_ref):
    hn = _rms(x_ref[...], nw_ref[...]).astype(BF16)
    cr, sr, cp, sp = cr_ref[...], sr_ref[...], cp_ref[...], sp_ref[...]
    hk = RET_HEADS * RET_DK
    hv = RET_HEADS * RET_DV

    def mm(lo, width):
        return _dot(hn, win_ref[:, lo:lo + width])

    u = mm(0, hk)
    rq_ref[...] = jnp.concatenate(
        [_rope(u[:, g:g + LANES], cr, sr, RET_DK) for g in range(0, hk, LANES)], axis=1).astype(BF16)
    u = mm(hk, hk)
    rk_ref[...] = jnp.concatenate(
        [_rope(u[:, g:g + LANES], cr, sr, RET_DK) * (RET_DK ** -0.5) for g in range(0, hk, LANES)],
        axis=1).astype(BF16)
    rv_ref[...] = mm(2 * hk, hv).astype(BF16)
    rg_ref[...] = mm(2 * hk + hv, hv)
    off = 2 * hk + 2 * hv
    cqn = _rms(mm(off, Q_LORA), qn_ref[...]).astype(BF16)
    c = _rms(mm(off + Q_LORA, KV_LORA), kvn_ref[...])
    c_ref[...] = c
    kp = _rope(mm(off + Q_LORA + KV_LORA, LANES), cp, sp, MLA_ROPE)
    kpe_ref[...] = kp[:, :MLA_ROPE]
    kcat_ref[...] = jnp.concatenate([c, kp], axis=1).astype(BF16)
    scale = (MLA_NOPE + MLA_ROPE) ** -0.5
    qn = _dot(cqn, wuqn_ref[...])
    for h in range(MLA_HEADS):
        ql = _dot(qn[:, h * LANES:(h + 1) * LANES].astype(BF16), wk_ref[h]) * scale
        qlat_ref[:, h * KV_LORA:(h + 1) * KV_LORA] = ql.astype(BF16)
    qp = _dot(cqn, wuqp_ref[...])
    qpe_ref[...] = jnp.concatenate(
        [_rope(qp[:, g:g + LANES], cp, sp, MLA_ROPE) * scale
         for g in range(0, MLA_HEADS * MLA_ROPE, LANES)], axis=1).astype(BF16)


def _proj_in(x2, nw, win, tabs, qn, kvn, wuqn, wuqp, wk, tm):
    n, d = x2.shape
    cr, sr, cp, sp = tabs
    ltab = cr.shape[0]
    nblk = ltab // tm if ltab >= tm else 0
    tab_rows = tm if nblk else ltab
    tab_map = (lambda i: (i % nblk, 0)) if nblk else (lambda i: (0, 0))
    row = lambda i: (i, 0)
    const = lambda i: (0, 0)
    hk = RET_HEADS * RET_DK
    hv = RET_HEADS * RET_DV
    tab_spec = pl.BlockSpec((tab_rows, LANES), tab_map)
    out_cols = [(hk, BF16), (hk, BF16), (hv, BF16), (hv, F32), (MLA_HEADS * KV_LORA, BF16),
                (MLA_HEADS * MLA_ROPE, BF16), (KV_LORA, F32), (MLA_ROPE, F32), (2 * LANES, BF16)]
    return pl.pallas_call(
        _proj_in_kernel,
        grid=(n // tm,),
        in_specs=[pl.BlockSpec((tm, d), row), pl.BlockSpec((1, d), const),
                  pl.BlockSpec(win.shape, const), tab_spec, tab_spec, tab_spec, tab_spec,
                  pl.BlockSpec((1, Q_LORA), const), pl.BlockSpec((1, KV_LORA), const),
                  pl.BlockSpec(wuqn.shape, const), pl.BlockSpec(wuqp.shape, const),
                  pl.BlockSpec(wk.shape, lambda i: (0, 0, 0))],
        out_specs=[pl.BlockSpec((tm, w), row) for w, _ in out_cols],
        out_shape=[jax.ShapeDtypeStruct((n, w), dt) for w, dt in out_cols],
        compiler_params=_params("parallel"),
        name="proj_in",
    )(x2, nw, win, cr, sr, cp, sp, qn, kvn, wuqn, wuqp, wk)


def _ret_kernel(rq_ref, rk_ref, rv_ref, rg_ref, gnw_ref, decay_ref, qdec_ref, kdec_ref, cdec_ref,
                o_ref, s_out_ref, s_scr):
    ci = pl.program_id(1)

    @pl.when(ci == 0)
    def _():
        s_scr[...] = jnp.zeros_like(s_scr)

    c = rq_ref.shape[0]
    lo = lax.broadcasted_iota(I32, (c, LANES), 1) < RET_DK
    rr = lax.broadcasted_iota(I32, (LANES, LANES), 0) < RET_DK
    cc = lax.broadcasted_iota(I32, (LANES, LANES), 1) < RET_DK
    diag_blocks = rr == cc
    outs = []
    for g in range(RET_HEADS // 2):
        gl = slice(g * LANES, (g + 1) * LANES)
        q2 = rq_ref[:, gl]
        k2 = rk_ref[:, gl]
        v2 = rv_ref[:, gl]
        zero = jnp.zeros_like(q2)
        sc = _dot_nt(jnp.concatenate([jnp.where(lo, q2, zero), jnp.where(lo, zero, q2)], axis=0), k2)
        pv_a = _dot((sc[:c] * decay_ref[2 * g]).astype(BF16), v2)
        pv_b = _dot((sc[c:] * decay_ref[2 * g + 1]).astype(BF16), v2)
        s2 = s_scr[g]
        o = jnp.where(lo, pv_a, pv_b) + _dot(q2, s2.astype(BF16)) * qdec_ref[:, gl]
        kd = (k2.astype(F32) * kdec_ref[:, gl]).astype(BF16)
        s_scr[g] = s2 * cdec_ref[:, gl] + jnp.where(diag_blocks, _dot_tn(kd, v2), 0.0)

        def head_mean(x):
            a = jnp.sum(jnp.where(lo, x, 0.0), axis=-1, keepdims=True)
            b = jnp.sum(jnp.where(lo, 0.0, x), axis=-1, keepdims=True)
            return jnp.where(lo, a, b) * (1.0 / RET_DV)

        mu = head_mean(o)
        var = head_mean(jnp.square(o - mu))
        outs.append((o - mu) * lax.rsqrt(var + EPS))
    gate = rg_ref[...]
    o_ref[...] = (jnp.concatenate(outs, axis=1) * gnw_ref[...] * (gate * _sigmoid(gate))).astype(BF16)

    @pl.when(ci == pl.num_programs(1) - 1)
    def _():
        for g in range(RET_HEADS // 2):
            s2 = s_scr[g]
            s_out_ref[2 * g] = s2[:RET_DK, :RET_DV]
            s_out_ref[2 * g + 1] = s2[RET_DK:, RET_DV:]


def _ret_tables(chunk):
    h = RET_HEADS
    log_g = jnp.log1p(-jnp.exp2(-5.0 - jnp.arange(h, dtype=F32)))
    idx = jnp.arange(chunk, dtype=F32)
    rel = idx[:, None] - idx[None, :]
    decay = jnp.where(rel[None] >= 0, jnp.exp(jnp.maximum(rel, 0.0)[None] * log_g[:, None, None]), 0.0)
    q_dec = jnp.exp((idx[:, None] + 1.0) * log_g[None, :])
    k_dec = jnp.exp((chunk - 1.0 - idx)[:, None] * log_g[None, :])
    c_dec = jnp.exp(chunk * log_g)
    rep = lambda t: jnp.repeat(t, RET_DK, axis=-1)
    return decay, rep(q_dec), rep(k_dec), rep(c_dec[None, :])


def _retention_prompt(rq, rk, rv, rg, gnw, b, l):
    chunk = next((c for c in (2 * RET_CHUNK, RET_CHUNK) if l % c == 0), l)
    decay, qdec, kdec, cdec = _ret_tables(chunk)
    hk = RET_HEADS * RET_DK
    n_chunks = l // chunk
    row = lambda bi, ci: (bi * n_chunks + ci, 0)
    const = lambda bi, ci: (0, 0)
    return pl.pallas_call(
        _ret_kernel,
        grid=(b, n_chunks),
        in_specs=[pl.BlockSpec((chunk, hk), row)] * 4 + [
            pl.BlockSpec((1, hk), const),
            pl.BlockSpec(decay.shape, lambda bi, ci: (0, 0, 0)),
            pl.BlockSpec((chunk, hk), const), pl.BlockSpec((chunk, hk), const),
            pl.BlockSpec((1, hk), const)],
        out_specs=[pl.BlockSpec((chunk, hk), row),
                   pl.BlockSpec((None, None, RET_HEADS, RET_DK, RET_DV), lambda bi, ci: (0, bi, 0, 0, 0))],
        out_shape=[jax.ShapeDtypeStruct((b * l, hk), BF16),
                   jax.ShapeDtypeStruct((1, b, RET_HEADS, RET_DK, RET_DV), F32)],
        scratch_shapes=[pltpu.VMEM((RET_HEADS // 2, 2 * RET_DK, 2 * RET_DV), F32)],
        compiler_params=_params("parallel", "arbitrary"),
        name="retention_prompt",
    )(rq, rk, rv, rg, gnw, decay, qdec, kdec, cdec)


def _ret_decode_kernel(q_ref, k_ref, v_ref, g_ref, gnw_ref, gam_ref, s_ref, o_ref, s_out_ref):
    gam = gam_ref[...]
    q = q_ref[...]
    k = k_ref[...]
    v = v_ref[...]
    qg = q * gam
    cross = jnp.zeros_like(v)
    for d in range(q.shape[0]):
        s_d = s_ref[d]
        cross = cross + qg[d:d + 1, :] * s_d
        s_out_ref[d] = s_d * gam + k[d:d + 1, :] * v
    o = jnp.sum(q * k, axis=0, keepdims=True) * v + cross
    mu = jnp.mean(o, axis=0, keepdims=True)
    var = jnp.mean(jnp.square(o - mu), axis=0, keepdims=True)
    g = g_ref[...]
    o_ref[...] = (o - mu) * lax.rsqrt(var + EPS) * gnw_ref[...] * (g * _sigmoid(g))


def _retention_decode(rq, rk, rv, rg, gnw, state):
    b = rq.shape[0]
    h, dk, dv = RET_HEADS, RET_DK, RET_DV
    gam = jnp.exp(jnp.log1p(-jnp.exp2(-5.0 - jnp.arange(h, dtype=F32)))).reshape(h, 1, 1)
    tr = lambda t: t.astype(F32).T
    vec = lambda d: pl.BlockSpec((d, b), lambda i: (i, 0))
    sspec = pl.BlockSpec((None, None, dk, dv, b), lambda i: (0, i, 0, 0, 0))
    o, s_new = pl.pallas_call(
        _ret_decode_kernel,
        grid=(h,),
        in_specs=[vec(dk), vec(dk), vec(dv), vec(dv), pl.BlockSpec((dv, 1), lambda i: (i, 0)),
                  pl.BlockSpec((None, 1, 1), lambda i: (i, 0, 0)), sspec],
        out_specs=[vec(dv), sspec],
        out_shape=[jax.ShapeDtypeStruct((h * dv, b), F32),
                   jax.ShapeDtypeStruct((1, h, dk, dv, b), F32)],
        compiler_params=_params("parallel"),
        name="retention_decode",
    )(tr(rq), tr(rk), tr(rv), tr(rg), gnw.reshape(h * dv, 1), gam, jnp.transpose(state, (0, 2, 3, 4, 1)))
    return o.T, jnp.transpose(s_new, (0, 4, 1, 2, 3))


def _attn_kernel(qlat_ref, qpe_ref, kcat_ref, wv_ref, o_ref, q_scr, m_scr, acc_scr, s_scr, p_scr, a_scr):
    i = pl.program_id(1)
    lane = lax.broadcasted_iota(I32, (Q_BLOCK, LANES), 1)
    per_group = LANES // MLA_ROPE
    parts = []
    for h in range(MLA_HEADS):
        g, r = divmod(h, per_group)
        pe = qpe_ref[:, g * LANES:(g + 1) * LANES]
        keep = (lane >= r * MLA_ROPE) & (lane < (r + 1) * MLA_ROPE)
        pe = jnp.where(keep, pe, jnp.zeros_like(pe))
        parts.append(jnp.concatenate([qlat_ref[:, h * KV_LORA:(h + 1) * KV_LORA], pe], axis=1))
    q_scr[...] = jnp.concatenate(parts, axis=0)
    rows = MLA_HEADS * Q_BLOCK
    m_scr[...] = jnp.full_like(m_scr, -jnp.inf)
    acc_scr[...] = jnp.zeros_like(acc_scr)
    ones = jnp.ones((KV_BLOCK, LANES), BF16)

    def keys(j):
        return kcat_ref[pl.ds(pl.multiple_of(j * KV_BLOCK, KV_BLOCK), KV_BLOCK), :]

    def scores(j, par):
        s_scr[par] = _dot_nt(q_scr[...], keys(j))

    def softmax(j, par, masked):
        s = s_scr[par]
        if masked:
            qpos = i * Q_BLOCK + lax.broadcasted_iota(I32, (rows, KV_BLOCK), 0) % Q_BLOCK
            kpos = j * KV_BLOCK + lax.broadcasted_iota(I32, (rows, KV_BLOCK), 1)
            s = jnp.where(kpos <= qpos, s, -jnp.inf)
        s0, s1 = s[:, :LANES], s[:, LANES:]
        m_old = m_scr[...]
        m_new = jnp.maximum(m_old, jnp.max(jnp.maximum(s0, s1), axis=-1, keepdims=True))
        a_scr[par] = jnp.exp(m_old - m_new)
        p_scr[par] = jnp.concatenate([jnp.exp(s0 - m_new), jnp.exp(s1 - m_new)], axis=1).astype(BF16)
        m_scr[...] = m_new

    def accumulate(j, par):
        alpha = a_scr[par]
        vext = jnp.concatenate([keys(j)[:, :KV_LORA], ones], axis=1)
        acc_scr[...] = jnp.concatenate([alpha, alpha], axis=1) * acc_scr[...] + _dot(p_scr[par], vext)

    def step(t, par):
        scores(t + 1, 1 - par)
        softmax(t, par, False)
        accumulate(t - 1, 1 - par)

    n_full = (i * Q_BLOCK) // KV_BLOCK
    scores(0, 0)

    @pl.when(n_full >= 1)
    def _():
        scores(1, 1)
        softmax(0, 0, False)

    def body(u, carry):
        step(2 * u + 1, 1)
        step(2 * u + 2, 0)
        return carry

    n_pairs = jnp.maximum(n_full - 1, 0) // 2
    lax.fori_loop(0, n_pairs, body, 0)
    even = n_full % 2 == 0

    @pl.when(even & (n_full >= 2))
    def _():
        step(n_full - 1, 1)

    @pl.when(even)
    def _():
        softmax(n_full, 0, True)

        @pl.when(n_full >= 1)
        def _():
            accumulate(n_full - 1, 1)

        accumulate(n_full, 0)

    @pl.when(jnp.logical_not(even))
    def _():
        softmax(n_full, 1, True)
        accumulate(n_full - 1, 0)
        accumulate(n_full, 1)

    o = acc_scr[:, :KV_LORA] / acc_scr[:, KV_LORA:]
    outs = [_dot(o[h * Q_BLOCK:(h + 1) * Q_BLOCK].astype(BF16), wv_ref[h]) for h in range(MLA_HEADS)]
    o_ref[...] = jnp.concatenate(outs, axis=1).astype(BF16)


def _attention_prompt(qlat, qpe, kcat, wv, b, l):
    nq = l // Q_BLOCK
    rows = MLA_HEADS * Q_BLOCK
    qmap = lambda bi, i: (bi * nq + i, 0)
    return pl.pallas_call(
        _attn_kernel,
        grid=(b, nq),
        in_specs=[pl.BlockSpec((Q_BLOCK, qlat.shape[1]), qmap),
                  pl.BlockSpec((Q_BLOCK, qpe.shape[1]), qmap),
                  pl.BlockSpec((l, kcat.shape[1]), lambda bi, i: (bi, 0)),
                  pl.BlockSpec(wv.shape, lambda bi, i: (0, 0, 0))],
        out_specs=pl.BlockSpec((Q_BLOCK, MLA_HEADS * MLA_V), qmap),
        out_shape=jax.ShapeDtypeStruct((b * l, MLA_HEADS * MLA_V), BF16),
        scratch_shapes=[pltpu.VMEM((rows, 2 * LANES), BF16), pltpu.VMEM((rows, LANES), F32),
                        pltpu.VMEM((rows, 2 * KV_LORA), F32), pltpu.VMEM((2, rows, KV_BLOCK), F32),
                        pltpu.VMEM((2, rows, KV_BLOCK), BF16), pltpu.VMEM((2, rows, LANES), F32)],
        compiler_params=_params("parallel", "arbitrary"),
        name="attention_prompt",
    )(qlat, qpe, kcat, wv)


def _attn_decode_kernel(pt_ref, ql_ref, qp_ref, cn_ref, kn_ref, lat_hbm, rope_hbm, o_ref,
                        kbuf, rbuf, sem):
    b = pl.program_id(0)
    nb = pl.num_programs(0)
    n_pages = pt_ref.shape[1]
    page = lat_hbm.shape[2]

    def copies(seq, slot, p):
        phys = pt_ref[seq, p]
        dst = pl.ds(p * page, page)
        return (pltpu.make_async_copy(lat_hbm.at[0, phys], kbuf.at[slot, dst], sem.at[slot, 0]),
                pltpu.make_async_copy(rope_hbm.at[0, phys], rbuf.at[slot, :, dst], sem.at[slot, 1]))

    def start_all(seq, slot):
        def body(p, carry):
            for cp in copies(seq, slot, p):
                cp.start()
            return carry
        lax.fori_loop(0, n_pages, body, 0, unroll=4)

    def wait_all(slot):
        pltpu.make_async_copy(kbuf.at[slot], kbuf.at[slot], sem.at[slot, 0]).wait()
        pltpu.make_async_copy(rbuf.at[slot], rbuf.at[slot], sem.at[slot, 1]).wait()

    slot = b % 2

    @pl.when(b == 0)
    def _():
        start_all(0, 0)

    @pl.when(b + 1 < nb)
    def _():
        start_all(b + 1, 1 - slot)

    wait_all(slot)
    ql = ql_ref[...]
    qp = qp_ref[...]
    kc = kbuf[slot].astype(BF16)
    kr = rbuf[slot].astype(BF16)
    s_past = _dot_nt(ql, kc) + _dot(qp, kr)
    cn = cn_ref[...].astype(BF16).astype(F32)
    kn = kn_ref[...].astype(BF16).astype(F32)
    s_new = (jnp.sum(ql.astype(F32) * cn, axis=-1, keepdims=True)
             + jnp.sum(qp.astype(F32) * kn, axis=-1, keepdims=True))
    m = jnp.maximum(jnp.max(s_past, axis=-1, keepdims=True), s_new)
    p_past = jnp.exp(s_past - m)
    p_new = jnp.exp(s_new - m)
    denom = jnp.sum(p_past, axis=-1, keepdims=True) + p_new
    o = _dot(p_past.astype(BF16), kc) + p_new.astype(BF16).astype(F32) * cn
    o_ref[...] = o / denom


def _attention_decode(qlat, qpe, c_new, kpe_new, lat_cache, rope_cache, page_table):
    b = qlat.shape[0]
    h = MLA_HEADS
    n_pages = page_table.shape[1]
    page = lat_cache.shape[2]
    past = n_pages * page
    blk = lambda w: pl.BlockSpec((None, h, w), lambda i, pt: (i, 0, 0))
    one = lambda w: pl.BlockSpec((None, 1, w), lambda i, pt: (i, 0, 0))
    grid_spec = pltpu.PrefetchScalarGridSpec(
        num_scalar_prefetch=1,
        grid=(b,),
        in_specs=[blk(KV_LORA), blk(MLA_ROPE), one(KV_LORA), one(MLA_ROPE),
                  pl.BlockSpec(memory_space=pl.ANY), pl.BlockSpec(memory_space=pl.ANY)],
        out_specs=blk(KV_LORA),
        scratch_shapes=[pltpu.VMEM((2, past, KV_LORA), F32), pltpu.VMEM((2, MLA_ROPE, past), F32),
                        pltpu.SemaphoreType.DMA((2, 2))],
    )
    return pl.pallas_call(
        _attn_decode_kernel,
        grid_spec=grid_spec,
        out_shape=jax.ShapeDtypeStruct((b, h, KV_LORA), F32),
        compiler_params=_params("arbitrary"),
        name="attention_decode",
    )(page_table, qlat.reshape(b, h, KV_LORA), qpe.reshape(b, h, MLA_ROPE),
      c_new.reshape(b, 1, KV_LORA), kpe_new.reshape(b, 1, MLA_ROPE), lat_cache,
      jnp.swapaxes(rope_cache, 2, 3))


def _route(logits):
    lane = lax.broadcasted_iota(I32, logits.shape, 1)
    lane_f = lane.astype(F32)
    neg = -jnp.inf
    big = float(LANES)
    gmask = lane < N_GROUPS
    gl = jnp.where(gmask, logits, neg)
    gmax = jnp.max(gl, axis=-1, keepdims=True)
    g_sel = jnp.min(jnp.where(gl == gmax, lane_f, big), axis=-1, keepdims=True)
    gsum = jnp.sum(jnp.where(gmask, jnp.exp(gl - gmax), 0.0), axis=-1, keepdims=True)
    g_w = 1.0 / gsum
    lo = N_GROUPS + EXPERTS_PER_GROUP * g_sel
    emask = (lane_f >= lo) & (lane_f < lo + EXPERTS_PER_GROUP)
    el = jnp.where(emask, logits, neg)
    v1 = jnp.max(el, axis=-1, keepdims=True)
    i1 = jnp.min(jnp.where(el == v1, lane_f, big), axis=-1, keepdims=True)
    el2 = jnp.where(lane_f == i1, neg, el)
    v2 = jnp.max(el2, axis=-1, keepdims=True)
    i2 = jnp.min(jnp.where(el2 == v2, lane_f, big), axis=-1, keepdims=True)
    e2 = jnp.exp(v2 - v1)
    w1 = g_w / (1.0 + e2)
    w2 = g_w * e2 / (1.0 + e2)
    return i1 - N_GROUPS, i2 - N_GROUPS, w1, w2


def _bf16_parts(w):
    hi = w.astype(BF16).astype(F32)
    mid = (w - hi).astype(BF16).astype(F32)
    return hi, mid, w - hi - mid


def _lane_pack(shape, cols):
    lane = lax.broadcasted_iota(I32, shape, 1)
    out = jnp.zeros(shape, F32)
    for k, col in enumerate(cols):
        out = jnp.where(lane == k, col, out)
    return out


def _mix_out_kernel(*refs, n_act, has_pre, has_bias):
    refs = list(refs)
    acts = [refs.pop(0) for _ in range(n_act)]
    ws = [refs.pop(0) for _ in range(n_act)]
    pre = refs.pop(0) if has_pre else None
    bias = refs.pop(0) if has_bias else None
    h_ref, nw_ref, rwh_ref, rwl_ref, rb_ref, h1_ref, xe_ref, rid_ref, cnt_ref = refs
    out = None
    for k, (a_ref, w_ref) in enumerate(zip(acts, ws)):
        a = a_ref[...]
        if has_pre and k == n_act - 1:
            a = _dot(a.astype(BF16), pre[...])
        t = _dot(a.astype(BF16), w_ref[...])
        out = t if out is None else out + t
    if has_bias:
        out = out + bias[...]
    h1 = h_ref[...] + out
    h1_ref[...] = h1
    xn = _rms(h1, nw_ref[...])
    xh = xn.astype(BF16)
    xl = (xn - xh.astype(F32)).astype(BF16)
    tm = xn.shape[0]
    prod = _dot(jnp.concatenate([xh, xl], axis=0), jnp.concatenate([rwh_ref[...], rwl_ref[...]], axis=1))
    logits = (prod[:tm, :LANES] + prod[:tm, LANES:]) + (prod[tm:, :LANES] + prod[tm:, LANES:]) + rb_ref[...]
    e1, e2, w1, w2 = _route(logits)
    meta = _lane_pack(logits.shape, (e1, e2) + _bf16_parts(w1) + _bf16_parts(w2))
    xe_ref[...] = jnp.concatenate([xn, meta], axis=1).astype(BF16)
    rid_ref[...] = _lane_pack(logits.shape, (e1, e2)).astype(I32)
    lane_f = lax.broadcasted_iota(I32, logits.shape, 1).astype(F32)
    hit = jnp.where((lane_f == e1) | (lane_f == e2), 1.0, 0.0)
    cnt_ref[...] = jnp.sum(hit, axis=0, keepdims=True)


def _mix_out(acts, ws, h, nw, rw, rb, tm, pre=None, bias=None):
    n, d = h.shape
    row = lambda i: (i, 0)
    const = lambda i: (0, 0)
    ins = list(acts) + list(ws)
    specs = [pl.BlockSpec((tm, a.shape[1]), row) for a in acts] + [pl.BlockSpec(w.shape, const) for w in ws]
    if pre is not None:
        ins.append(pre)
        specs.append(pl.BlockSpec(pre.shape, const))
    if bias is not None:
        ins.append(bias)
        specs.append(pl.BlockSpec(bias.shape, const))
    ins += [h, nw, rw[0], rw[1], rb]
    specs += [pl.BlockSpec((tm, d), row), pl.BlockSpec((1, d), const),
              pl.BlockSpec(rw[0].shape, const), pl.BlockSpec(rw[1].shape, const), pl.BlockSpec(rb.shape, const)]
    kern = functools.partial(_mix_out_kernel, n_act=len(acts), has_pre=pre is not None,
                             has_bias=bias is not None)
    outs = pl.pallas_call(
        kern,
        grid=(n // tm,),
        in_specs=specs,
        out_specs=[pl.BlockSpec((tm, d), row), pl.BlockSpec((tm, d + LANES), row),
                   pl.BlockSpec((tm, LANES), row), pl.BlockSpec((None, 1, LANES), lambda i: (i, 0, 0))],
        out_shape=[jax.ShapeDtypeStruct((n, d), F32), jax.ShapeDtypeStruct((n, d + LANES), BF16),
                   jax.ShapeDtypeStruct((n, LANES), I32), jax.ShapeDtypeStruct((n // tm, 1, LANES), F32)],
        compiler_params=_params("parallel"),
        name="mix_out_route",
    )(*ins)
    return tuple(outs)


SEG_ALIGN = 16
FFN_ROWS = 512
PERM_ROWS = 256


def _seg_bits(t):
    bits = []
    b = SEG_ALIGN
    while b <= t:
        bits.append(b)
        b *= 2
    return tuple(reversed(bits))


def _slot_cap(t):
    cap = 2 * t + N_EXPERTS * (SEG_ALIGN - 1)
    return -(-cap // PERM_ROWS) * PERM_ROWS


def _moe_meta(counts, tile_tokens):
    cnt = counts[:, :N_EXPERTS].astype(I32)
    ell = ((cnt + SEG_ALIGN - 1) // SEG_ALIGN) * SEG_ALIGN
    off = jnp.cumsum(ell, axis=1) - ell
    tot = jnp.sum(ell, axis=0)
    region = ((tot + FFN_ROWS - 1) // FFN_ROWS) * FFN_ROWS
    gend = jnp.cumsum(region)
    gbase = gend - region
    gseg = gbase[None, :] + jnp.cumsum(ell, axis=0) - ell
    n_pairs = 2 * sum(tile_tokens)
    n_rows_max = n_pairs + len(tile_tokens) * N_EXPERTS * (SEG_ALIGN - 1) + N_EXPERTS * (FFN_ROWS - 1)
    n_steps = -(-n_rows_max // FFN_ROWS)
    starts = jnp.arange(n_steps, dtype=I32) * FFN_ROWS
    step_expert = jnp.minimum(jnp.sum((gend[None, :] <= starts[:, None]).astype(I32), axis=1), N_EXPERTS - 1)
    n_used = (gend[-1] // FFN_ROWS).reshape(1)
    off_f = off.astype(F32)
    return dict(off=off, ell=ell, gseg=gseg, tail_start=gbase + tot, tail_len=region - tot, end=gend[-1:],
                step_expert=step_expert, n_used=n_used, n_rows=n_steps * FFN_ROWS,
                off_col=off_f[:, :, None], off_row=jnp.pad(off_f, ((0, 0), (0, LANES - N_EXPERTS)))[:, None, :])


def _chunk_copies(n, bits, make):
    for b in bits:
        hi = n & (-(2 * b))

        @pl.when((n & b) != 0)
        def _():
            make(hi, b)


def _segment_copies(n, make):
    big = 2 * SEG_ALIGN
    shift = big.bit_length() - 1
    n_big = lax.shift_right_logical(n, shift)

    def body(k, carry):
        make(lax.shift_left(k, shift), big)
        return carry

    lax.fori_loop(0, n_big, body, 0)

    @pl.when((n & SEG_ALIGN) != 0)
    def _():
        make(lax.shift_left(n_big, shift), SEG_ALIGN)


def _pair_slots_rows(ids_t, off_col, t):
    sub = lax.broadcasted_iota(I32, (N_EXPERTS, t), 0).astype(F32)
    e1 = jnp.where(sub == ids_t[0:1, :], 1.0, 0.0)
    e2 = jnp.where(sub == ids_t[1:2, :], 1.0, 0.0)
    before = jnp.where(lax.broadcasted_iota(I32, (t, t), 0) < lax.broadcasted_iota(I32, (t, t), 1), 1.0, 0.0)
    before = before.astype(BF16)
    cum1 = _dot(e1.astype(BF16), before)
    cum2 = _dot(e2.astype(BF16), before)
    cnt1 = jnp.sum(e1, axis=1, keepdims=True)
    slot0 = jnp.sum(e1 * (off_col + cum1), axis=0, keepdims=True)
    slot1 = jnp.sum(e2 * (off_col + cnt1 + cum2), axis=0, keepdims=True)
    return slot0, slot1


def _pair_slots_cols(ids, off_row, t):
    lane = lax.broadcasted_iota(I32, (t, LANES), 1).astype(F32)
    idf = ids.astype(F32)
    e1 = jnp.where(lane == idf[:, 0:1], 1.0, 0.0)
    e2 = jnp.where(lane == idf[:, 1:2], 1.0, 0.0)
    before = jnp.where(lax.broadcasted_iota(I32, (t, t), 1) < lax.broadcasted_iota(I32, (t, t), 0), 1.0, 0.0)
    before = before.astype(BF16)
    cum1 = _dot(before, e1.astype(BF16))
    cum2 = _dot(before, e2.astype(BF16))
    cnt1 = jnp.sum(e1, axis=0, keepdims=True)
    slot0 = jnp.sum(e1 * (off_row + cum1), axis=1, keepdims=True)
    slot1 = jnp.sum(e2 * (off_row + cnt1 + cum2), axis=1, keepdims=True)
    return slot0, slot1


def _permute_kernel(off_ref, ell_ref, gseg_ref, tstart_ref, tlen_ref, end_ref, xe_ref, ids_ref, offcol_ref, *rest,
                    tile0, zero_tiles):
    xg_hbm, xs_scr, zero_scr, sem = rest[-4:]
    s = pl.program_id(0)
    last = pl.num_programs(0) - 1
    t = xe_ref.shape[0]
    cap = xs_scr.shape[1]
    slot = s % 2

    def seg_start(tile, buf):
        def body(e, carry):
            o = off_ref[tile, e]
            g = gseg_ref[tile, e]

            def make(hi, b):
                src = xs_scr.at[buf, pl.ds(pl.multiple_of(o + hi, SEG_ALIGN), b)]
                dst = xg_hbm.at[pl.ds(pl.multiple_of(g + hi, SEG_ALIGN), b)]
                pltpu.make_async_copy(src, dst, sem.at[buf]).start()

            _segment_copies(ell_ref[tile, e], make)
            return carry

        lax.fori_loop(0, N_EXPERTS, body, 0)

    def seg_wait(tile, buf):
        def make(hi, b):
            pltpu.make_async_copy(xs_scr.at[buf, pl.ds(0, b)], xg_hbm.at[pl.ds(0, b)], sem.at[buf]).wait()

        _chunk_copies(off_ref[tile, N_EXPERTS - 1] + ell_ref[tile, N_EXPERTS - 1], _seg_bits(cap), make)

    if zero_tiles is not None:
        zrows = zero_scr.shape[0]

        def zero_fill(g, n, fn):
            def make(hi, b):
                dst = xg_hbm.at[pl.ds(pl.multiple_of(g + hi, SEG_ALIGN), b)]
                fn(pltpu.make_async_copy(zero_scr.at[pl.ds(0, b)], dst, sem.at[2]))

            _chunk_copies(n, _seg_bits(zrows), make)

        def zero_copies(fn):
            def body(e, carry):
                zero_fill(tstart_ref[e], tlen_ref[e], fn)
                for zt in zero_tiles:
                    zero_fill(gseg_ref[zt, e], ell_ref[zt, e], fn)
                return carry

            lax.fori_loop(0, N_EXPERTS, body, 0)

            def trailing(k, carry):
                dst = xg_hbm.at[pl.ds(pl.multiple_of(end_ref[0] + k * zrows, SEG_ALIGN), zrows)]
                fn(pltpu.make_async_copy(zero_scr, dst, sem.at[2]))
                return carry

            n_trailing = lax.shift_right_logical(xg_hbm.shape[0] - end_ref[0], zrows.bit_length() - 1)
            lax.fori_loop(0, n_trailing, trailing, 0)

        @pl.when(s == 0)
        def _():
            zero_scr[...] = jnp.zeros_like(zero_scr)
            zero_copies(lambda cp: cp.start())
            zero_copies(lambda cp: cp.wait())

    @pl.when(s >= 2)
    def _():
        seg_wait(tile0 + s - 2, slot)

    tile = tile0 + s
    ids_t = ids_ref[...].astype(F32).T
    slot0, slot1 = _pair_slots_rows(ids_t, offcol_ref[...], t)
    used = off_ref[tile, N_EXPERTS - 1] + ell_ref[tile, N_EXPERTS - 1]
    xe = xe_ref[...]
    for r0 in range(0, cap, PERM_ROWS):
        @pl.when(r0 < used)
        def _():
            rows = (lax.broadcasted_iota(I32, (PERM_ROWS, t), 0) + r0).astype(F32)
            pm = jnp.where((rows == slot0) | (rows == slot1), 1.0, 0.0).astype(BF16)
            xs_scr[slot, pl.ds(r0, PERM_ROWS), :] = _dot(pm, xe).astype(BF16)

    seg_start(tile, slot)

    @pl.when(s == last)
    def _():
        @pl.when(s >= 1)
        def _():
            seg_wait(tile - 1, 1 - slot)

        seg_wait(tile, slot)


def _moe_permute(meta, xe, ids, t, tile0, xg_prev=None):
    n, de = xe.shape
    cap = _slot_cap(t)
    n_tiles = meta["off"].shape[0]
    zero_tiles = None if xg_prev is not None else tuple(range(tile0 + n // t, n_tiles))
    prefetch = [meta["off"], meta["ell"], meta["gseg"], meta["tail_start"], meta["tail_len"], meta["end"]]
    imap = lambda s, *_: (s, 0)
    in_specs = [pl.BlockSpec((t, de), imap), pl.BlockSpec((t, LANES), imap),
                pl.BlockSpec((None, N_EXPERTS, 1), lambda s, *_: (tile0 + s, 0, 0))]
    ins = [xe, ids, meta["off_col"]]
    aliases = {}
    if xg_prev is not None:
        in_specs.append(pl.BlockSpec(memory_space=pl.ANY))
        ins.append(xg_prev)
        aliases = {len(prefetch) + len(ins) - 1: 0}
    grid_spec = pltpu.PrefetchScalarGridSpec(
        num_scalar_prefetch=len(prefetch),
        grid=(n // t,),
        in_specs=in_specs,
        out_specs=pl.BlockSpec(memory_space=pl.ANY),
        scratch_shapes=[pltpu.VMEM((2, cap, de), BF16), pltpu.VMEM((FFN_ROWS // 2, de), BF16),
                        pltpu.SemaphoreType.DMA((3,))],
    )
    return pl.pallas_call(
        functools.partial(_permute_kernel, tile0=tile0, zero_tiles=zero_tiles),
        grid_spec=grid_spec,
        out_shape=jax.ShapeDtypeStruct((meta["n_rows"], de), BF16),
        input_output_aliases=aliases,
        compiler_params=_params("arbitrary"),
        name="moe_permute",
    )(*prefetch, *ins)


def _ffn_kernel(te_ref, nu_ref, x_ref, wg_ref, wu_ref, wd_ref, o_ref, wg_b, wu_b, wd_b):
    i = pl.program_id(0)
    e = te_ref[i]
    d = o_ref.shape[1]

    @pl.when((i == 0) | (e != te_ref[jnp.maximum(i - 1, 0)]))
    def _():
        wg_b[...] = wg_ref[...].astype(BF16)
        wu_b[...] = wu_ref[...].astype(BF16)
        wd_b[...] = wd_ref[...].astype(BF16)

    @pl.when(i < nu_ref[0])
    def _():
        x = x_ref[:, :d]
        meta = x_ref[:, d:].astype(F32)
        w1 = meta[:, 2:3] + meta[:, 3:4] + meta[:, 4:5]
        w2 = meta[:, 5:6] + meta[:, 6:7] + meta[:, 7:8]
        w = jnp.where(meta[:, 0:1] == e.astype(F32), w1, w2)
        g = _dot(x, wg_b[...])
        u = _dot(x, wu_b[...])
        hmid = (g * _sigmoid(g) * u).astype(BF16)
        o_ref[...] = (_dot(hmid, wd_b[...]) * w).astype(BF16)

    @pl.when(i >= nu_ref[0])
    def _():
        o_ref[...] = jnp.zeros_like(o_ref)


def _moe_ffn(xs, tile_expert, n_used, wg, wu, wd, tr, layer):
    n_rows, d = xs.shape[0], wg.shape[2]
    de = wg.shape[3]
    wmap = lambda i, te, nu: (layer, te[i], 0, 0)
    grid_spec = pltpu.PrefetchScalarGridSpec(
        num_scalar_prefetch=2,
        grid=(n_rows // tr,),
        in_specs=[pl.BlockSpec((tr, xs.shape[1]), lambda i, te, nu: (i, 0)),
                  pl.BlockSpec((None, None, d, de), wmap),
                  pl.BlockSpec((None, None, d, de), wmap),
                  pl.BlockSpec((None, None, de, d), wmap)],
        out_specs=pl.BlockSpec((tr, d), lambda i, te, nu: (i, 0)),
        scratch_shapes=[pltpu.VMEM((d, de), BF16), pltpu.VMEM((d, de), BF16), pltpu.VMEM((de, d), BF16)],
    )
    return pl.pallas_call(
        _ffn_kernel,
        grid_spec=grid_spec,
        out_shape=jax.ShapeDtypeStruct((n_rows, d), BF16),
        compiler_params=_params("arbitrary"),
        name="moe_ffn",
    )(tile_expert, n_used, xs, wg, wu, wd)


def _unpermute_kernel(off_ref, ell_ref, gseg_ref, ids_ref, offrow_ref, h_ref, yg_hbm, *rest, tile0, tail):
    n_out = 2 if tail == "conv_in" else 1
    ple_refs, out_refs, (ys_scr, sem) = rest[:-n_out - 2], rest[-n_out - 2:-2], rest[-2:]
    s = pl.program_id(0)
    t = h_ref.shape[0]
    cap = ys_scr.shape[1]
    slot = s % 2

    def seg_start(tile, buf):
        def body(e, carry):
            o = off_ref[tile, e]
            g = gseg_ref[tile, e]

            def make(hi, b):
                src = yg_hbm.at[pl.ds(pl.multiple_of(g + hi, SEG_ALIGN), b)]
                dst = ys_scr.at[buf, pl.ds(pl.multiple_of(o + hi, SEG_ALIGN), b)]
                pltpu.make_async_copy(src, dst, sem.at[buf]).start()

            _segment_copies(ell_ref[tile, e], make)
            return carry

        lax.fori_loop(0, N_EXPERTS, body, 0)

    def seg_wait(tile, buf):
        def make(hi, b):
            pltpu.make_async_copy(yg_hbm.at[pl.ds(0, b)], ys_scr.at[buf, pl.ds(0, b)], sem.at[buf]).wait()

        _chunk_copies(off_ref[tile, N_EXPERTS - 1] + ell_ref[tile, N_EXPERTS - 1], _seg_bits(cap), make)

    tile = tile0 + s

    @pl.when(s == 0)
    def _():
        ys_scr[...] = jnp.zeros_like(ys_scr)
        seg_start(tile, slot)

    @pl.when(s + 1 < pl.num_programs(0))
    def _():
        seg_start(tile + 1, 1 - slot)

    seg_wait(tile, slot)
    slot0, slot1 = _pair_slots_cols(ids_ref[...], offrow_ref[...], t)
    cols = lax.broadcasted_iota(I32, (t, cap), 1).astype(F32)
    pu = jnp.where((cols == slot0) | (cols == slot1), 1.0, 0.0).astype(BF16)
    h = h_ref[...] + _dot(pu, ys_scr[slot])
    p_ref, proj_ref, pn_ref, gate_ref = ple_refs[:4]
    pp = _dot(p_ref[...].astype(BF16), proj_ref[...])
    g = _dot(_rms(h, pn_ref[...]).astype(BF16), gate_ref[...])
    h3 = h + pp * _sigmoid(g)
    if tail == "conv_in":
        nm_ref, pw1_ref, pb1_ref = ple_refs[4:]
        h3_ref, a_ref = out_refs
        h3_ref[...] = h3
        d = h3.shape[1]
        hn = _rms(h3, nm_ref[...]).astype(BF16)
        ua = _dot(hn, pw1_ref[:, :d]) + pb1_ref[:, :d]
        ub = _dot(hn, pw1_ref[:, d:]) + pb1_ref[:, d:]
        a_ref[...] = ua * _sigmoid(ub)
    else:
        (nf_ref,) = ple_refs[4:]
        (y_ref,) = out_refs
        y_ref[...] = _rms(h3, nf_ref[...])


def _moe_unpermute(meta, h, ids, yg, t, tile0, ple):
    n, d = h.shape
    p, layer, proj, pn, gate, tail, extra = ple
    cap = _slot_cap(t)
    prefetch = [meta["off"], meta["ell"], meta["gseg"]]
    imap = lambda s, *_: (s, 0)
    const = lambda s, *_: (0, 0)
    n_out = 2 if tail == "conv_in" else 1
    grid_spec = pltpu.PrefetchScalarGridSpec(
        num_scalar_prefetch=len(prefetch),
        grid=(n // t,),
        in_specs=[pl.BlockSpec((t, LANES), imap),
                  pl.BlockSpec((None, 1, LANES), lambda s, *_: (tile0 + s, 0, 0)),
                  pl.BlockSpec((t, d), imap), pl.BlockSpec(memory_space=pl.ANY),
                  pl.BlockSpec((None, t, p.shape[2]), lambda s, *_: (layer, s, 0)),
                  pl.BlockSpec(proj.shape, const), pl.BlockSpec((1, d), const), pl.BlockSpec(gate.shape, const)]
        + [pl.BlockSpec(e.shape, const) for e in extra],
        out_specs=[pl.BlockSpec((t, d), imap)] * n_out,
        scratch_shapes=[pltpu.VMEM((2, cap, d), BF16), pltpu.SemaphoreType.DMA((2,))],
    )
    return pl.pallas_call(
        functools.partial(_unpermute_kernel, tile0=tile0, tail=tail),
        grid_spec=grid_spec,
        out_shape=[jax.ShapeDtypeStruct((n, d), F32)] * n_out,
        compiler_params=_params("arbitrary"),
        name="moe_unpermute_" + tail,
    )(*prefetch, ids, meta["off_row"], h, yg, p, proj, pn, gate, *extra)


def _moe(groups, ples, wg, wu, wd, layer):
    tiles = [t for g in groups for t in [g[4]] * (g[0].shape[0] // g[4])]
    meta = _moe_meta(jnp.concatenate([g[3].reshape(-1, LANES) for g in groups], axis=0), tiles)
    xg = None
    tile0 = 0
    for h1, xe, ids, _, t in groups:
        xg = _moe_permute(meta, xe, ids, t, tile0, xg)
        tile0 += h1.shape[0] // t
    yg = _moe_ffn(xg, meta["step_expert"], meta["n_used"], wg, wu, wd, FFN_ROWS, layer)
    outs = []
    tile0 = 0
    for (h1, xe, ids, _, t), ple in zip(groups, ples):
        outs.append(_moe_unpermute(meta, h1, ids, yg, t, tile0, ple))
        tile0 += h1.shape[0] // t
    return outs


def _ln_swish(y, w, b):
    mu = jnp.mean(y, axis=-1, keepdims=True)
    var = jnp.mean(jnp.square(y - mu), axis=-1, keepdims=True)
    z = (y - mu) * lax.rsqrt(var + EPS) * w + b
    return z * _sigmoid(z)


def _conv_kernel(a_ref, dw_ref, dwb_ref, lnw_ref, lnb_ref, o_ref, ext, shifted, acc):
    t = pl.program_id(1)
    tl = a_ref.shape[0]
    halo = ext.shape[0] - tl
    sub = 8

    @pl.when(t == 0)
    def _():
        ext[0:halo, :] = jnp.zeros((halo, ext.shape[1]), F32)

    ext[halo:, :] = a_ref[...]
    first = halo - (CONV_WIDTH - 1)
    srows = shifted.shape[1]
    for c0 in range(0, ext.shape[1], LANES):
        cs = slice(c0, c0 + LANES)
        for r in range(1, sub):
            shifted[r - 1] = ext[r:r + srows, cs]
        s = None
        for j in range(CONV_WIDTH):
            r = (first + j) % sub
            base = first + j - r
            win = ext[base:base + tl, cs] if r == 0 else shifted[r - 1, base:base + tl, :]
            term = win * dw_ref[j:j + 1, cs]
            s = term if s is None else s + term
        acc[:, cs] = s
    ext[0:halo, :] = ext[tl:tl + halo, :]
    o_ref[...] = _ln_swish(acc[...] + dwb_ref[...], lnw_ref[...], lnb_ref[...]).astype(BF16)


def _conv_prompt(a, dw, dwb, lnw, lnb, b, l, tl=256):
    d = a.shape[1]
    nt = l // tl
    halo = 32
    row = lambda bi, t: (bi * nt + t, 0)
    const = lambda bi, t: (0, 0)
    return pl.pallas_call(
        _conv_kernel,
        grid=(b, nt),
        in_specs=[pl.BlockSpec((tl, d), row), pl.BlockSpec(dw.shape, const),
                  pl.BlockSpec((1, d), const), pl.BlockSpec((1, d), const), pl.BlockSpec((1, d), const)],
        out_specs=pl.BlockSpec((tl, d), row),
        out_shape=jax.ShapeDtypeStruct((b * l, d), BF16),
        scratch_shapes=[pltpu.VMEM((tl + halo, d), F32), pltpu.VMEM((7, tl + halo - 8, LANES), F32),
                        pltpu.VMEM((tl, d), F32)],
        compiler_params=_params("parallel", "arbitrary"),
        name="conv_prompt",
    )(a, dw, dwb, lnw, lnb)


def _conv_decode_kernel(buf_ref, a_ref, dw_ref, dwb_ref, lnw_ref, lnb_ref, y_ref, new_ref):
    w = CONV_WIDTH - 1
    a = a_ref[...]
    y = a * dw_ref[w:w + 1, :]
    for j in range(w):
        y = y + buf_ref[j] * dw_ref[j:j + 1, :]
    y_ref[...] = _ln_swish(y + dwb_ref[...], lnw_ref[...], lnb_ref[...])
    for j in range(w - 1):
        new_ref[j] = buf_ref[j + 1]
    new_ref[w - 1] = a


def _conv_decode(state, a, dw, dwb, lnw, lnb, bb=8):
    _, b, w, d = state.shape
    row = pl.BlockSpec((bb, d), lambda i: (i, 0))
    sspec = pl.BlockSpec((None, w, bb, d), lambda i: (0, 0, i, 0))
    const = lambda i: (0, 0)
    y, new = pl.pallas_call(
        _conv_decode_kernel,
        grid=(b // bb,),
        in_specs=[sspec, row, pl.BlockSpec(dw.shape, const), pl.BlockSpec((1, d), const),
                  pl.BlockSpec((1, d), const), pl.BlockSpec((1, d), const)],
        out_specs=[row, sspec],
        out_shape=[jax.ShapeDtypeStruct((b, d), F32), jax.ShapeDtypeStruct((1, w, b, d), F32)],
        compiler_params=_params("parallel"),
        name="conv_decode",
    )(jnp.transpose(state, (0, 2, 1, 3)), a, dw, dwb, lnw, lnb)
    return y, jnp.transpose(new, (0, 2, 1, 3))


def _prep_weights(W):
    P = {}
    w_in = W["w_in"][0]
    kpe_cols = w_in[:, -MLA_ROPE:]
    P["w_in"] = jnp.concatenate([w_in[:, :-MLA_ROPE]] + [kpe_cols] * (LANES // MLA_ROPE), axis=1).astype(BF16)
    w_uq = W["w_uq"][0]
    nope = jnp.pad(w_uq[..., :MLA_NOPE], ((0, 0), (0, 0), (0, LANES - MLA_NOPE)))
    P["w_uq_nope"] = nope.reshape(Q_LORA, MLA_HEADS * LANES).astype(BF16)
    P["w_uq_pe"] = w_uq[..., MLA_NOPE:].reshape(Q_LORA, MLA_HEADS * MLA_ROPE).astype(BF16)
    w_ukv = W["w_ukv"][0]
    wk = jnp.transpose(w_ukv[..., :MLA_NOPE], (1, 2, 0))
    P["wk"] = jnp.pad(wk, ((0, 0), (0, LANES - MLA_NOPE), (0, 0))).astype(BF16)
    wv = jnp.transpose(w_ukv[..., MLA_NOPE:], (1, 0, 2))
    P["wv"] = wv.astype(BF16)
    eye = jnp.eye(MLA_HEADS, dtype=F32)
    P["wv_bd"] = (wv[:, :, None, :] * eye[:, None, :, None]).reshape(
        MLA_HEADS * KV_LORA, MLA_HEADS * MLA_V).astype(BF16)
    w_out = W["w_out"][0].astype(BF16)
    P["w_out_r"] = w_out[:RET_HEADS * RET_DV]
    P["w_out_m"] = w_out[RET_HEADS * RET_DV:]
    d = w_out.shape[1]
    for i in range(W["moe_w_group"].shape[0]):
        rw = jnp.concatenate([W["moe_w_group"][i], W["moe_w_router"][i].reshape(d, N_EXPERTS)], axis=1)
        rb = jnp.concatenate([W["moe_b_group"][i], W["moe_b_router"][i].reshape(N_EXPERTS)])
        padc = LANES - rw.shape[1]
        rw = jnp.pad(rw, ((0, 0), (0, padc)))
        rw_hi = rw.astype(BF16)
        P["router_w", i] = (rw_hi, (rw - rw_hi.astype(F32)).astype(BF16))
        P["router_b", i] = jnp.pad(rb, (0, padc)).reshape(1, LANES)
        P["ple_proj", i] = W["ple_proj"][i].astype(BF16)
        P["ple_gate", i] = W["ple_gate"][i].astype(BF16)
    P["pw1"] = W["conv_pw1"][0].astype(BF16)
    P["pw2"] = W["conv_pw2"][0].astype(BF16)
    P["dw"] = jnp.pad(W["conv_dw"][0], ((0, 1), (0, 0)))
    return P


def _vec(v):
    return v.reshape(1, -1)


def _mixer0(x, pos, W, P, tm, past):
    b, l, d = x.shape
    n = b * l
    h = x.reshape(n, d)
    tabs = _rope_tables(pos, RET_DK) + _rope_tables(pos, MLA_ROPE)
    rq, rk, rv, rg, qlat, qpe, c, kpe, kcat = _proj_in(
        h, _vec(W["norm_mix"][0]), P["w_in"], tabs, _vec(W["mla_q_norm"][0]), _vec(W["mla_kv_norm"][0]),
        P["w_uq_nope"], P["w_uq_pe"], P["wk"], tm)
    gnw = _vec(W["ret_gn_w"][0])
    if past is None:
        o_r, s_new = _retention_prompt(rq, rk, rv, rg, gnw, b, l)
        o_m = _attention_prompt(qlat, qpe, kcat, P["wv"], b, l)
        pre = None
    else:
        ret_state, lat_cache, rope_cache, page_table = past
        o_r, s_new = _retention_decode(rq, rk, rv, rg, gnw, ret_state)
        o_m = _attention_decode(qlat, qpe, c, kpe, lat_cache, rope_cache, page_table)
        o_m = o_m.reshape(n, MLA_HEADS * KV_LORA)
        pre = P["wv_bd"]
    routed = _mix_out([o_r, o_m], [P["w_out_r"], P["w_out_m"]], h, _vec(W["norm_ffn"][0]),
                      P["router_w", 0], P["router_b", 0], tm, pre=pre)
    return routed, (c.reshape(1, b, l, KV_LORA), kpe.reshape(1, b, l, MLA_ROPE), s_new)


def _ple_operands(p, layer, W, P):
    n = p.shape[1] * p.shape[2]
    if layer == 0:
        tail, extra = "conv_in", [_vec(W["norm_mix"][1]), P["pw1"], _vec(W["conv_pw1_b"][0])]
    else:
        tail, extra = "final", [_vec(W["norm_final"])]
    return (p.reshape(p.shape[0], n, -1), layer, P["ple_proj", layer], _vec(W["ple_norm"][layer]),
            P["ple_gate", layer], tail, extra)


def _mixer1(h3, a, W, P, tm, b, l, conv_state):
    n, d = h3.shape
    conv_vecs = (_vec(W["conv_dw_b"][0]), _vec(W["conv_ln_w"][0]), _vec(W["conv_ln_b"][0]))
    if conv_state is None:
        yact = _conv_prompt(a, P["dw"], *conv_vecs, b, l)
        new_state = a.reshape(b, l, d)[:, l - (CONV_WIDTH - 1):, :][None]
    else:
        yact, new_state = _conv_decode(conv_state, a, P["dw"], *conv_vecs)
    routed = _mix_out([yact], [P["pw2"]], h3, _vec(W["norm_ffn"][1]),
                      P["router_w", 1], P["router_b", 1], tm, bias=_vec(W["conv_pw2_b"][0]))
    return routed, new_state


def kernel(x_prompt, x_sample, p_prompt, p_sample, cache_mla_latent, cache_mla_rope, state_retention, state_conv,
           page_table, norm_mix, norm_ffn, norm_final, w_in, ret_gn_w, mla_q_norm, mla_kv_norm, w_uq, w_ukv, w_out,
           conv_pw1, conv_pw1_b, conv_dw, conv_dw_b, conv_ln_w, conv_ln_b, conv_pw2, conv_pw2_b,
           moe_w_group, moe_b_group, moe_w_router, moe_b_router, moe_w_gate, moe_w_up, moe_w_down,
           ple_proj, ple_norm, ple_gate):
    W = dict(norm_mix=norm_mix, norm_ffn=norm_ffn, norm_final=norm_final, w_in=w_in, ret_gn_w=ret_gn_w,
             mla_q_norm=mla_q_norm, mla_kv_norm=mla_kv_norm, w_uq=w_uq, w_ukv=w_ukv, w_out=w_out,
             conv_pw1=conv_pw1, conv_pw1_b=conv_pw1_b, conv_dw=conv_dw, conv_dw_b=conv_dw_b, conv_ln_w=conv_ln_w,
             conv_ln_b=conv_ln_b, conv_pw2=conv_pw2, conv_pw2_b=conv_pw2_b, moe_w_group=moe_w_group,
             moe_b_group=moe_b_group, moe_w_router=moe_w_router, moe_b_router=moe_b_router, moe_w_gate=moe_w_gate,
             moe_w_up=moe_w_up, moe_w_down=moe_w_down, ple_proj=ple_proj, ple_norm=ple_norm, ple_gate=ple_gate)
    assert w_in.shape[0] == 1 and conv_pw1.shape[0] == 1, "one retention/attention layer, one conv layer"
    P = _prep_weights(W)
    past_len = page_table.shape[1] * cache_mla_latent.shape[2]
    pos_prompt = jnp.arange(x_prompt.shape[1])
    pos_sample = past_len + jnp.arange(x_sample.shape[1])
    bp, lp, _ = x_prompt.shape
    bs, ls, _ = x_sample.shape
    tm_p = min(512, lp)
    tm_s = bs * ls
    moe_w = lambda i: (moe_w_gate, moe_w_up, moe_w_down, i)
    routed_p, (lat_p, rope_p, ret_p) = _mixer0(x_prompt, pos_prompt, W, P, tm_p, None)
    routed_s, (lat_s, rope_s, ret_s) = _mixer0(
        x_sample, pos_sample, W, P, tm_s, (state_retention, cache_mla_latent, cache_mla_rope, page_table))
    ples = lambda layer: [_ple_operands(p_prompt, layer, W, P), _ple_operands(p_sample, layer, W, P)]
    (h3_p, a_p), (h3_s, a_s) = _moe([routed_p + (tm_p,), routed_s + (tm_s,)], ples(0), *moe_w(0))
    routed_p, conv_p = _mixer1(h3_p, a_p, W, P, tm_p, bp, lp, None)
    routed_s, conv_s = _mixer1(h3_s, a_s, W, P, tm_s, bs, ls, state_conv)
    (y_p,), (y_s,) = _moe([routed_p + (tm_p,), routed_s + (tm_s,)], ples(1), *moe_w(1))
    return (y_p.reshape(x_prompt.shape), y_s.reshape(x_sample.shape), lat_p, rope_p, ret_p, conv_p,
            lat_s, rope_s, ret_s, conv_s)
```

```python
import functools

import jax
import jax.numpy as jnp
from jax import lax
from jax.experimental import pallas as pl
from jax.experimental.pallas import tpu as pltpu

F32 = jnp.float32
BF16 = jnp.bfloat16
I32 = jnp.int32

EPS = 1e-6
ROPE_THETA = 10000.0
RET_HEADS = 8
RET_DK = 64
RET_DV = 64
RET_CHUNK = 128
MLA_HEADS = 8
MLA_NOPE = 64
MLA_ROPE = 32
MLA_V = 64
Q_LORA = 256
KV_LORA = 128
CONV_WIDTH = 31
N_GROUPS = 4
EXPERTS_PER_GROUP = 8
N_EXPERTS = N_GROUPS * EXPERTS_PER_GROUP
LANES = 128
Q_BLOCK = 256
KV_BLOCK = 256
VMEM_LIMIT_BYTES = 56 * 1024 * 1024


def _params(*sem):
    return pltpu.CompilerParams(dimension_semantics=sem, vmem_limit_bytes=VMEM_LIMIT_BYTES)


def _rms(x, w):
    return x * lax.rsqrt(jnp.mean(x * x, axis=-1, keepdims=True) + EPS) * w


def _sigmoid(x):
    return 1.0 / (1.0 + jnp.exp(-x))


def _dot(a, b):
    return jnp.dot(a, b, preferred_element_type=F32)


def _dot_nt(a, b):
    return lax.dot_general(a, b, (((1,), (1,)), ((), ())), preferred_element_type=F32)


def _dot_tn(a, b):
    return lax.dot_general(a, b, (((0,), (0,)), ((), ())), preferred_element_type=F32)


def _rope(x, cos_t, sin_t, period):
    half = period // 2
    lane = lax.broadcasted_iota(I32, x.shape, 1)
    first = (lane % period) < half
    rot = jnp.where(first, pltpu.roll(x, LANES - half, 1), pltpu.roll(x, half, 1))
    return x * cos_t + rot * sin_t


def _rope_tables(pos, period):
    half = period // 2
    inv = ROPE_THETA ** (-2.0 * jnp.arange(half, dtype=F32) / period)
    ang = pos.astype(F32)[:, None] * inv[None, :]
    cos = jnp.cos(ang)
    sin = jnp.sin(ang)
    cos_p = jnp.concatenate([cos, cos], axis=1)
    sin_p = jnp.concatenate([-sin, sin], axis=1)
    reps = LANES // period
    return jnp.tile(cos_p, (1, reps)), jnp.tile(sin_p, (1, reps))


def _proj_in_kernel(x_ref, nw_ref, win_ref, cr_ref, sr_ref, cp_ref, sp_ref, qn_ref, kvn_ref,
                    wuqn_ref, wuqp_ref, wk_ref,
                    rq_ref, rk_ref, rv_ref, rg_ref, qlat_ref, qpe_ref, c_ref, kpe_ref, kcat_ref):
    hn = _rms(x_ref[...], nw_ref[...]).astype(BF16)
    cr, sr, cp, sp = cr_ref[...], sr_ref[...], cp_ref[...], sp_ref[...]
    hk = RET_HEADS * RET_DK
    hv = RET_HEADS * RET_DV

    def mm(lo, width):
        return _dot(hn, win_ref[:, lo:lo + width])

    u = mm(0, hk)
    rq_ref[...] = jnp.concatenate(
        [_rope(u[:, g:g + LANES], cr, sr, RET_DK) for g in range(0, hk, LANES)], axis=1).astype(BF16)
    u = mm(hk, hk)
    rk_ref[...] = jnp.concatenate(
        [_rope(u[:, g:g + LANES], cr, sr, RET_DK) * (RET_DK ** -0.5) for g in range(0, hk, LANES)],
        axis=1).astype(BF16)
    rv_ref[...] = mm(2 * hk, hv).astype(BF16)
    rg_ref[...] = mm(2 * hk + hv, hv)
    off = 2 * hk + 2 * hv
    cqn = _rms(mm(off, Q_LORA), qn_ref[...]).astype(BF16)
    c = _rms(mm(off + Q_LORA, KV_LORA), kvn_ref[...])
    c_ref[...] = c
    kp = _rope(mm(off + Q_LORA + KV_LORA, LANES), cp, sp, MLA_ROPE)
    kpe_ref[...] = kp[:, :MLA_ROPE]
    kcat_ref[...] = jnp.concatenate([c, kp], axis=1).astype(BF16)
    scale = (MLA_NOPE + MLA_ROPE) ** -0.5
    qn = _dot(cqn, wuqn_ref[...])
    for h in range(MLA_HEADS):
        ql = _dot(qn[:, h * LANES:(h + 1) * LANES].astype(BF16), wk_ref[h]) * scale
        qlat_ref[:, h * KV_LORA:(h + 1) * KV_LORA] = ql.astype(BF16)
    qp = _dot(cqn, wuqp_ref[...])
    qpe_ref[...] = jnp.concatenate(
        [_rope(qp[:, g:g + LANES], cp, sp, MLA_ROPE) * scale
         for g in range(0, MLA_HEADS * MLA_ROPE, LANES)], axis=1).astype(BF16)


def _proj_in(x2, nw, win, tabs, qn, kvn, wuqn, wuqp, wk, tm):
    n, d = x2.shape
    cr, sr, cp, sp = tabs
    ltab = cr.shape[0]
    nblk = ltab // tm if ltab >= tm else 0
    tab_rows = tm if nblk else ltab
    tab_map = (lambda i: (i % nblk, 0)) if nblk else (lambda i: (0, 0))
    row = lambda i: (i, 0)
    const = lambda i: (0, 0)
    hk = RET_HEADS * RET_DK
    hv = RET_HEADS * RET_DV
    tab_spec = pl.BlockSpec((tab_rows, LANES), tab_map)
    out_cols = [(hk, BF16), (hk, BF16), (hv, BF16), (hv, F32), (MLA_HEADS * KV_LORA, BF16),
                (MLA_HEADS * MLA_ROPE, BF16), (KV_LORA, F32), (MLA_ROPE, F32), (2 * LANES, BF16)]
    return pl.pallas_call(
        _proj_in_kernel,
        grid=(n // tm,),
        in_specs=[pl.BlockSpec((tm, d), row), pl.BlockSpec((1, d), const),
                  pl.BlockSpec(win.shape, const), tab_spec, tab_spec, tab_spec, tab_spec,
                  pl.BlockSpec((1, Q_LORA), const), pl.BlockSpec((1, KV_LORA), const),
                  pl.BlockSpec(wuqn.shape, const), pl.BlockSpec(wuqp.shape, const),
                  pl.BlockSpec(wk.shape, lambda i: (0, 0, 0))],
        out_specs=[pl.BlockSpec((tm, w), row) for w, _ in out_cols],
        out_shape=[jax.ShapeDtypeStruct((n, w), dt) for w, dt in out_cols],
        compiler_params=_params("parallel"),
        name="proj_in",
    )(x2, nw, win, cr, sr, cp, sp, qn, kvn, wuqn, wuqp, wk)


def _ret_kernel(rq_ref, rk_ref, rv_ref, rg_ref, gnw_ref, decay_ref, qdec_ref, kdec_ref, cdec_ref,
                o_ref, s_out_ref, s_scr):
    ci = pl.program_id(1)

    @pl.when(ci == 0)
    def _():
        s_scr[...] = jnp.zeros_like(s_scr)

    c = rq_ref.shape[0]
    lo = lax.broadcasted_iota(I32, (c, LANES), 1) < RET_DK
    rr = lax.broadcasted_iota(I32, (LANES, LANES), 0) < RET_DK
    cc = lax.broadcasted_iota(I32, (LANES, LANES), 1) < RET_DK
    diag_blocks = rr == cc
    outs = []
    for g in range(RET_HEADS // 2):
        gl = slice(g * LANES, (g + 1) * LANES)
        q2 = rq_ref[:, gl]
        k2 = rk_ref[:, gl]
        v2 = rv_ref[:, gl]
        zero = jnp.zeros_like(q2)
        sc = _dot_nt(jnp.concatenate([jnp.where(lo, q2, zero), jnp.where(lo, zero, q2)], axis=0), k2)
        pv_a = _dot((sc[:c] * decay_ref[2 * g]).astype(BF16), v2)
        pv_b = _dot((sc[c:] * decay_ref[2 * g + 1]).astype(BF16), v2)
        s2 = s_scr[g]
        o = jnp.where(lo, pv_a, pv_b) + _dot(q2, s2.astype(BF16)) * qdec_ref[:, gl]
        kd = (k2.astype(F32) * kdec_ref[:, gl]).astype(BF16)
        s_scr[g] = s2 * cdec_ref[:, gl] + jnp.where(diag_blocks, _dot_tn(kd, v2), 0.0)

        def head_mean(x):
            a = jnp.sum(jnp.where(lo, x, 0.0), axis=-1, keepdims=True)
            b = jnp.sum(jnp.where(lo, 0.0, x), axis=-1, keepdims=True)
            return jnp.where(lo, a, b) * (1.0 / RET_DV)

        mu = head_mean(o)
        var = head_mean(jnp.square(o - mu))
        outs.append((o - mu) * lax.rsqrt(var + EPS))
    gate = rg_ref[...]
    o_ref[...] = (jnp.concatenate(outs, axis=1) * gnw_ref[...] * (gate * _sigmoid(gate))).astype(BF16)

    @pl.when(ci == pl.num_programs(1) - 1)
    def _():
        for g in range(RET_HEADS // 2):
            s2 = s_scr[g]
            s_out_ref[2 * g] = s2[:RET_DK, :RET_DV]
            s_out_ref[2 * g + 1] = s2[RET_DK:, RET_DV:]


def _ret_tables(chunk):
    h = RET_HEADS
    log_g = jnp.log1p(-jnp.exp2(-5.0 - jnp.arange(h, dtype=F32)))
    idx = jnp.arange(chunk, dtype=F32)
    rel = idx[:, None] - idx[None, :]
    decay = jnp.where(rel[None] >= 0, jnp.exp(jnp.maximum(rel, 0.0)[None] * log_g[:, None, None]), 0.0)
    q_dec = jnp.exp((idx[:, None] + 1.0) * log_g[None, :])
    k_dec = jnp.exp((chunk - 1.0 - idx)[:, None] * log_g[None, :])
    c_dec = jnp.exp(chunk * log_g)
    rep = lambda t: jnp.repeat(t, RET_DK, axis=-1)
    return decay, rep(q_dec), rep(k_dec), rep(c_dec[None, :])


def _retention_prompt(rq, rk, rv, rg, gnw, b, l):
    chunk = next((c for c in (2 * RET_CHUNK, RET_CHUNK) if l % c == 0), l)
    decay, qdec, kdec, cdec = _ret_tables(chunk)
    hk = RET_HEADS * RET_DK
    n_chunks = l // chunk
    row = lambda bi, ci: (bi * n_chunks + ci, 0)
    const = lambda bi, ci: (0, 0)
    return pl.pallas_call(
        _ret_kernel,
        grid=(b, n_chunks),
        in_specs=[pl.BlockSpec((chunk, hk), row)] * 4 + [
            pl.BlockSpec((1, hk), const),
            pl.BlockSpec(decay.shape, lambda bi, ci: (0, 0, 0)),
            pl.BlockSpec((chunk, hk), const), pl.BlockSpec((chunk, hk), const),
            pl.BlockSpec((1, hk), const)],
        out_specs=[pl.BlockSpec((chunk, hk), row),
                   pl.BlockSpec((None, None, RET_HEADS, RET_DK, RET_DV), lambda bi, ci: (0, bi, 0, 0, 0))],
        out_shape=[jax.ShapeDtypeStruct((b * l, hk), BF16),
                   jax.ShapeDtypeStruct((1, b, RET_HEADS, RET_DK, RET_DV), F32)],
        scratch_shapes=[pltpu.VMEM((RET_HEADS // 2, 2 * RET_DK, 2 * RET_DV), F32)],
        compiler_params=_params("parallel", "arbitrary"),
        name="retention_prompt",
    )(rq, rk, rv, rg, gnw, decay, qdec, kdec, cdec)


def _ret_decode_kernel(q_ref, k_ref, v_ref, g_ref, gnw_ref, gam_ref, s_ref, o_ref, s_out_ref):
    gam = gam_ref[...]
    q = q_ref[...]
    k = k_ref[...]
    v = v_ref[...]
    qg = q * gam
    cross = jnp.zeros_like(v)
    for d in range(q.shape[0]):
        s_d = s_ref[d]
        cross = cross + qg[d:d + 1, :] * s_d
        s_out_ref[d] = s_d * gam + k[d:d + 1, :] * v
    o = jnp.sum(q * k, axis=0, keepdims=True) * v + cross
    mu = jnp.mean(o, axis=0, keepdims=True)
    var = jnp.mean(jnp.square(o - mu), axis=0, keepdims=True)
    g = g_ref[...]
    o_ref[...] = (o - mu) * lax.rsqrt(var + EPS) * gnw_ref[...] * (g * _sigmoid(g))


def _retention_decode(rq, rk, rv, rg, gnw, state):
    b = rq.shape[0]
    h, dk, dv = RET_HEADS, RET_DK, RET_DV
    gam = jnp.exp(jnp.log1p(-jnp.exp2(-5.0 - jnp.arange(h, dtype=F32)))).reshape(h, 1, 1)
    tr = lambda t: t.astype(F32).T
    vec = lambda d: pl.BlockSpec((d, b), lambda i: (i, 0))
    sspec = pl.BlockSpec((None, None, dk, dv, b), lambda i: (0, i, 0, 0, 0))
    o, s_new = pl.pallas_call(
        _ret_decode_kernel,
        grid=(h,),
        in_specs=[vec(dk), vec(dk), vec(dv), vec(dv), pl.BlockSpec((dv, 1), lambda i: (i, 0)),
                  pl.BlockSpec((None, 1, 1), lambda i: (i, 0, 0)), sspec],
        out_specs=[vec(dv), sspec],
        out_shape=[jax.ShapeDtypeStruct((h * dv, b), F32),
                   jax.ShapeDtypeStruct((1, h, dk, dv, b), F32)],
        compiler_params=_params("parallel"),
        name="retention_decode",
    )(tr(rq), tr(rk), tr(rv), tr(rg), gnw.reshape(h * dv, 1), gam, jnp.transpose(state, (0, 2, 3, 4, 1)))
    return o.T, jnp.transpose(s_new, (0, 4, 1, 2, 3))


def _attn_kernel(qlat_ref, qpe_ref, kcat_ref, wv_ref, o_ref, q_scr, m_scr, acc_scr, s_scr, p_scr, a_scr):
    i = pl.program_id(1)
    lane = lax.broadcasted_iota(I32, (Q_BLOCK, LANES), 1)
    per_group = LANES // MLA_ROPE
    parts = []
    for h in range(MLA_HEADS):
        g, r = divmod(h, per_group)
        pe = qpe_ref[:, g * LANES:(g + 1) * LANES]
        keep = (lane >= r * MLA_ROPE) & (lane < (r + 1) * MLA_ROPE)
        pe = jnp.where(keep, pe, jnp.zeros_like(pe))
        parts.append(jnp.concatenate([qlat_ref[:, h * KV_LORA:(h + 1) * KV_LORA], pe], axis=1))
    q_scr[...] = jnp.concatenate(parts, axis=0)
    rows = MLA_HEADS * Q_BLOCK
    m_scr[...] = jnp.full_like(m_scr, -jnp.inf)
    acc_scr[...] = jnp.zeros_like(acc_scr)
    ones = jnp.ones((KV_BLOCK, LANES), BF16)

    def keys(j):
        return kcat_ref[pl.ds(pl.multiple_of(j * KV_BLOCK, KV_BLOCK), KV_BLOCK), :]

    def scores(j, par):
        s_scr[par] = _dot_nt(q_scr[...], keys(j))

    def softmax(j, par, masked):
        s = s_scr[par]
        if masked:
            qpos = i * Q_BLOCK + lax.broadcasted_iota(I32, (rows, KV_BLOCK), 0) % Q_BLOCK
            kpos = j * KV_BLOCK + lax.broadcasted_iota(I32, (rows, KV_BLOCK), 1)
            s = jnp.where(kpos <= qpos, s, -jnp.inf)
        s0, s1 = s[:, :LANES], s[:, LANES:]
        m_old = m_scr[...]
        m_new = jnp.maximum(m_old, jnp.max(jnp.maximum(s0, s1), axis=-1, keepdims=True))
        a_scr[par] = jnp.exp(m_old - m_new)
        p_scr[par] = jnp.concatenate([jnp.exp(s0 - m_new), jnp.exp(s1 - m_new)], axis=1).astype(BF16)
        m_scr[...] = m_new

    def accumulate(j, par):
        alpha = a_scr[par]
        vext = jnp.concatenate([keys(j)[:, :KV_LORA], ones], axis=1)
        acc_scr[...] = jnp.concatenate([alpha, alpha], axis=1) * acc_scr[...] + _dot(p_scr[par], vext)

    def step(t, par):
        scores(t + 1, 1 - par)
        softmax(t, par, False)
        accumulate(t - 1, 1 - par)

    n_full = (i * Q_BLOCK) // KV_BLOCK
    scores(0, 0)

    @pl.when(n_full >= 1)
    def _():
        scores(1, 1)
        softmax(0, 0, False)

    def body(u, carry):
        step(2 * u + 1, 1)
        step(2 * u + 2, 0)
        return carry

    n_pairs = jnp.maximum(n_full - 1, 0) // 2
    lax.fori_loop(0, n_pairs, body, 0)
    even = n_full % 2 == 0

    @pl.when(even & (n_full >= 2))
    def _():
        step(n_full - 1, 1)

    @pl.when(even)
    def _():
        softmax(n_full, 0, True)

        @pl.when(n_full >= 1)
        def _():
            accumulate(n_full - 1, 1)

        accumulate(n_full, 0)

    @pl.when(jnp.logical_not(even))
    def _():
        softmax(n_full, 1, True)
        accumulate(n_full - 1, 0)
        accumulate(n_full, 1)

    o = acc_scr[:, :KV_LORA] / acc_scr[:, KV_LORA:]
    outs = [_dot(o[h * Q_BLOCK:(h + 1) * Q_BLOCK].astype(BF16), wv_ref[h]) for h in range(MLA_HEADS)]
    o_ref[...] = jnp.concatenate(outs, axis=1).astype(BF16)


def _attention_prompt(qlat, qpe, kcat, wv, b, l):
    nq = l // Q_BLOCK
    rows = MLA_HEADS * Q_BLOCK
    qmap = lambda bi, i: (bi * nq + i, 0)
    return pl.pallas_call(
        _attn_kernel,
        grid=(b, nq),
        in_specs=[pl.BlockSpec((Q_BLOCK, qlat.shape[1]), qmap),
                  pl.BlockSpec((Q_BLOCK, qpe.shape[1]), qmap),
                  pl.BlockSpec((l, kcat.shape[1]), lambda bi, i: (bi, 0)),
                  pl.BlockSpec(wv.shape, lambda bi, i: (0, 0, 0))],
        out_specs=pl.BlockSpec((Q_BLOCK, MLA_HEADS * MLA_V), qmap),
        out_shape=jax.ShapeDtypeStruct((b * l, MLA_HEADS * MLA_V), BF16),
        scratch_shapes=[pltpu.VMEM((rows, 2 * LANES), BF16), pltpu.VMEM((rows, LANES), F32),
                        pltpu.VMEM((rows, 2 * KV_LORA), F32), pltpu.VMEM((2, rows, KV_BLOCK), F32),
                        pltpu.VMEM((2, rows, KV_BLOCK), BF16), pltpu.VMEM((2, rows, LANES), F32)],
        compiler_params=_params("parallel", "arbitrary"),
        name="attention_prompt",
    )(qlat, qpe, kcat, wv)


def _attn_decode_kernel(pt_ref, ql_ref, qp_ref, cn_ref, kn_ref, lat_hbm, rope_hbm, o_ref,
                        kbuf, rbuf, sem):
    b = pl.program_id(0)
    nb = pl.num_programs(0)
    n_pages = pt_ref.shape[1]
    page = lat_hbm.shape[2]

    def copies(seq, slot, p):
        phys = pt_ref[seq, p]
        dst = pl.ds(p * page, page)
        return (pltpu.make_async_copy(lat_hbm.at[0, phys], kbuf.at[slot, dst], sem.at[slot, 0]),
                pltpu.make_async_copy(rope_hbm.at[0, phys], rbuf.at[slot, :, dst], sem.at[slot, 1]))

    def start_all(seq, slot):
        def body(p, carry):
            for cp in copies(seq, slot, p):
                cp.start()
            return carry
        lax.fori_loop(0, n_pages, body, 0, unroll=4)

    def wait_all(slot):
        pltpu.make_async_copy(kbuf.at[slot], kbuf.at[slot], sem.at[slot, 0]).wait()
        pltpu.make_async_copy(rbuf.at[slot], rbuf.at[slot], sem.at[slot, 1]).wait()

    slot = b % 2

    @pl.when(b == 0)
    def _():
        start_all(0, 0)

    @pl.when(b + 1 < nb)
    def _():
        start_all(b + 1, 1 - slot)

    wait_all(slot)
    ql = ql_ref[...]
    qp = qp_ref[...]
    kc = kbuf[slot].astype(BF16)
    kr = rbuf[slot].astype(BF16)
    s_past = _dot_nt(ql, kc) + _dot(qp, kr)
    cn = cn_ref[...].astype(BF16).astype(F32)
    kn = kn_ref[...].astype(BF16).astype(F32)
    s_new = (jnp.sum(ql.astype(F32) * cn, axis=-1, keepdims=True)
             + jnp.sum(qp.astype(F32) * kn, axis=-1, keepdims=True))
    m = jnp.maximum(jnp.max(s_past, axis=-1, keepdims=True), s_new)
    p_past = jnp.exp(s_past - m)
    p_new = jnp.exp(s_new - m)
    denom = jnp.sum(p_past, axis=-1, keepdims=True) + p_new
    o = _dot(p_past.astype(BF16), kc) + p_new.astype(BF16).astype(F32) * cn
    o_ref[...] = o / denom


def _attention_decode(qlat, qpe, c_new, kpe_new, lat_cache, rope_cache, page_table):
    b = qlat.shape[0]
    h = MLA_HEADS
    n_pages = page_table.shape[1]
    page = lat_cache.shape[2]
    past = n_pages * page
    blk = lambda w: pl.BlockSpec((None, h, w), lambda i, pt: (i, 0, 0))
    one = lambda w: pl.BlockSpec((None, 1, w), lambda i, pt: (i, 0, 0))
    grid_spec = pltpu.PrefetchScalarGridSpec(
        num_scalar_prefetch=1,
        grid=(b,),
        in_specs=[blk(KV_LORA), blk(MLA_ROPE), one(KV_LORA), one(MLA_ROPE),
                  pl.BlockSpec(memory_space=pl.ANY), pl.BlockSpec(memory_space=pl.ANY)],
        out_specs=blk(KV_LORA),
        scratch_shapes=[pltpu.VMEM((2, past, KV_LORA), F32), pltpu.VMEM((2, MLA_ROPE, past), F32),
                        pltpu.SemaphoreType.DMA((2, 2))],
    )
    return pl.pallas_call(
        _attn_decode_kernel,
        grid_spec=grid_spec,
        out_shape=jax.ShapeDtypeStruct((b, h, KV_LORA), F32),
        compiler_params=_params("arbitrary"),
        name="attention_decode",
    )(page_table, qlat.reshape(b, h, KV_LORA), qpe.reshape(b, h, MLA_ROPE),
      c_new.reshape(b, 1, KV_LORA), kpe_new.reshape(b, 1, MLA_ROPE), lat_cache,
      jnp.swapaxes(rope_cache, 2, 3))


def _route(logits):
    lane = lax.broadcasted_iota(I32, logits.shape, 1)
    lane_f = lane.astype(F32)
    neg = -jnp.inf
    big = float(LANES)
    gmask = lane < N_GROUPS
    gl = jnp.where(gmask, logits, neg)
    gmax = jnp.max(gl, axis=-1, keepdims=True)
    g_sel = jnp.min(jnp.where(gl == gmax, lane_f, big), axis=-1, keepdims=True)
    gsum = jnp.sum(jnp.where(gmask, jnp.exp(gl - gmax), 0.0), axis=-1, keepdims=True)
    g_w = 1.0 / gsum
    lo = N_GROUPS + EXPERTS_PER_GROUP * g_sel
    emask = (lane_f >= lo) & (lane_f < lo + EXPERTS_PER_GROUP)
    el = jnp.where(emask, logits, neg)
    v1 = jnp.max(el, axis=-1, keepdims=True)
    i1 = jnp.min(jnp.where(el == v1, lane_f, big), axis=-1, keepdims=True)
    el2 = jnp.where(lane_f == i1, neg, el)
    v2 = jnp.max(el2, axis=-1, keepdims=True)
    i2 = jnp.min(jnp.where(el2 == v2, lane_f, big), axis=-1, keepdims=True)
    e2 = jnp.exp(v2 - v1)
    w1 = g_w / (1.0 + e2)
    w2 = g_w * e2 / (1.0 + e2)
    return i1 - N_GROUPS, i2 - N_GROUPS, w1, w2


def _bf16_parts(w):
    hi = w.astype(BF16).astype(F32)
    mid = (w - hi).astype(BF16).astype(F32)
    return hi, mid, w - hi - mid


def _lane_pack(shape, cols):
    lane = lax.broadcasted_iota(I32, shape, 1)
    out = jnp.zeros(shape, F32)
    for k, col in enumerate(cols):
        out = jnp.where(lane == k, col, out)
    return out


def _mix_out_kernel(*refs, n_act, has_pre, has_bias):
    refs = list(refs)
    acts = [refs.pop(0) for _ in range(n_act)]
    ws = [refs.pop(0) for _ in range(n_act)]
    pre = refs.pop(0) if has_pre else None
    bias = refs.pop(0) if has_bias else None
    h_ref, nw_ref, rwh_ref, rwl_ref, rb_ref, h1_ref, xe_ref, rid_ref, cnt_ref = refs
    out = None
    for k, (a_ref, w_ref) in enumerate(zip(acts, ws)):
        a = a_ref[...]
        if has_pre and k == n_act - 1:
            a = _dot(a.astype(BF16), pre[...])
        t = _dot(a.astype(BF16), w_ref[...])
        out = t if out is None else out + t
    if has_bias:
        out = out + bias[...]
    h1 = h_ref[...] + out
    h1_ref[...] = h1
    xn = _rms(h1, nw_ref[...])
    xh = xn.astype(BF16)
    xl = (xn - xh.astype(F32)).astype(BF16)
    tm = xn.shape[0]
    prod = _dot(jnp.concatenate([xh, xl], axis=0), jnp.concatenate([rwh_ref[...], rwl_ref[...]], axis=1))
    logits = (prod[:tm, :LANES] + prod[:tm, LANES:]) + (prod[tm:, :LANES] + prod[tm:, LANES:]) + rb_ref[...]
    e1, e2, w1, w2 = _route(logits)
    meta = _lane_pack(logits.shape, (e1, e2) + _bf16_parts(w1) + _bf16_parts(w2))
    xe_ref[...] = jnp.concatenate([xn, meta], axis=1).astype(BF16)
    rid_ref[...] = _lane_pack(logits.shape, (e1, e2)).astype(I32)
    lane_f = lax.broadcasted_iota(I32, logits.shape, 1).astype(F32)
    hit = jnp.where((lane_f == e1) | (lane_f == e2), 1.0, 0.0)
    cnt_ref[...] = jnp.sum(hit, axis=0, keepdims=True)


def _mix_out(acts, ws, h, nw, rw, rb, tm, pre=None, bias=None):
    n, d = h.shape
    row = lambda i: (i, 0)
    const = lambda i: (0, 0)
    ins = list(acts) + list(ws)
    specs = [pl.BlockSpec((tm, a.shape[1]), row) for a in acts] + [pl.BlockSpec(w.shape, const) for w in ws]
    if pre is not None:
        ins.append(pre)
        specs.append(pl.BlockSpec(pre.shape, const))
    if bias is not None:
        ins.append(bias)
        specs.append(pl.BlockSpec(bias.shape, const))
    ins += [h, nw, rw[0], rw[1], rb]
    specs += [pl.BlockSpec((tm, d), row), pl.BlockSpec((1, d), const),
              pl.BlockSpec(rw[0].shape, const), pl.BlockSpec(rw[1].shape, const), pl.BlockSpec(rb.shape, const)]
    kern = functools.partial(_mix_out_kernel, n_act=len(acts), has_pre=pre is not None,
                             has_bias=bias is not None)
    outs = pl.pallas_call(
        kern,
        grid=(n // tm,),
        in_specs=specs,
        out_specs=[pl.BlockSpec((tm, d), row), pl.BlockSpec((tm, d + LANES), row),
                   pl.BlockSpec((tm, LANES), row), pl.BlockSpec((None, 1, LANES), lambda i: (i, 0, 0))],
        out_shape=[jax.ShapeDtypeStruct((n, d), F32), jax.ShapeDtypeStruct((n, d + LANES), BF16),
                   jax.ShapeDtypeStruct((n, LANES), I32), jax.ShapeDtypeStruct((n // tm, 1, LANES), F32)],
        compiler_params=_params("parallel"),
        name="mix_out_route",
    )(*ins)
    return tuple(outs)


SEG_ALIGN = 16
FFN_ROWS = 512
PERM_ROWS = 256


def _seg_bits(t):
    bits = []
    b = SEG_ALIGN
    while b <= t:
        bits.append(b)
        b *= 2
    return tuple(reversed(bits))


def _slot_cap(t):
    cap = 2 * t + N_EXPERTS * (SEG_ALIGN - 1)
    return -(-cap // PERM_ROWS) * PERM_ROWS


def _moe_meta(counts, tile_tokens):
    cnt = counts[:, :N_EXPERTS].astype(I32)
    ell = ((cnt + SEG_ALIGN - 1) // SEG_ALIGN) * SEG_ALIGN
    off = jnp.cumsum(ell, axis=1) - ell
    tot = jnp.sum(ell, axis=0)
    region = ((tot + FFN_ROWS - 1) // FFN_ROWS) * FFN_ROWS
    gend = jnp.cumsum(region)
    gbase = gend - region
    gseg = gbase[None, :] + jnp.cumsum(ell, axis=0) - ell
    n_pairs = 2 * sum(tile_tokens)
    n_rows_max = n_pairs + len(tile_tokens) * N_EXPERTS * (SEG_ALIGN - 1) + N_EXPERTS * (FFN_ROWS - 1)
    n_steps = -(-n_rows_max // FFN_ROWS)
    starts = jnp.arange(n_steps, dtype=I32) * FFN_ROWS
    step_expert = jnp.minimum(jnp.sum((gend[None, :] <= starts[:, None]).astype(I32), axis=1), N_EXPERTS - 1)
    n_used = (gend[-1] // FFN_ROWS).reshape(1)
    off_f = off.astype(F32)
    return dict(off=off, ell=ell, gseg=gseg, tail_start=gbase + tot, tail_len=region - tot, end=gend[-1:],
                step_expert=step_expert, n_used=n_used, n_rows=n_steps * FFN_ROWS,
                off_col=off_f[:, :, None], off_row=jnp.pad(off_f, ((0, 0), (0, LANES - N_EXPERTS)))[:, None, :])


def _chunk_copies(n, bits, make):
    for b in bits:
        hi = n & (-(2 * b))

        @pl.when((n & b) != 0)
        def _():
            make(hi, b)


def _segment_copies(n, make):
    big = 2 * SEG_ALIGN
    shift = big.bit_length() - 1
    n_big = lax.shift_right_logical(n, shift)

    def body(k, carry):
        make(lax.shift_left(k, shift), big)
        return carry

    lax.fori_loop(0, n_big, body, 0)

    @pl.when((n & SEG_ALIGN) != 0)
    def _():
        make(lax.shift_left(n_big, shift), SEG_ALIGN)


def _pair_slots_rows(ids_t, off_col, t):
    sub = lax.broadcasted_iota(I32, (N_EXPERTS, t), 0).astype(F32)
    e1 = jnp.where(sub == ids_t[0:1, :], 1.0, 0.0)
    e2 = jnp.where(sub == ids_t[1:2, :], 1.0, 0.0)
    before = jnp.where(lax.broadcasted_iota(I32, (t, t), 0) < lax.broadcasted_iota(I32, (t, t), 1), 1.0, 0.0)
    before = before.astype(BF16)
    cum1 = _dot(e1.astype(BF16), before)
    cum2 = _dot(e2.astype(BF16), before)
    cnt1 = jnp.sum(e1, axis=1, keepdims=True)
    slot0 = jnp.sum(e1 * (off_col + cum1), axis=0, keepdims=True)
    slot1 = jnp.sum(e2 * (off_col + cnt1 + cum2), axis=0, keepdims=True)
    return slot0, slot1


def _pair_slots_cols(ids, off_row, t):
    lane = lax.broadcasted_iota(I32, (t, LANES), 1).astype(F32)
    idf = ids.astype(F32)
    e1 = jnp.where(lane == idf[:, 0:1], 1.0, 0.0)
    e2 = jnp.where(lane == idf[:, 1:2], 1.0, 0.0)
    before = jnp.where(lax.broadcasted_iota(I32, (t, t), 1) < lax.broadcasted_iota(I32, (t, t), 0), 1.0, 0.0)
    before = before.astype(BF16)
    cum1 = _dot(before, e1.astype(BF16))
    cum2 = _dot(before, e2.astype(BF16))
    cnt1 = jnp.sum(e1, axis=0, keepdims=True)
    slot0 = jnp.sum(e1 * (off_row + cum1), axis=1, keepdims=True)
    slot1 = jnp.sum(e2 * (off_row + cnt1 + cum2), axis=1, keepdims=True)
    return slot0, slot1


def _permute_kernel(off_ref, ell_ref, gseg_ref, tstart_ref, tlen_ref, end_ref, xe_ref, ids_ref, offcol_ref, *rest,
                    tile0, zero_tiles):
    xg_hbm, xs_scr, zero_scr, sem = rest[-4:]
    s = pl.program_id(0)
    last = pl.num_programs(0) - 1
    t = xe_ref.shape[0]
    cap = xs_scr.shape[1]
    slot = s % 2

    def seg_start(tile, buf):
        def body(e, carry):
            o = off_ref[tile, e]
            g = gseg_ref[tile, e]

            def make(hi, b):
                src = xs_scr.at[buf, pl.ds(pl.multiple_of(o + hi, SEG_ALIGN), b)]
                dst = xg_hbm.at[pl.ds(pl.multiple_of(g + hi, SEG_ALIGN), b)]
                pltpu.make_async_copy(src, dst, sem.at[buf]).start()

            _segment_copies(ell_ref[tile, e], make)
            return carry

        lax.fori_loop(0, N_EXPERTS, body, 0)

    def seg_wait(tile, buf):
        def make(hi, b):
            pltpu.make_async_copy(xs_scr.at[buf, pl.ds(0, b)], xg_hbm.at[pl.ds(0, b)], sem.at[buf]).wait()

        _chunk_copies(off_ref[tile, N_EXPERTS - 1] + ell_ref[tile, N_EXPERTS - 1], _seg_bits(cap), make)

    if zero_tiles is not None:
        zrows = zero_scr.shape[0]

        def zero_fill(g, n, fn):
            def make(hi, b):
                dst = xg_hbm.at[pl.ds(pl.multiple_of(g + hi, SEG_ALIGN), b)]
                fn(pltpu.make_async_copy(zero_scr.at[pl.ds(0, b)], dst, sem.at[2]))

            _chunk_copies(n, _seg_bits(zrows), make)

        def zero_copies(fn):
            def body(e, carry):
                zero_fill(tstart_ref[e], tlen_ref[e], fn)
                for zt in zero_tiles:
                    zero_fill(gseg_ref[zt, e], ell_ref[zt, e], fn)
                return carry

            lax.fori_loop(0, N_EXPERTS, body, 0)

            def trailing(k, carry):
                dst = xg_hbm.at[pl.ds(pl.multiple_of(end_ref[0] + k * zrows, SEG_ALIGN), zrows)]
                fn(pltpu.make_async_copy(zero_scr, dst, sem.at[2]))
                return carry

            n_trailing = lax.shift_right_logical(xg_hbm.shape[0] - end_ref[0], zrows.bit_length() - 1)
            lax.fori_loop(0, n_trailing, trailing, 0)

        @pl.when(s == 0)
        def _():
            zero_scr[...] = jnp.zeros_like(zero_scr)
            zero_copies(lambda cp: cp.start())
            zero_copies(lambda cp: cp.wait())

    @pl.when(s >= 2)
    def _():
        seg_wait(tile0 + s - 2, slot)

    tile = tile0 + s
    ids_t = ids_ref[...].astype(F32).T
    slot0, slot1 = _pair_slots_rows(ids_t, offcol_ref[...], t)
    used = off_ref[tile, N_EXPERTS - 1] + ell_ref[tile, N_EXPERTS - 1]
    xe = xe_ref[...]
    for r0 in range(0, cap, PERM_ROWS):
        @pl.when(r0 < used)
        def _():
            rows = (lax.broadcasted_iota(I32, (PERM_ROWS, t), 0) + r0).astype(F32)
            pm = jnp.where((rows == slot0) | (rows == slot1), 1.0, 0.0).astype(BF16)
            xs_scr[slot, pl.ds(r0, PERM_ROWS), :] = _dot(pm, xe).astype(BF16)

    seg_start(tile, slot)

    @pl.when(s == last)
    def _():
        @pl.when(s >= 1)
        def _():
            seg_wait(tile - 1, 1 - slot)

        seg_wait(tile, slot)


def _moe_permute(meta, xe, ids, t, tile0, xg_prev=None):
    n, de = xe.shape
    cap = _slot_cap(t)
    n_tiles = meta["off"].shape[0]
    zero_tiles = None if xg_prev is not None else tuple(range(tile0 + n // t, n_tiles))
    prefetch = [meta["off"], meta["ell"], meta["gseg"], meta["tail_start"], meta["tail_len"], meta["end"]]
    imap = lambda s, *_: (s, 0)
    in_specs = [pl.BlockSpec((t, de), imap), pl.BlockSpec((t, LANES), imap),
                pl.BlockSpec((None, N_EXPERTS, 1), lambda s, *_: (tile0 + s, 0, 0))]
    ins = [xe, ids, meta["off_col"]]
    aliases = {}
    if xg_prev is not None:
        in_specs.append(pl.BlockSpec(memory_space=pl.ANY))
        ins.append(xg_prev)
        aliases = {len(prefetch) + len(ins) - 1: 0}
    grid_spec = pltpu.PrefetchScalarGridSpec(
        num_scalar_prefetch=len(prefetch),
        grid=(n // t,),
        in_specs=in_specs,
        out_specs=pl.BlockSpec(memory_space=pl.ANY),
        scratch_shapes=[pltpu.VMEM((2, cap, de), BF16), pltpu.VMEM((FFN_ROWS // 2, de), BF16),
                        pltpu.SemaphoreType.DMA((3,))],
    )
    return pl.pallas_call(
        functools.partial(_permute_kernel, tile0=tile0, zero_tiles=zero_tiles),
        grid_spec=grid_spec,
        out_shape=jax.ShapeDtypeStruct((meta["n_rows"], de), BF16),
        input_output_aliases=aliases,
        compiler_params=_params("arbitrary"),
        name="moe_permute",
    )(*prefetch, *ins)


def _ffn_kernel(te_ref, nu_ref, x_ref, wg_ref, wu_ref, wd_ref, o_ref, wg_b, wu_b, wd_b):
    i = pl.program_id(0)
    e = te_ref[i]
    d = o_ref.shape[1]

    @pl.when((i == 0) | (e != te_ref[jnp.maximum(i - 1, 0)]))
    def _():
        wg_b[...] = wg_ref[...].astype(BF16)
        wu_b[...] = wu_ref[...].astype(BF16)
        wd_b[...] = wd_ref[...].astype(BF16)

    @pl.when(i < nu_ref[0])
    def _():
        x = x_ref[:, :d]
        meta = x_ref[:, d:].astype(F32)
        w1 = meta[:, 2:3] + meta[:, 3:4] + meta[:, 4:5]
        w2 = meta[:, 5:6] + meta[:, 6:7] + meta[:, 7:8]
        w = jnp.where(meta[:, 0:1] == e.astype(F32), w1, w2)
        g = _dot(x, wg_b[...])
        u = _dot(x, wu_b[...])
        hmid = (g * _sigmoid(g) * u).astype(BF16)
        o_ref[...] = (_dot(hmid, wd_b[...]) * w).astype(BF16)

    @pl.when(i >= nu_ref[0])
    def _():
        o_ref[...] = jnp.zeros_like(o_ref)


def _moe_ffn(xs, tile_expert, n_used, wg, wu, wd, tr, layer):
    n_rows, d = xs.shape[0], wg.shape[2]
    de = wg.shape[3]
    wmap = lambda i, te, nu: (layer, te[i], 0, 0)
    grid_spec = pltpu.PrefetchScalarGridSpec(
        num_scalar_prefetch=2,
        grid=(n_rows // tr,),
        in_specs=[pl.BlockSpec((tr, xs.shape[1]), lambda i, te, nu: (jnp.maximum(jnp.minimum(i, nu[0] - 1), 0), 0)),
                  pl.BlockSpec((None, None, d, de), wmap),
                  pl.BlockSpec((None, None, d, de), wmap),
                  pl.BlockSpec((None, None, de, d), wmap)],
        out_specs=pl.BlockSpec((tr, d), lambda i, te, nu: (i, 0)),
        scratch_shapes=[pltpu.VMEM((d, de), BF16), pltpu.VMEM((d, de), BF16), pltpu.VMEM((de, d), BF16)],
    )
    return pl.pallas_call(
        _ffn_kernel,
        grid_spec=grid_spec,
        out_shape=jax.ShapeDtypeStruct((n_rows, d), BF16),
        compiler_params=_params("arbitrary"),
        name="moe_ffn",
    )(tile_expert, n_used, xs, wg, wu, wd)


def _unpermute_kernel(off_ref, ell_ref, gseg_ref, ids_ref, offrow_ref, h_ref, yg_hbm, *rest, tile0, tail):
    n_out = 2 if tail == "conv_in" else 1
    ple_refs, out_refs, (ys_scr, sem) = rest[:-n_out - 2], rest[-n_out - 2:-2], rest[-2:]
    s = pl.program_id(0)
    t = h_ref.shape[0]
    cap = ys_scr.shape[1]
    slot = s % 2

    def seg_start(tile, buf):
        def body(e, carry):
            o = off_ref[tile, e]
            g = gseg_ref[tile, e]

            def make(hi, b):
                src = yg_hbm.at[pl.ds(pl.multiple_of(g + hi, SEG_ALIGN), b)]
                dst = ys_scr.at[buf, pl.ds(pl.multiple_of(o + hi, SEG_ALIGN), b)]
                pltpu.make_async_copy(src, dst, sem.at[buf]).start()

            _segment_copies(ell_ref[tile, e], make)
            return carry

        lax.fori_loop(0, N_EXPERTS, body, 0)

    def seg_wait(tile, buf):
        def make(hi, b):
            pltpu.make_async_copy(yg_hbm.at[pl.ds(0, b)], ys_scr.at[buf, pl.ds(0, b)], sem.at[buf]).wait()

        _chunk_copies(off_ref[tile, N_EXPERTS - 1] + ell_ref[tile, N_EXPERTS - 1], _seg_bits(cap), make)

    tile = tile0 + s

    @pl.when(s == 0)
    def _():
        ys_scr[...] = jnp.zeros_like(ys_scr)
        seg_start(tile, slot)

    @pl.when(s + 1 < pl.num_programs(0))
    def _():
        seg_start(tile + 1, 1 - slot)

    seg_wait(tile, slot)
    slot0, slot1 = _pair_slots_cols(ids_ref[...], offrow_ref[...], t)
    cols = lax.broadcasted_iota(I32, (t, cap), 1).astype(F32)
    pu = jnp.where((cols == slot0) | (cols == slot1), 1.0, 0.0).astype(BF16)
    h = h_ref[...] + _dot(pu, ys_scr[slot])
    p_ref, proj_ref, pn_ref, gate_ref = ple_refs[:4]
    pp = _dot(p_ref[...].astype(BF16), proj_ref[...])
    g = _dot(_rms(h, pn_ref[...]).astype(BF16), gate_ref[...])
    h3 = h + pp * _sigmoid(g)
    if tail == "conv_in":
        nm_ref, pw1_ref, pb1_ref = ple_refs[4:]
        h3_ref, a_ref = out_refs
        h3_ref[...] = h3
        d = h3.shape[1]
        hn = _rms(h3, nm_ref[...]).astype(BF16)
        ua = _dot(hn, pw1_ref[:, :d]) + pb1_ref[:, :d]
        ub = _dot(hn, pw1_ref[:, d:]) + pb1_ref[:, d:]
        a_ref[...] = ua * _sigmoid(ub)
    else:
        (nf_ref,) = ple_refs[4:]
        (y_ref,) = out_refs
        y_ref[...] = _rms(h3, nf_ref[...])


def _moe_unpermute(meta, h, ids, yg, t, tile0, ple):
    n, d = h.shape
    p, layer, proj, pn, gate, tail, extra = ple
    cap = _slot_cap(t)
    prefetch = [meta["off"], meta["ell"], meta["gseg"]]
    imap = lambda s, *_: (s, 0)
    const = lambda s, *_: (0, 0)
    n_out = 2 if tail == "conv_in" else 1
    grid_spec = pltpu.PrefetchScalarGridSpec(
        num_scalar_prefetch=len(prefetch),
        grid=(n // t,),
        in_specs=[pl.BlockSpec((t, LANES), imap),
                  pl.BlockSpec((None, 1, LANES), lambda s, *_: (tile0 + s, 0, 0)),
                  pl.BlockSpec((t, d), imap), pl.BlockSpec(memory_space=pl.ANY),
                  pl.BlockSpec((None, t, p.shape[2]), lambda s, *_: (layer, s, 0)),
                  pl.BlockSpec(proj.shape, const), pl.BlockSpec((1, d), const), pl.BlockSpec(gate.shape, const)]
        + [pl.BlockSpec(e.shape, const) for e in extra],
        out_specs=[pl.BlockSpec((t, d), imap)] * n_out,
        scratch_shapes=[pltpu.VMEM((2, cap, d), BF16), pltpu.SemaphoreType.DMA((2,))],
    )
    return pl.pallas_call(
        functools.partial(_unpermute_kernel, tile0=tile0, tail=tail),
        grid_spec=grid_spec,
        out_shape=[jax.ShapeDtypeStruct((n, d), F32)] * n_out,
        compiler_params=_params("arbitrary"),
        name="moe_unpermute_" + tail,
    )(*prefetch, ids, meta["off_row"], h, yg, p, proj, pn, gate, *extra)


def _moe(groups, ples, wg, wu, wd, layer):
    tiles = [t for g in groups for t in [g[4]] * (g[0].shape[0] // g[4])]
    meta = _moe_meta(jnp.concatenate([g[3].reshape(-1, LANES) for g in groups], axis=0), tiles)
    xg = None
    tile0 = 0
    for h1, xe, ids, _, t in groups:
        xg = _moe_permute(meta, xe, ids, t, tile0, xg)
        tile0 += h1.shape[0] // t
    yg = _moe_ffn(xg, meta["step_expert"], meta["n_used"], wg, wu, wd, FFN_ROWS, layer)
    outs = []
    tile0 = 0
    for (h1, xe, ids, _, t), ple in zip(groups, ples):
        outs.append(_moe_unpermute(meta, h1, ids, yg, t, tile0, ple))
        tile0 += h1.shape[0] // t
    return outs


def _ln_swish(y, w, b):
    mu = jnp.mean(y, axis=-1, keepdims=True)
    var = jnp.mean(jnp.square(y - mu), axis=-1, keepdims=True)
    z = (y - mu) * lax.rsqrt(var + EPS) * w + b
    return z * _sigmoid(z)


def _conv_kernel(a_ref, dw_ref, dwb_ref, lnw_ref, lnb_ref, o_ref, ext, shifted, acc):
    t = pl.program_id(1)
    tl = a_ref.shape[0]
    halo = ext.shape[0] - tl
    sub = 8

    @pl.when(t == 0)
    def _():
        ext[0:halo, :] = jnp.zeros((halo, ext.shape[1]), F32)

    ext[halo:, :] = a_ref[...]
    first = halo - (CONV_WIDTH - 1)
    srows = shifted.shape[1]
    for c0 in range(0, ext.shape[1], LANES):
        cs = slice(c0, c0 + LANES)
        for r in range(1, sub):
            shifted[r - 1] = ext[r:r + srows, cs]
        s = None
        for j in range(CONV_WIDTH):
            r = (first + j) % sub
            base = first + j - r
            win = ext[base:base + tl, cs] if r == 0 else shifted[r - 1, base:base + tl, :]
            term = win * dw_ref[j:j + 1, cs]
            s = term if s is None else s + term
        acc[:, cs] = s
    ext[0:halo, :] = ext[tl:tl + halo, :]
    o_ref[...] = _ln_swish(acc[...] + dwb_ref[...], lnw_ref[...], lnb_ref[...]).astype(BF16)


def _conv_prompt(a, dw, dwb, lnw, lnb, b, l, tl=256):
    d = a.shape[1]
    nt = l // tl
    halo = 32
    row = lambda bi, t: (bi * nt + t, 0)
    const = lambda bi, t: (0, 0)
    return pl.pallas_call(
        _conv_kernel,
        grid=(b, nt),
        in_specs=[pl.BlockSpec((tl, d), row), pl.BlockSpec(dw.shape, const),
                  pl.BlockSpec((1, d), const), pl.BlockSpec((1, d), const), pl.BlockSpec((1, d), const)],
        out_specs=pl.BlockSpec((tl, d), row),
        out_shape=jax.ShapeDtypeStruct((b * l, d), BF16),
        scratch_shapes=[pltpu.VMEM((tl + halo, d), F32), pltpu.VMEM((7, tl + halo - 8, LANES), F32),
                        pltpu.VMEM((tl, d), F32)],
        compiler_params=_params("parallel", "arbitrary"),
        name="conv_prompt",
    )(a, dw, dwb, lnw, lnb)


def _conv_decode_kernel(buf_ref, a_ref, dw_ref, dwb_ref, lnw_ref, lnb_ref, y_ref, new_ref):
    w = CONV_WIDTH - 1
    a = a_ref[...]
    y = a * dw_ref[w:w + 1, :]
    for j in range(w):
        y = y + buf_ref[j] * dw_ref[j:j + 1, :]
    y_ref[...] = _ln_swish(y + dwb_ref[...], lnw_ref[...], lnb_ref[...])
    for j in range(w - 1):
        new_ref[j] = buf_ref[j + 1]
    new_ref[w - 1] = a


def _conv_decode(state, a, dw, dwb, lnw, lnb, bb=8):
    _, b, w, d = state.shape
    row = pl.BlockSpec((bb, d), lambda i: (i, 0))
    sspec = pl.BlockSpec((None, w, bb, d), lambda i: (0, 0, i, 0))
    const = lambda i: (0, 0)
    y, new = pl.pallas_call(
        _conv_decode_kernel,
        grid=(b // bb,),
        in_specs=[sspec, row, pl.BlockSpec(dw.shape, const), pl.BlockSpec((1, d), const),
                  pl.BlockSpec((1, d), const), pl.BlockSpec((1, d), const)],
        out_specs=[row, sspec],
        out_shape=[jax.ShapeDtypeStruct((b, d), F32), jax.ShapeDtypeStruct((1, w, b, d), F32)],
        compiler_params=_params("parallel"),
        name="conv_decode",
    )(jnp.transpose(state, (0, 2, 1, 3)), a, dw, dwb, lnw, lnb)
    return y, jnp.transpose(new, (0, 2, 1, 3))


def _prep_weights(W):
    P = {}
    w_in = W["w_in"][0]
    kpe_cols = w_in[:, -MLA_ROPE:]
    P["w_in"] = jnp.concatenate([w_in[:, :-MLA_ROPE]] + [kpe_cols] * (LANES // MLA_ROPE), axis=1).astype(BF16)
    w_uq = W["w_uq"][0]
    nope = jnp.pad(w_uq[..., :MLA_NOPE], ((0, 0), (0, 0), (0, LANES - MLA_NOPE)))
    P["w_uq_nope"] = nope.reshape(Q_LORA, MLA_HEADS * LANES).astype(BF16)
    P["w_uq_pe"] = w_uq[..., MLA_NOPE:].reshape(Q_LORA, MLA_HEADS * MLA_ROPE).astype(BF16)
    w_ukv = W["w_ukv"][0]
    wk = jnp.transpose(w_ukv[..., :MLA_NOPE], (1, 2, 0))
    P["wk"] = jnp.pad(wk, ((0, 0), (0, LANES - MLA_NOPE), (0, 0))).astype(BF16)
    wv = jnp.transpose(w_ukv[..., MLA_NOPE:], (1, 0, 2))
    P["wv"] = wv.astype(BF16)
    eye = jnp.eye(MLA_HEADS, dtype=F32)
    P["wv_bd"] = (wv[:, :, None, :] * eye[:, None, :, None]).reshape(
        MLA_HEADS * KV_LORA, MLA_HEADS * MLA_V).astype(BF16)
    w_out = W["w_out"][0].astype(BF16)
    P["w_out_r"] = w_out[:RET_HEADS * RET_DV]
    P["w_out_m"] = w_out[RET_HEADS * RET_DV:]
    d = w_out.shape[1]
    for i in range(W["moe_w_group"].shape[0]):
        rw = jnp.concatenate([W["moe_w_group"][i], W["moe_w_router"][i].reshape(d, N_EXPERTS)], axis=1)
        rb = jnp.concatenate([W["moe_b_group"][i], W["moe_b_router"][i].reshape(N_EXPERTS)])
        padc = LANES - rw.shape[1]
        rw = jnp.pad(rw, ((0, 0), (0, padc)))
        rw_hi = rw.astype(BF16)
        P["router_w", i] = (rw_hi, (rw - rw_hi.astype(F32)).astype(BF16))
        P["router_b", i] = jnp.pad(rb, (0, padc)).reshape(1, LANES)
        P["ple_proj", i] = W["ple_proj"][i].astype(BF16)
        P["ple_gate", i] = W["ple_gate"][i].astype(BF16)
    P["pw1"] = W["conv_pw1"][0].astype(BF16)
    P["pw2"] = W["conv_pw2"][0].astype(BF16)
    P["dw"] = jnp.pad(W["conv_dw"][0], ((0, 1), (0, 0)))
    return P


def _vec(v):
    return v.reshape(1, -1)


def _mixer0(x, pos, W, P, tm, past):
    b, l, d = x.shape
    n = b * l
    h = x.reshape(n, d)
    tabs = _rope_tables(pos, RET_DK) + _rope_tables(pos, MLA_ROPE)
    rq, rk, rv, rg, qlat, qpe, c, kpe, kcat = _proj_in(
        h, _vec(W["norm_mix"][0]), P["w_in"], tabs, _vec(W["mla_q_norm"][0]), _vec(W["mla_kv_norm"][0]),
        P["w_uq_nope"], P["w_uq_pe"], P["wk"], 2 * tm if l % (2 * tm) == 0 else tm)
    gnw = _vec(W["ret_gn_w"][0])
    if past is None:
        o_r, s_new = _retention_prompt(rq, rk, rv, rg, gnw, b, l)
        o_m = _attention_prompt(qlat, qpe, kcat, P["wv"], b, l)
        pre = None
    else:
        ret_state, lat_cache, rope_cache, page_table = past
        o_r, s_new = _retention_decode(rq, rk, rv, rg, gnw, ret_state)
        o_m = _attention_decode(qlat, qpe, c, kpe, lat_cache, rope_cache, page_table)
        o_m = o_m.reshape(n, MLA_HEADS * KV_LORA)
        pre = P["wv_bd"]
    routed = _mix_out([o_r, o_m], [P["w_out_r"], P["w_out_m"]], h, _vec(W["norm_ffn"][0]),
                      P["router_w", 0], P["router_b", 0], tm, pre=pre)
    return routed, (c.reshape(1, b, l, KV_LORA), kpe.reshape(1, b, l, MLA_ROPE), s_new)


def _ple_operands(p, layer, W, P):
    n = p.shape[1] * p.shape[2]
    if layer == 0:
        tail, extra = "conv_in", [_vec(W["norm_mix"][1]), P["pw1"], _vec(W["conv_pw1_b"][0])]
    else:
        tail, extra = "final", [_vec(W["norm_final"])]
    return (p.reshape(p.shape[0], n, -1), layer, P["ple_proj", layer], _vec(W["ple_norm"][layer]),
            P["ple_gate", layer], tail, extra)


def _mixer1(h3, a, W, P, tm, b, l, conv_state):
    n, d = h3.shape
    conv_vecs = (_vec(W["conv_dw_b"][0]), _vec(W["conv_ln_w"][0]), _vec(W["conv_ln_b"][0]))
    if conv_state is None:
        yact = _conv_prompt(a, P["dw"], *conv_vecs, b, l)
        new_state = a.reshape(b, l, d)[:, l - (CONV_WIDTH - 1):, :][None]
    else:
        yact, new_state = _conv_decode(conv_state, a, P["dw"], *conv_vecs)
    routed = _mix_out([yact], [P["pw2"]], h3, _vec(W["norm_ffn"][1]),
                      P["router_w", 1], P["router_b", 1], tm, bias=_vec(W["conv_pw2_b"][0]))
    return routed, new_state


def kernel(x_prompt, x_sample, p_prompt, p_sample, cache_mla_latent, cache_mla_rope, state_retention, state_conv,
           page_table, norm_mix, norm_ffn, norm_final, w_in, ret_gn_w, mla_q_norm, mla_kv_norm, w_uq, w_ukv, w_out,
           conv_pw1, conv_pw1_b, conv_dw, conv_dw_b, conv_ln_w, conv_ln_b, conv_pw2, conv_pw2_b,
           moe_w_group, moe_b_group, moe_w_router, moe_b_router, moe_w_gate, moe_w_up, moe_w_down,
           ple_proj, ple_norm, ple_gate):
    W = dict(norm_mix=norm_mix, norm_ffn=norm_ffn, norm_final=norm_final, w_in=w_in, ret_gn_w=ret_gn_w,
             mla_q_norm=mla_q_norm, mla_kv_norm=mla_kv_norm, w_uq=w_uq, w_ukv=w_ukv, w_out=w_out,
             conv_pw1=conv_pw1, conv_pw1_b=conv_pw1_b, conv_dw=conv_dw, conv_dw_b=conv_dw_b, conv_ln_w=conv_ln_w,
             conv_ln_b=conv_ln_b, conv_pw2=conv_pw2, conv_pw2_b=conv_pw2_b, moe_w_group=moe_w_group,
             moe_b_group=moe_b_group, moe_w_router=moe_w_router, moe_b_router=moe_b_router, moe_w_gate=moe_w_gate,
             moe_w_up=moe_w_up, moe_w_down=moe_w_down, ple_proj=ple_proj, ple_norm=ple_norm, ple_gate=ple_gate)
    assert w_in.shape[0] == 1 and conv_pw1.shape[0] == 1, "one retention/attention layer, one conv layer"
    P = _prep_weights(W)
    past_len = page_table.shape[1] * cache_mla_latent.shape[2]
    pos_prompt = jnp.arange(x_prompt.shape[1])
    pos_sample = past_len + jnp.arange(x_sample.shape[1])
    bp, lp, _ = x_prompt.shape
    bs, ls, _ = x_sample.shape
    tm_p = min(512, lp)
    tm_s = bs * ls
    moe_w = lambda i: (moe_w_gate, moe_w_up, moe_w_down, i)
    routed_p, (lat_p, rope_p, ret_p) = _mixer0(x_prompt, pos_prompt, W, P, tm_p, None)
    routed_s, (lat_s, rope_s, ret_s) = _mixer0(
        x_sample, pos_sample, W, P, tm_s, (state_retention, cache_mla_latent, cache_mla_rope, page_table))
    ples = lambda layer: [_ple_operands(p_prompt, layer, W, P), _ple_operands(p_sample, layer, W, P)]
    (h3_p, a_p), (h3_s, a_s) = _moe([routed_p + (tm_p,), routed_s + (tm_s,)], ples(0), *moe_w(0))
    routed_p, conv_p = _mixer1(h3_p, a_p, W, P, tm_p, bp, lp, None)
    routed_s, conv_s = _mixer1(h3_s, a_s, W, P, tm_s, bs, ls, state_conv)
    (y_p,), (y_s,) = _moe([routed_p + (tm_p,), routed_s + (tm_s,)], ples(1), *moe_w(1))
    return (y_p.reshape(x_prompt.shape), y_s.reshape(x_sample.shape), lat_p, rope_p, ret_p, conv_p,
            lat_s, rope_s, ret_s, conv_s)
```

```python
import functools

import jax
import jax.numpy as jnp
from jax import lax
from jax.experimental import pallas as pl
from jax.experimental.pallas import tpu as pltpu

F32 = jnp.float32
BF16 = jnp.bfloat16
I32 = jnp.int32

EPS = 1e-6
ROPE_THETA = 10000.0
RET_HEADS = 8
RET_DK = 64
RET_DV = 64
RET_CHUNK = 128
MLA_HEADS = 8
MLA_NOPE = 64
MLA_ROPE = 32
MLA_V = 64
Q_LORA = 256
KV_LORA = 128
CONV_WIDTH = 31
N_GROUPS = 4
EXPERTS_PER_GROUP = 8
N_EXPERTS = N_GROUPS * EXPERTS_PER_GROUP
LANES = 128
Q_BLOCK = 256
KV_BLOCK = 256
VMEM_LIMIT_BYTES = 56 * 1024 * 1024


def _params(*sem):
    return pltpu.CompilerParams(dimension_semantics=sem, vmem_limit_bytes=VMEM_LIMIT_BYTES)


def _rms(x, w):
    return x * lax.rsqrt(jnp.mean(x * x, axis=-1, keepdims=True) + EPS) * w


def _sigmoid(x):
    return 1.0 / (1.0 + jnp.exp(-x))


def _dot(a, b):
    return jnp.dot(a, b, preferred_element_type=F32)


def _dot_nt(a, b):
    return lax.dot_general(a, b, (((1,), (1,)), ((), ())), preferred_element_type=F32)


def _dot_tn(a, b):
    return lax.dot_general(a, b, (((0,), (0,)), ((), ())), preferred_element_type=F32)


def _rope(x, cos_t, sin_t, period):
    half = period // 2
    lane = lax.broadcasted_iota(I32, x.shape, 1)
    first = (lane % period) < half
    rot = jnp.where(first, pltpu.roll(x, LANES - half, 1), pltpu.roll(x, half, 1))
    return x * cos_t + rot * sin_t


def _rope_tables(pos, period):
    half = period // 2
    inv = ROPE_THETA ** (-2.0 * jnp.arange(half, dtype=F32) / period)
    ang = pos.astype(F32)[:, None] * inv[None, :]
    cos = jnp.cos(ang)
    sin = jnp.sin(ang)
    cos_p = jnp.concatenate([cos, cos], axis=1)
    sin_p = jnp.concatenate([-sin, sin], axis=1)
    reps = LANES // period
    return jnp.tile(cos_p, (1, reps)), jnp.tile(sin_p, (1, reps))


def _proj_in_kernel(x_ref, nw_ref, win_ref, cr_ref, sr_ref, cp_ref, sp_ref, qn_ref, kvn_ref,
                    wuqn_ref, wuqp_ref, wk_ref,
                    rq_ref, rk_ref, rv_ref, rg_ref, qlat_ref, qpe_ref, c_ref, kpe_ref, kcat_ref):
    hn = _rms(x_ref[...], nw_ref[...]).astype(BF16)
    cr, sr, cp, sp = cr_ref[...], sr_ref[...], cp_ref[...], sp_ref[...]
    hk = RET_HEADS * RET_DK
    hv = RET_HEADS * RET_DV

    def mm(lo, width):
        return _dot(hn, win_ref[:, lo:lo + width])

    u = mm(0, hk)
    rq_ref[...] = jnp.concatenate(
        [_rope(u[:, g:g + LANES], cr, sr, RET_DK) for g in range(0, hk, LANES)], axis=1).astype(BF16)
    u = mm(hk, hk)
    rk_ref[...] = jnp.concatenate(
        [_rope(u[:, g:g + LANES], cr, sr, RET_DK) * (RET_DK ** -0.5) for g in range(0, hk, LANES)],
        axis=1).astype(BF16)
    rv_ref[...] = mm(2 * hk, hv).astype(BF16)
    rg_ref[...] = mm(2 * hk + hv, hv)
    off = 2 * hk + 2 * hv
    cqn = _rms(mm(off, Q_LORA), qn_ref[...]).astype(BF16)
    c = _rms(mm(off + Q_LORA, KV_LORA), kvn_ref[...])
    c_ref[...] = c
    kp = _rope(mm(off + Q_LORA + KV_LORA, LANES), cp, sp, MLA_ROPE)
    kpe_ref[...] = kp[:, :MLA_ROPE]
    kcat_ref[...] = jnp.concatenate([c, kp], axis=1).astype(BF16)
    scale = (MLA_NOPE + MLA_ROPE) ** -0.5
    qn = _dot(cqn, wuqn_ref[...])
    for h in range(MLA_HEADS):
        ql = _dot(qn[:, h * LANES:(h + 1) * LANES].astype(BF16), wk_ref[h]) * scale
        qlat_ref[:, h * KV_LORA:(h + 1) * KV_LORA] = ql.astype(BF16)
    qp = _dot(cqn, wuqp_ref[...])
    qpe_ref[...] = jnp.concatenate(
        [_rope(qp[:, g:g + LANES], cp, sp, MLA_ROPE) * scale
         for g in range(0, MLA_HEADS * MLA_ROPE, LANES)], axis=1).astype(BF16)


def _proj_in(x2, nw, win, tabs, qn, kvn, wuqn, wuqp, wk, tm):
    n, d = x2.shape
    cr, sr, cp, sp = tabs
    ltab = cr.shape[0]
    nblk = ltab // tm if ltab >= tm else 0
    tab_rows = tm if nblk else ltab
    tab_map = (lambda i: (i % nblk, 0)) if nblk else (lambda i: (0, 0))
    row = lambda i: (i, 0)
    const = lambda i: (0, 0)
    hk = RET_HEADS * RET_DK
    hv = RET_HEADS * RET_DV
    tab_spec = pl.BlockSpec((tab_rows, LANES), tab_map)
    out_cols = [(hk, BF16), (hk, BF16), (hv, BF16), (hv, F32), (MLA_HEADS * KV_LORA, BF16),
                (MLA_HEADS * MLA_ROPE, BF16), (KV_LORA, F32), (MLA_ROPE, F32), (2 * LANES, BF16)]
    return pl.pallas_call(
        _proj_in_kernel,
        grid=(n // tm,),
        in_specs=[pl.BlockSpec((tm, d), row), pl.BlockSpec((1, d), const),
                  pl.BlockSpec(win.shape, const), tab_spec, tab_spec, tab_spec, tab_spec,
                  pl.BlockSpec((1, Q_LORA), const), pl.BlockSpec((1, KV_LORA), const),
                  pl.BlockSpec(wuqn.shape, const), pl.BlockSpec(wuqp.shape, const),
                  pl.BlockSpec(wk.shape, lambda i: (0, 0, 0))],
        out_specs=[pl.BlockSpec((tm, w), row) for w, _ in out_cols],
        out_shape=[jax.ShapeDtypeStruct((n, w), dt) for w, dt in out_cols],
        compiler_params=_params("parallel"),
        name="proj_in",
    )(x2, nw, win, cr, sr, cp, sp, qn, kvn, wuqn, wuqp, wk)


def _ret_kernel(rq_ref, rk_ref, rv_ref, rg_ref, gnw_ref, decay_ref, qdec_ref, kdec_ref, cdec_ref,
                o_ref, s_out_ref, s_scr):
    ci = pl.program_id(1)

    @pl.when(ci == 0)
    def _():
        s_scr[...] = jnp.zeros_like(s_scr)

    c = rq_ref.shape[0]
    lo = lax.broadcasted_iota(I32, (c, LANES), 1) < RET_DK
    rr = lax.broadcasted_iota(I32, (LANES, LANES), 0) < RET_DK
    cc = lax.broadcasted_iota(I32, (LANES, LANES), 1) < RET_DK
    diag_blocks = rr == cc
    outs = []
    for g in range(RET_HEADS // 2):
        gl = slice(g * LANES, (g + 1) * LANES)
        q2 = rq_ref[:, gl]
        k2 = rk_ref[:, gl]
        v2 = rv_ref[:, gl]
        zero = jnp.zeros_like(q2)
        sc = _dot_nt(jnp.concatenate([jnp.where(lo, q2, zero), jnp.where(lo, zero, q2)], axis=0), k2)
        pv_a = _dot((sc[:c] * decay_ref[2 * g]).astype(BF16), v2)
        pv_b = _dot((sc[c:] * decay_ref[2 * g + 1]).astype(BF16), v2)
        s2 = s_scr[g]
        o = jnp.where(lo, pv_a, pv_b) + _dot(q2, s2.astype(BF16)) * qdec_ref[:, gl]
        kd = (k2.astype(F32) * kdec_ref[:, gl]).astype(BF16)
        s_scr[g] = s2 * cdec_ref[:, gl] + jnp.where(diag_blocks, _dot_tn(kd, v2), 0.0)

        def head_mean(x):
            a = jnp.sum(jnp.where(lo, x, 0.0), axis=-1, keepdims=True)
            b = jnp.sum(jnp.where(lo, 0.0, x), axis=-1, keepdims=True)
            return jnp.where(lo, a, b) * (1.0 / RET_DV)

        mu = head_mean(o)
        var = head_mean(jnp.square(o - mu))
        outs.append((o - mu) * lax.rsqrt(var + EPS))
    gate = rg_ref[...]
    o_ref[...] = (jnp.concatenate(outs, axis=1) * gnw_ref[...] * (gate * _sigmoid(gate))).astype(BF16)

    @pl.when(ci == pl.num_programs(1) - 1)
    def _():
        for g in range(RET_HEADS // 2):
            s2 = s_scr[g]
            s_out_ref[2 * g] = s2[:RET_DK, :RET_DV]
            s_out_ref[2 * g + 1] = s2[RET_DK:, RET_DV:]


def _ret_tables(chunk):
    h = RET_HEADS
    log_g = jnp.log1p(-jnp.exp2(-5.0 - jnp.arange(h, dtype=F32)))
    idx = jnp.arange(chunk, dtype=F32)
    rel = idx[:, None] - idx[None, :]
    decay = jnp.where(rel[None] >= 0, jnp.exp(jnp.maximum(rel, 0.0)[None] * log_g[:, None, None]), 0.0)
    q_dec = jnp.exp((idx[:, None] + 1.0) * log_g[None, :])
    k_dec = jnp.exp((chunk - 1.0 - idx)[:, None] * log_g[None, :])
    c_dec = jnp.exp(chunk * log_g)
    rep = lambda t: jnp.repeat(t, RET_DK, axis=-1)
    return decay, rep(q_dec), rep(k_dec), rep(c_dec[None, :])


def _retention_prompt(rq, rk, rv, rg, gnw, b, l):
    chunk = next((c for c in (2 * RET_CHUNK, RET_CHUNK) if l % c == 0), l)
    decay, qdec, kdec, cdec = _ret_tables(chunk)
    hk = RET_HEADS * RET_DK
    n_chunks = l // chunk
    row = lambda bi, ci: (bi * n_chunks + ci, 0)
    const = lambda bi, ci: (0, 0)
    return pl.pallas_call(
        _ret_kernel,
        grid=(b, n_chunks),
        in_specs=[pl.BlockSpec((chunk, hk), row)] * 4 + [
            pl.BlockSpec((1, hk), const),
            pl.BlockSpec(decay.shape, lambda bi, ci: (0, 0, 0)),
            pl.BlockSpec((chunk, hk), const), pl.BlockSpec((chunk, hk), const),
            pl.BlockSpec((1, hk), const)],
        out_specs=[pl.BlockSpec((chunk, hk), row),
                   pl.BlockSpec((None, None, RET_HEADS, RET_DK, RET_DV), lambda bi, ci: (0, bi, 0, 0, 0))],
        out_shape=[jax.ShapeDtypeStruct((b * l, hk), BF16),
                   jax.ShapeDtypeStruct((1, b, RET_HEADS, RET_DK, RET_DV), F32)],
        scratch_shapes=[pltpu.VMEM((RET_HEADS // 2, 2 * RET_DK, 2 * RET_DV), F32)],
        compiler_params=_params("parallel", "arbitrary"),
        name="retention_prompt",
    )(rq, rk, rv, rg, gnw, decay, qdec, kdec, cdec)


def _ret_decode_kernel(q_ref, k_ref, v_ref, g_ref, gnw_ref, gam_ref, s_ref, o_ref, s_out_ref):
    gam = gam_ref[...]
    q = q_ref[...]
    k = k_ref[...]
    v = v_ref[...]
    qg = q * gam
    cross = jnp.zeros_like(v)
    for d in range(q.shape[0]):
        s_d = s_ref[d]
        cross = cross + qg[d:d + 1, :] * s_d
        s_out_ref[d] = s_d * gam + k[d:d + 1, :] * v
    o = jnp.sum(q * k, axis=0, keepdims=True) * v + cross
    mu = jnp.mean(o, axis=0, keepdims=True)
    var = jnp.mean(jnp.square(o - mu), axis=0, keepdims=True)
    g = g_ref[...]
    o_ref[...] = (o - mu) * lax.rsqrt(var + EPS) * gnw_ref[...] * (g * _sigmoid(g))


def _retention_decode(rq, rk, rv, rg, gnw, state):
    b = rq.shape[0]
    h, dk, dv = RET_HEADS, RET_DK, RET_DV
    gam = jnp.exp(jnp.log1p(-jnp.exp2(-5.0 - jnp.arange(h, dtype=F32)))).reshape(h, 1, 1)
    tr = lambda t: t.astype(F32).T
    vec = lambda d: pl.BlockSpec((d, b), lambda i: (i, 0))
    sspec = pl.BlockSpec((None, None, dk, dv, b), lambda i: (0, i, 0, 0, 0))
    o, s_new = pl.pallas_call(
        _ret_decode_kernel,
        grid=(h,),
        in_specs=[vec(dk), vec(dk), vec(dv), vec(dv), pl.BlockSpec((dv, 1), lambda i: (i, 0)),
                  pl.BlockSpec((None, 1, 1), lambda i: (i, 0, 0)), sspec],
        out_specs=[vec(dv), sspec],
        out_shape=[jax.ShapeDtypeStruct((h * dv, b), F32),
                   jax.ShapeDtypeStruct((1, h, dk, dv, b), F32)],
        compiler_params=_params("parallel"),
        name="retention_decode",
    )(tr(rq), tr(rk), tr(rv), tr(rg), gnw.reshape(h * dv, 1), gam, jnp.transpose(state, (0, 2, 3, 4, 1)))
    return o.T, jnp.transpose(s_new, (0, 4, 1, 2, 3))


def _attn_kernel(qlat_ref, qpe_ref, kcat_ref, wv_ref, o_ref, q_scr, m_scr, acc_scr, s_scr, p_scr, a_scr):
    i = pl.program_id(1)
    lane = lax.broadcasted_iota(I32, (Q_BLOCK, LANES), 1)
    per_group = LANES // MLA_ROPE
    parts = []
    for h in range(MLA_HEADS):
        g, r = divmod(h, per_group)
        pe = qpe_ref[:, g * LANES:(g + 1) * LANES]
        keep = (lane >= r * MLA_ROPE) & (lane < (r + 1) * MLA_ROPE)
        pe = jnp.where(keep, pe, jnp.zeros_like(pe))
        parts.append(jnp.concatenate([qlat_ref[:, h * KV_LORA:(h + 1) * KV_LORA], pe], axis=1))
    q_scr[...] = jnp.concatenate(parts, axis=0)
    rows = MLA_HEADS * Q_BLOCK
    m_scr[...] = jnp.full_like(m_scr, -jnp.inf)
    acc_scr[...] = jnp.zeros_like(acc_scr)
    ones = jnp.ones((KV_BLOCK, LANES), BF16)

    def keys(j):
        return kcat_ref[pl.ds(pl.multiple_of(j * KV_BLOCK, KV_BLOCK), KV_BLOCK), :]

    def scores(j, par):
        s_scr[par] = _dot_nt(q_scr[...], keys(j))

    def softmax(j, par, masked):
        s = s_scr[par]
        if masked:
            qpos = i * Q_BLOCK + lax.broadcasted_iota(I32, (rows, KV_BLOCK), 0) % Q_BLOCK
            kpos = j * KV_BLOCK + lax.broadcasted_iota(I32, (rows, KV_BLOCK), 1)
            s = jnp.where(kpos <= qpos, s, -jnp.inf)
        s0, s1 = s[:, :LANES], s[:, LANES:]
        m_old = m_scr[...]
        m_new = jnp.maximum(m_old, jnp.max(jnp.maximum(s0, s1), axis=-1, keepdims=True))
        a_scr[par] = jnp.exp(m_old - m_new)
        p_scr[par] = jnp.concatenate([jnp.exp(s0 - m_new), jnp.exp(s1 - m_new)], axis=1).astype(BF16)
        m_scr[...] = m_new

    def accumulate(j, par):
        alpha = a_scr[par]
        vext = jnp.concatenate([keys(j)[:, :KV_LORA], ones], axis=1)
        acc_scr[...] = jnp.concatenate([alpha, alpha], axis=1) * acc_scr[...] + _dot(p_scr[par], vext)

    def step(t, par):
        scores(t + 1, 1 - par)
        softmax(t, par, False)
        accumulate(t - 1, 1 - par)

    n_full = (i * Q_BLOCK) // KV_BLOCK
    scores(0, 0)

    @pl.when(n_full >= 1)
    def _():
        scores(1, 1)
        softmax(0, 0, False)

    def body(u, carry):
        step(2 * u + 1, 1)
        step(2 * u + 2, 0)
        return carry

    n_pairs = jnp.maximum(n_full - 1, 0) // 2
    lax.fori_loop(0, n_pairs, body, 0)
    even = n_full % 2 == 0

    @pl.when(even & (n_full >= 2))
    def _():
        step(n_full - 1, 1)

    @pl.when(even)
    def _():
        softmax(n_full, 0, True)

        @pl.when(n_full >= 1)
        def _():
            accumulate(n_full - 1, 1)

        accumulate(n_full, 0)

    @pl.when(jnp.logical_not(even))
    def _():
        softmax(n_full, 1, True)
        accumulate(n_full - 1, 0)
        accumulate(n_full, 1)

    o = acc_scr[:, :KV_LORA] / acc_scr[:, KV_LORA:]
    outs = [_dot(o[h * Q_BLOCK:(h + 1) * Q_BLOCK].astype(BF16), wv_ref[h]) for h in range(MLA_HEADS)]
    o_ref[...] = jnp.concatenate(outs, axis=1).astype(BF16)


def _attention_prompt(qlat, qpe, kcat, wv, b, l):
    nq = l // Q_BLOCK
    rows = MLA_HEADS * Q_BLOCK
    qmap = lambda bi, i: (bi * nq + i, 0)
    return pl.pallas_call(
        _attn_kernel,
        grid=(b, nq),
        in_specs=[pl.BlockSpec((Q_BLOCK, qlat.shape[1]), qmap),
                  pl.BlockSpec((Q_BLOCK, qpe.shape[1]), qmap),
                  pl.BlockSpec((l, kcat.shape[1]), lambda bi, i: (bi, 0)),
                  pl.BlockSpec(wv.shape, lambda bi, i: (0, 0, 0))],
        out_specs=pl.BlockSpec((Q_BLOCK, MLA_HEADS * MLA_V), qmap),
        out_shape=jax.ShapeDtypeStruct((b * l, MLA_HEADS * MLA_V), BF16),
        scratch_shapes=[pltpu.VMEM((rows, 2 * LANES), BF16), pltpu.VMEM((rows, LANES), F32),
                        pltpu.VMEM((rows, 2 * KV_LORA), F32), pltpu.VMEM((2, rows, KV_BLOCK), F32),
                        pltpu.VMEM((2, rows, KV_BLOCK), BF16), pltpu.VMEM((2, rows, LANES), F32)],
        compiler_params=_params("parallel", "arbitrary"),
        name="attention_prompt",
    )(qlat, qpe, kcat, wv)


def _attn_decode_kernel(pt_ref, ql_ref, qp_ref, cn_ref, kn_ref, lat_hbm, rope_hbm, o_ref,
                        kbuf, rbuf, sem):
    b = pl.program_id(0)
    nb = pl.num_programs(0)
    n_pages = pt_ref.shape[1]
    page = lat_hbm.shape[2]

    def copies(seq, slot, p):
        phys = pt_ref[seq, p]
        dst = pl.ds(p * page, page)
        return (pltpu.make_async_copy(lat_hbm.at[0, phys], kbuf.at[slot, dst], sem.at[slot, 0]),
                pltpu.make_async_copy(rope_hbm.at[0, phys], rbuf.at[slot, :, dst], sem.at[slot, 1]))

    def start_all(seq, slot):
        def body(p, carry):
            for cp in copies(seq, slot, p):
                cp.start()
            return carry
        lax.fori_loop(0, n_pages, body, 0, unroll=4)

    def wait_all(slot):
        pltpu.make_async_copy(kbuf.at[slot], kbuf.at[slot], sem.at[slot, 0]).wait()
        pltpu.make_async_copy(rbuf.at[slot], rbuf.at[slot], sem.at[slot, 1]).wait()

    slot = b % 2

    @pl.when(b == 0)
    def _():
        start_all(0, 0)

    @pl.when(b + 1 < nb)
    def _():
        start_all(b + 1, 1 - slot)

    wait_all(slot)
    ql = ql_ref[...]
    qp = qp_ref[...]
    kc = kbuf[slot].astype(BF16)
    kr = rbuf[slot].astype(BF16)
    s_past = _dot_nt(ql, kc) + _dot(qp, kr)
    cn = cn_ref[...].astype(BF16).astype(F32)
    kn = kn_ref[...].astype(BF16).astype(F32)
    s_new = (jnp.sum(ql.astype(F32) * cn, axis=-1, keepdims=True)
             + jnp.sum(qp.astype(F32) * kn, axis=-1, keepdims=True))
    m = jnp.maximum(jnp.max(s_past, axis=-1, keepdims=True), s_new)
    p_past = jnp.exp(s_past - m)
    p_new = jnp.exp(s_new - m)
    denom = jnp.sum(p_past, axis=-1, keepdims=True) + p_new
    o = _dot(p_past.astype(BF16), kc) + p_new.astype(BF16).astype(F32) * cn
    o_ref[...] = o / denom


def _attention_decode(qlat, qpe, c_new, kpe_new, lat_cache, rope_cache, page_table):
    b = qlat.shape[0]
    h = MLA_HEADS
    n_pages = page_table.shape[1]
    page = lat_cache.shape[2]
    past = n_pages * page
    blk = lambda w: pl.BlockSpec((None, h, w), lambda i, pt: (i, 0, 0))
    one = lambda w: pl.BlockSpec((None, 1, w), lambda i, pt: (i, 0, 0))
    grid_spec = pltpu.PrefetchScalarGridSpec(
        num_scalar_prefetch=1,
        grid=(b,),
        in_specs=[blk(KV_LORA), blk(MLA_ROPE), one(KV_LORA), one(MLA_ROPE),
                  pl.BlockSpec(memory_space=pl.ANY), pl.BlockSpec(memory_space=pl.ANY)],
        out_specs=blk(KV_LORA),
        scratch_shapes=[pltpu.VMEM((2, past, KV_LORA), F32), pltpu.VMEM((2, MLA_ROPE, past), F32),
                        pltpu.SemaphoreType.DMA((2, 2))],
    )
    return pl.pallas_call(
        _attn_decode_kernel,
        grid_spec=grid_spec,
        out_shape=jax.ShapeDtypeStruct((b, h, KV_LORA), F32),
        compiler_params=_params("arbitrary"),
        name="attention_decode",
    )(page_table, qlat.reshape(b, h, KV_LORA), qpe.reshape(b, h, MLA_ROPE),
      c_new.reshape(b, 1, KV_LORA), kpe_new.reshape(b, 1, MLA_ROPE), lat_cache,
      jnp.swapaxes(rope_cache, 2, 3))


def _route(logits):
    lane = lax.broadcasted_iota(I32, logits.shape, 1)
    lane_f = lane.astype(F32)
    neg = -jnp.inf
    big = float(LANES)
    gmask = lane < N_GROUPS
    gl = jnp.where(gmask, logits, neg)
    gmax = jnp.max(gl, axis=-1, keepdims=True)
    g_sel = jnp.min(jnp.where(gl == gmax, lane_f, big), axis=-1, keepdims=True)
    gsum = jnp.sum(jnp.where(gmask, jnp.exp(gl - gmax), 0.0), axis=-1, keepdims=True)
    g_w = 1.0 / gsum
    lo = N_GROUPS + EXPERTS_PER_GROUP * g_sel
    emask = (lane_f >= lo) & (lane_f < lo + EXPERTS_PER_GROUP)
    el = jnp.where(emask, logits, neg)
    v1 = jnp.max(el, axis=-1, keepdims=True)
    i1 = jnp.min(jnp.where(el == v1, lane_f, big), axis=-1, keepdims=True)
    el2 = jnp.where(lane_f == i1, neg, el)
    v2 = jnp.max(el2, axis=-1, keepdims=True)
    i2 = jnp.min(jnp.where(el2 == v2, lane_f, big), axis=-1, keepdims=True)
    e2 = jnp.exp(v2 - v1)
    w1 = g_w / (1.0 + e2)
    w2 = g_w * e2 / (1.0 + e2)
    return i1 - N_GROUPS, i2 - N_GROUPS, w1, w2


def _bf16_parts(w):
    hi = w.astype(BF16).astype(F32)
    mid = (w - hi).astype(BF16).astype(F32)
    return hi, mid, w - hi - mid


def _lane_pack(shape, cols):
    lane = lax.broadcasted_iota(I32, shape, 1)
    out = jnp.zeros(shape, F32)
    for k, col in enumerate(cols):
        out = jnp.where(lane == k, col, out)
    return out


def _mix_out_kernel(*refs, n_act, has_pre, has_bias):
    refs = list(refs)
    acts = [refs.pop(0) for _ in range(n_act)]
    ws = [refs.pop(0) for _ in range(n_act)]
    pre = refs.pop(0) if has_pre else None
    bias = refs.pop(0) if has_bias else None
    h_ref, nw_ref, rwh_ref, rwl_ref, rb_ref, h1_ref, xe_ref, rid_ref, cnt_ref = refs
    out = None
    for k, (a_ref, w_ref) in enumerate(zip(acts, ws)):
        a = a_ref[...]
        if has_pre and k == n_act - 1:
            a = _dot(a.astype(BF16), pre[...])
        t = _dot(a.astype(BF16), w_ref[...])
        out = t if out is None else out + t
    if has_bias:
        out = out + bias[...]
    h1 = h_ref[...] + out
    h1_ref[...] = h1
    xn = _rms(h1, nw_ref[...])
    xh = xn.astype(BF16)
    xl = (xn - xh.astype(F32)).astype(BF16)
    tm = xn.shape[0]
    prod = _dot(jnp.concatenate([xh, xl], axis=0), jnp.concatenate([rwh_ref[...], rwl_ref[...]], axis=1))
    logits = (prod[:tm, :LANES] + prod[:tm, LANES:]) + (prod[tm:, :LANES] + prod[tm:, LANES:]) + rb_ref[...]
    e1, e2, w1, w2 = _route(logits)
    meta = _lane_pack(logits.shape, (e1, e2) + _bf16_parts(w1) + _bf16_parts(w2))
    xe_ref[...] = jnp.concatenate([xn, meta], axis=1).astype(BF16)
    rid_ref[...] = _lane_pack(logits.shape, (e1, e2)).astype(I32)
    lane_f = lax.broadcasted_iota(I32, logits.shape, 1).astype(F32)
    hit = jnp.where((lane_f == e1) | (lane_f == e2), 1.0, 0.0)
    cnt_ref[...] = jnp.sum(hit, axis=0, keepdims=True)


def _mix_out(acts, ws, h, nw, rw, rb, tm, pre=None, bias=None):
    n, d = h.shape
    row = lambda i: (i, 0)
    const = lambda i: (0, 0)
    ins = list(acts) + list(ws)
    specs = [pl.BlockSpec((tm, a.shape[1]), row) for a in acts] + [pl.BlockSpec(w.shape, const) for w in ws]
    if pre is not None:
        ins.append(pre)
        specs.append(pl.BlockSpec(pre.shape, const))
    if bias is not None:
        ins.append(bias)
        specs.append(pl.BlockSpec(bias.shape, const))
    ins += [h, nw, rw[0], rw[1], rb]
    specs += [pl.BlockSpec((tm, d), row), pl.BlockSpec((1, d), const),
              pl.BlockSpec(rw[0].shape, const), pl.BlockSpec(rw[1].shape, const), pl.BlockSpec(rb.shape, const)]
    kern = functools.partial(_mix_out_kernel, n_act=len(acts), has_pre=pre is not None,
                             has_bias=bias is not None)
    outs = pl.pallas_call(
        kern,
        grid=(n // tm,),
        in_specs=specs,
        out_specs=[pl.BlockSpec((tm, d), row), pl.BlockSpec((tm, d + LANES), row),
                   pl.BlockSpec((tm, LANES), row), pl.BlockSpec((None, 1, LANES), lambda i: (i, 0, 0))],
        out_shape=[jax.ShapeDtypeStruct((n, d), F32), jax.ShapeDtypeStruct((n, d + LANES), BF16),
                   jax.ShapeDtypeStruct((n, LANES), I32), jax.ShapeDtypeStruct((n // tm, 1, LANES), F32)],
        compiler_params=_params("parallel"),
        name="mix_out_route",
    )(*ins)
    return tuple(outs)


SEG_ALIGN = 16
FFN_ROWS = 512
PERM_ROWS = 256


def _seg_bits(t):
    bits = []
    b = SEG_ALIGN
    while b <= t:
        bits.append(b)
        b *= 2
    return tuple(reversed(bits))


def _slot_cap(t):
    cap = 2 * t + N_EXPERTS * (SEG_ALIGN - 1)
    return -(-cap // PERM_ROWS) * PERM_ROWS


def _moe_meta(counts, tile_tokens):
    cnt = counts[:, :N_EXPERTS].astype(I32)
    ell = ((cnt + SEG_ALIGN - 1) // SEG_ALIGN) * SEG_ALIGN
    off = jnp.cumsum(ell, axis=1) - ell
    tot = jnp.sum(ell, axis=0)
    region = ((tot + FFN_ROWS - 1) // FFN_ROWS) * FFN_ROWS
    gend = jnp.cumsum(region)
    gbase = gend - region
    gseg = gbase[None, :] + jnp.cumsum(ell, axis=0) - ell
    n_pairs = 2 * sum(tile_tokens)
    n_rows_max = n_pairs + len(tile_tokens) * N_EXPERTS * (SEG_ALIGN - 1) + N_EXPERTS * (FFN_ROWS - 1)
    n_steps = -(-n_rows_max // FFN_ROWS)
    starts = jnp.arange(n_steps, dtype=I32) * FFN_ROWS
    step_expert = jnp.minimum(jnp.sum((gend[None, :] <= starts[:, None]).astype(I32), axis=1), N_EXPERTS - 1)
    n_used = (gend[-1] // FFN_ROWS).reshape(1)
    off_f = off.astype(F32)
    return dict(off=off, ell=ell, gseg=gseg, tail_start=gbase + tot, tail_len=region - tot, end=gend[-1:],
                step_expert=step_expert, n_used=n_used, n_rows=n_steps * FFN_ROWS,
                off_col=off_f[:, :, None], off_row=jnp.pad(off_f, ((0, 0), (0, LANES - N_EXPERTS)))[:, None, :])


def _chunk_copies(n, bits, make):
    for b in bits:
        hi = n & (-(2 * b))

        @pl.when((n & b) != 0)
        def _():
            make(hi, b)


def _segment_copies(n, make):
    big = 2 * SEG_ALIGN
    shift = big.bit_length() - 1
    n_big = lax.shift_right_logical(n, shift)

    def body(k, carry):
        make(lax.shift_left(k, shift), big)
        return carry

    lax.fori_loop(0, n_big, body, 0)

    @pl.when((n & SEG_ALIGN) != 0)
    def _():
        make(lax.shift_left(n_big, shift), SEG_ALIGN)


def _pair_slots_rows(ids_t, off_col, t):
    sub = lax.broadcasted_iota(I32, (N_EXPERTS, t), 0).astype(F32)
    e1 = jnp.where(sub == ids_t[0:1, :], 1.0, 0.0)
    e2 = jnp.where(sub == ids_t[1:2, :], 1.0, 0.0)
    before = jnp.where(lax.broadcasted_iota(I32, (t, t), 0) < lax.broadcasted_iota(I32, (t, t), 1), 1.0, 0.0)
    before = before.astype(BF16)
    cum1 = _dot(e1.astype(BF16), before)
    cum2 = _dot(e2.astype(BF16), before)
    cnt1 = jnp.sum(e1, axis=1, keepdims=True)
    slot0 = jnp.sum(e1 * (off_col + cum1), axis=0, keepdims=True)
    slot1 = jnp.sum(e2 * (off_col + cnt1 + cum2), axis=0, keepdims=True)
    return slot0, slot1


def _pair_slots_cols(ids, off_row, t):
    lane = lax.broadcasted_iota(I32, (t, LANES), 1).astype(F32)
    idf = ids.astype(F32)
    e1 = jnp.where(lane == idf[:, 0:1], 1.0, 0.0)
    e2 = jnp.where(lane == idf[:, 1:2], 1.0, 0.0)
    before = jnp.where(lax.broadcasted_iota(I32, (t, t), 1) < lax.broadcasted_iota(I32, (t, t), 0), 1.0, 0.0)
    before = before.astype(BF16)
    cum1 = _dot(before, e1.astype(BF16))
    cum2 = _dot(before, e2.astype(BF16))
    cnt1 = jnp.sum(e1, axis=0, keepdims=True)
    slot0 = jnp.sum(e1 * (off_row + cum1), axis=1, keepdims=True)
    slot1 = jnp.sum(e2 * (off_row + cnt1 + cum2), axis=1, keepdims=True)
    return slot0, slot1


def _permute_kernel(off_ref, ell_ref, gseg_ref, tstart_ref, tlen_ref, end_ref, xe_ref, ids_ref, offcol_ref, *rest,
                    tile0, zero_tiles):
    xg_hbm, xs_scr, zero_scr, sem = rest[-4:]
    s = pl.program_id(0)
    last = pl.num_programs(0) - 1
    t = xe_ref.shape[0]
    cap = xs_scr.shape[1]
    slot = s % 2

    def seg_start(tile, buf):
        def body(e, carry):
            o = off_ref[tile, e]
            g = gseg_ref[tile, e]

            def make(hi, b):
                src = xs_scr.at[buf, pl.ds(pl.multiple_of(o + hi, SEG_ALIGN), b)]
                dst = xg_hbm.at[pl.ds(pl.multiple_of(g + hi, SEG_ALIGN), b)]
                pltpu.make_async_copy(src, dst, sem.at[buf]).start()

            _segment_copies(ell_ref[tile, e], make)
            return carry

        lax.fori_loop(0, N_EXPERTS, body, 0)

    def seg_wait(tile, buf):
        def make(hi, b):
            pltpu.make_async_copy(xs_scr.at[buf, pl.ds(0, b)], xg_hbm.at[pl.ds(0, b)], sem.at[buf]).wait()

        _chunk_copies(off_ref[tile, N_EXPERTS - 1] + ell_ref[tile, N_EXPERTS - 1], _seg_bits(cap), make)

    if zero_tiles is not None:
        zrows = zero_scr.shape[0]

        def zero_fill(g, n, fn):
            def make(hi, b):
                dst = xg_hbm.at[pl.ds(pl.multiple_of(g + hi, SEG_ALIGN), b)]
                fn(pltpu.make_async_copy(zero_scr.at[pl.ds(0, b)], dst, sem.at[2]))

            _chunk_copies(n, _seg_bits(zrows), make)

        def zero_copies(fn):
            def body(e, carry):
                zero_fill(tstart_ref[e], tlen_ref[e], fn)
                for zt in zero_tiles:
                    zero_fill(gseg_ref[zt, e], ell_ref[zt, e], fn)
                return carry

            lax.fori_loop(0, N_EXPERTS, body, 0)

            def trailing(k, carry):
                dst = xg_hbm.at[pl.ds(pl.multiple_of(end_ref[0] + k * zrows, SEG_ALIGN), zrows)]
                fn(pltpu.make_async_copy(zero_scr, dst, sem.at[2]))
                return carry

            n_trailing = lax.shift_right_logical(xg_hbm.shape[0] - end_ref[0], zrows.bit_length() - 1)
            lax.fori_loop(0, n_trailing, trailing, 0)

        @pl.when(s == 0)
        def _():
            zero_scr[...] = jnp.zeros_like(zero_scr)
            zero_copies(lambda cp: cp.start())
            zero_copies(lambda cp: cp.wait())

    @pl.when(s >= 2)
    def _():
        seg_wait(tile0 + s - 2, slot)

    tile = tile0 + s
    ids_t = ids_ref[...].astype(F32).T
    slot0, slot1 = _pair_slots_rows(ids_t, offcol_ref[...], t)
    used = off_ref[tile, N_EXPERTS - 1] + ell_ref[tile, N_EXPERTS - 1]
    xe = xe_ref[...]
    for r0 in range(0, cap, PERM_ROWS):
        @pl.when(r0 < used)
        def _():
            rows = (lax.broadcasted_iota(I32, (PERM_ROWS, t), 0) + r0).astype(F32)
            pm = jnp.where((rows == slot0) | (rows == slot1), 1.0, 0.0).astype(BF16)
            xs_scr[slot, pl.ds(r0, PERM_ROWS), :] = _dot(pm, xe).astype(BF16)

    seg_start(tile, slot)

    @pl.when(s == last)
    def _():
        @pl.when(s >= 1)
        def _():
            seg_wait(tile - 1, 1 - slot)

        seg_wait(tile, slot)


def _moe_permute(meta, xe, ids, t, tile0, xg_prev=None):
    n, de = xe.shape
    cap = _slot_cap(t)
    n_tiles = meta["off"].shape[0]
    zero_tiles = None if xg_prev is not None else tuple(range(tile0 + n // t, n_tiles))
    prefetch = [meta["off"], meta["ell"], meta["gseg"], meta["tail_start"], meta["tail_len"], meta["end"]]
    imap = lambda s, *_: (s, 0)
    in_specs = [pl.BlockSpec((t, de), imap), pl.BlockSpec((t, LANES), imap),
                pl.BlockSpec((None, N_EXPERTS, 1), lambda s, *_: (tile0 + s, 0, 0))]
    ins = [xe, ids, meta["off_col"]]
    aliases = {}
    if xg_prev is not None:
        in_specs.append(pl.BlockSpec(memory_space=pl.ANY))
        ins.append(xg_prev)
        aliases = {len(prefetch) + len(ins) - 1: 0}
    grid_spec = pltpu.PrefetchScalarGridSpec(
        num_scalar_prefetch=len(prefetch),
        grid=(n // t,),
        in_specs=in_specs,
        out_specs=pl.BlockSpec(memory_space=pl.ANY),
        scratch_shapes=[pltpu.VMEM((2, cap, de), BF16), pltpu.VMEM((FFN_ROWS // 2, de), BF16),
                        pltpu.SemaphoreType.DMA((3,))],
    )
    return pl.pallas_call(
        functools.partial(_permute_kernel, tile0=tile0, zero_tiles=zero_tiles),
        grid_spec=grid_spec,
        out_shape=jax.ShapeDtypeStruct((meta["n_rows"], de), BF16),
        input_output_aliases=aliases,
        compiler_params=_params("arbitrary"),
        name="moe_permute",
    )(*prefetch, *ins)


def _ffn_kernel(te_ref, nu_ref, x_ref, wg_ref, wu_ref, wd_ref, o_ref, wg_b, wu_b, wd_b):
    i = pl.program_id(0)
    e = te_ref[i]
    d = o_ref.shape[1]

    @pl.when((i == 0) | (e != te_ref[jnp.maximum(i - 1, 0)]))
    def _():
        wg_b[...] = wg_ref[...].astype(BF16)
        wu_b[...] = wu_ref[...].astype(BF16)
        wd_b[...] = wd_ref[...].astype(BF16)

    @pl.when(i < nu_ref[0])
    def _():
        x = x_ref[:, :d]
        meta = x_ref[:, d:].astype(F32)
        w1 = meta[:, 2:3] + meta[:, 3:4] + meta[:, 4:5]
        w2 = meta[:, 5:6] + meta[:, 6:7] + meta[:, 7:8]
        w = jnp.where(meta[:, 0:1] == e.astype(F32), w1, w2)
        g = _dot(x, wg_b[...])
        u = _dot(x, wu_b[...])
        hmid = (g * _sigmoid(g) * u).astype(BF16)
        o_ref[...] = (_dot(hmid, wd_b[...]) * w).astype(BF16)

    @pl.when(i >= nu_ref[0])
    def _():
        o_ref[...] = jnp.zeros_like(o_ref)


def _moe_ffn(xs, tile_expert, n_used, wg, wu, wd, tr, layer):
    n_rows, d = xs.shape[0], wg.shape[2]
    de = wg.shape[3]
    wmap = lambda i, te, nu: (layer, te[i], 0, 0)
    grid_spec = pltpu.PrefetchScalarGridSpec(
        num_scalar_prefetch=2,
        grid=(n_rows // tr,),
        in_specs=[pl.BlockSpec((tr, xs.shape[1]), lambda i, te, nu: (jnp.maximum(jnp.minimum(i, nu[0] - 1), 0), 0)),
                  pl.BlockSpec((None, None, d, de), wmap),
                  pl.BlockSpec((None, None, d, de), wmap),
                  pl.BlockSpec((None, None, de, d), wmap)],
        out_specs=pl.BlockSpec((tr, d), lambda i, te, nu: (i, 0)),
        scratch_shapes=[pltpu.VMEM((d, de), BF16), pltpu.VMEM((d, de), BF16), pltpu.VMEM((de, d), BF16)],
    )
    return pl.pallas_call(
        _ffn_kernel,
        grid_spec=grid_spec,
        out_shape=jax.ShapeDtypeStruct((n_rows, d), BF16),
        compiler_params=_params("arbitrary"),
        name="moe_ffn",
    )(tile_expert, n_used, xs, wg, wu, wd)


def _unpermute_kernel(off_ref, ell_ref, gseg_ref, ids_ref, offrow_ref, h_ref, yg_hbm, *rest, tile0, tail):
    n_out, n_scr = _TAIL_ARITY[tail if isinstance(tail, str) else tail[0]]
    ple_refs, out_refs, scr = rest[:-n_out - n_scr], rest[-n_out - n_scr:-n_scr], rest[-n_scr:]
    ys_scr, sem = scr[:2]
    s = pl.program_id(0)
    t = h_ref.shape[0]
    cap = ys_scr.shape[1]
    slot = s % 2

    def seg_start(tile, buf):
        def body(e, carry):
            o = off_ref[tile, e]
            g = gseg_ref[tile, e]

            def make(hi, b):
                src = yg_hbm.at[pl.ds(pl.multiple_of(g + hi, SEG_ALIGN), b)]
                dst = ys_scr.at[buf, pl.ds(pl.multiple_of(o + hi, SEG_ALIGN), b)]
                pltpu.make_async_copy(src, dst, sem.at[buf]).start()

            _segment_copies(ell_ref[tile, e], make)
            return carry

        lax.fori_loop(0, N_EXPERTS, body, 0)

    def seg_wait(tile, buf):
        def make(hi, b):
            pltpu.make_async_copy(yg_hbm.at[pl.ds(0, b)], ys_scr.at[buf, pl.ds(0, b)], sem.at[buf]).wait()

        _chunk_copies(off_ref[tile, N_EXPERTS - 1] + ell_ref[tile, N_EXPERTS - 1], _seg_bits(cap), make)

    tile = tile0 + s

    @pl.when(s == 0)
    def _():
        ys_scr[...] = jnp.zeros_like(ys_scr)
        seg_start(tile, slot)

    @pl.when(s + 1 < pl.num_programs(0))
    def _():
        seg_start(tile + 1, 1 - slot)

    seg_wait(tile, slot)
    slot0, slot1 = _pair_slots_cols(ids_ref[...], offrow_ref[...], t)
    cols = lax.broadcasted_iota(I32, (t, cap), 1).astype(F32)
    pu = jnp.where((cols == slot0) | (cols == slot1), 1.0, 0.0).astype(BF16)
    h = h_ref[...] + _dot(pu, ys_scr[slot])
    p_ref, proj_ref, pn_ref, gate_ref = ple_refs[:4]
    pp = _dot(p_ref[...].astype(BF16), proj_ref[...])
    g = _dot(_rms(h, pn_ref[...]).astype(BF16), gate_ref[...])
    h3 = h + pp * _sigmoid(g)
    if tail == "conv_in":
        nm_ref, pw1_ref, pb1_ref = ple_refs[4:]
        h3_ref, a_ref = out_refs
        h3_ref[...] = h3
        d = h3.shape[1]
        hn = _rms(h3, nm_ref[...]).astype(BF16)
        ua = _dot(hn, pw1_ref[:, :d]) + pb1_ref[:, :d]
        ub = _dot(hn, pw1_ref[:, d:]) + pb1_ref[:, d:]
        a_ref[...] = ua * _sigmoid(ub)
    elif tail == "final":
        (nf_ref,) = ple_refs[4:]
        (y_ref,) = out_refs
        y_ref[...] = _rms(h3, nf_ref[...])
    else:
        tiles_per_seq = tail[1]
        nm_ref, pw1_ref, pb1_ref, dw_ref, dwb_ref, lnw_ref, lnb_ref = ple_refs[4:]
        h3_ref, yact_ref, last_ref = out_refs
        ext, shifted, acc = scr[2:]
        h3_ref[...] = h3
        d = h3.shape[1]
        hn = _rms(h3, nm_ref[...]).astype(BF16)
        ua = _dot(hn, pw1_ref[:, :d]) + pb1_ref[:, :d]
        ub = _dot(hn, pw1_ref[:, d:]) + pb1_ref[:, d:]
        a = ua * _sigmoid(ub)

        @pl.when(s % tiles_per_seq == 0)
        def _():
            _conv_reset(ext, CONV_TILE)

        for r0 in range(0, t, CONV_TILE):
            y = _conv_rows(a[r0:r0 + CONV_TILE], dw_ref, ext, shifted, acc)
            yact_ref[r0:r0 + CONV_TILE, :] = _ln_swish(y + dwb_ref[...], lnw_ref[...], lnb_ref[...]).astype(BF16)

        @pl.when(s % tiles_per_seq == tiles_per_seq - 1)
        def _():
            last_ref[...] = a[t - CONV_HALO:, :]


_TAIL_ARITY = {"conv_in": (2, 2), "final": (1, 2), "conv": (3, 5)}


def _moe_unpermute(meta, h, ids, yg, t, tile0, ple):
    n, d = h.shape
    p, layer, proj, pn, gate, tail, extra = ple
    cap = _slot_cap(t)
    prefetch = [meta["off"], meta["ell"], meta["gseg"]]
    imap = lambda s, *_: (s, 0)
    const = lambda s, *_: (0, 0)
    tile_out = pl.BlockSpec((t, d), imap)
    scratch = [pltpu.VMEM((2, cap, d), BF16), pltpu.SemaphoreType.DMA((2,))]
    if tail == "conv_in":
        out_specs, out_shape, name = [tile_out] * 2, [jax.ShapeDtypeStruct((n, d), F32)] * 2, tail
    elif tail == "final":
        out_specs, out_shape, name = [tile_out], [jax.ShapeDtypeStruct((n, d), F32)], tail
    else:
        tiles_per_seq = tail[1]
        n_seq = n // (t * tiles_per_seq)
        out_specs = [tile_out, tile_out,
                     pl.BlockSpec((None, CONV_HALO, d), lambda s, *_: (s // tiles_per_seq, 0, 0))]
        out_shape = [jax.ShapeDtypeStruct((n, d), F32), jax.ShapeDtypeStruct((n, d), BF16),
                     jax.ShapeDtypeStruct((n_seq, CONV_HALO, d), F32)]
        scratch += _conv_scratch(d)
        name = tail[0]
    grid_spec = pltpu.PrefetchScalarGridSpec(
        num_scalar_prefetch=len(prefetch),
        grid=(n // t,),
        in_specs=[pl.BlockSpec((t, LANES), imap),
                  pl.BlockSpec((None, 1, LANES), lambda s, *_: (tile0 + s, 0, 0)),
                  pl.BlockSpec((t, d), imap), pl.BlockSpec(memory_space=pl.ANY),
                  pl.BlockSpec((None, t, p.shape[2]), lambda s, *_: (layer, s, 0)),
                  pl.BlockSpec(proj.shape, const), pl.BlockSpec((1, d), const), pl.BlockSpec(gate.shape, const)]
        + [pl.BlockSpec(e.shape, const) for e in extra],
        out_specs=out_specs,
        scratch_shapes=scratch,
    )
    return pl.pallas_call(
        functools.partial(_unpermute_kernel, tile0=tile0, tail=tail),
        grid_spec=grid_spec,
        out_shape=out_shape,
        compiler_params=_params("arbitrary"),
        name="moe_unpermute_" + name,
    )(*prefetch, ids, meta["off_row"], h, yg, p, proj, pn, gate, *extra)


def _moe(groups, ples, wg, wu, wd, layer):
    tiles = [t for g in groups for t in [g[4]] * (g[0].shape[0] // g[4])]
    meta = _moe_meta(jnp.concatenate([g[3].reshape(-1, LANES) for g in groups], axis=0), tiles)
    xg = None
    tile0 = 0
    for h1, xe, ids, _, t in groups:
        xg = _moe_permute(meta, xe, ids, t, tile0, xg)
        tile0 += h1.shape[0] // t
    yg = _moe_ffn(xg, meta["step_expert"], meta["n_used"], wg, wu, wd, FFN_ROWS, layer)
    outs = []
    tile0 = 0
    for (h1, xe, ids, _, t), ple in zip(groups, ples):
        outs.append(_moe_unpermute(meta, h1, ids, yg, t, tile0, ple))
        tile0 += h1.shape[0] // t
    return outs


def _ln_swish(y, w, b):
    mu = jnp.mean(y, axis=-1, keepdims=True)
    var = jnp.mean(jnp.square(y - mu), axis=-1, keepdims=True)
    z = (y - mu) * lax.rsqrt(var + EPS) * w + b
    return z * _sigmoid(z)


def _conv_kernel(a_ref, dw_ref, dwb_ref, lnw_ref, lnb_ref, o_ref, ext, shifted, acc):
    @pl.when(pl.program_id(1) == 0)
    def _():
        _conv_reset(ext, a_ref.shape[0])

    y = _conv_rows(a_ref[...], dw_ref, ext, shifted, acc)
    o_ref[...] = _ln_swish(y + dwb_ref[...], lnw_ref[...], lnb_ref[...]).astype(BF16)


CONV_HALO = 32
CONV_TILE = 256


def _conv_scratch(d):
    return [pltpu.VMEM((CONV_TILE + CONV_HALO, d), F32), pltpu.VMEM((7, CONV_TILE + CONV_HALO - 8, LANES), F32),
            pltpu.VMEM((CONV_TILE, d), F32)]


def _conv_reset(ext, tl):
    halo = ext.shape[0] - tl
    ext[0:halo, :] = jnp.zeros((halo, ext.shape[1]), F32)


def _conv_rows(a, dw_ref, ext, shifted, acc):
    tl = a.shape[0]
    halo = ext.shape[0] - tl
    sub = 8
    ext[halo:, :] = a
    first = halo - (CONV_WIDTH - 1)
    srows = shifted.shape[1]
    for c0 in range(0, ext.shape[1], LANES):
        cs = slice(c0, c0 + LANES)
        for r in range(1, sub):
            shifted[r - 1] = ext[r:r + srows, cs]
        s = None
        for j in range(CONV_WIDTH):
            r = (first + j) % sub
            base = first + j - r
            win = ext[base:base + tl, cs] if r == 0 else shifted[r - 1, base:base + tl, :]
            term = win * dw_ref[j:j + 1, cs]
            s = term if s is None else s + term
        acc[:, cs] = s
    ext[0:halo, :] = ext[tl:tl + halo, :]
    return acc[...]


def _conv_prompt(a, dw, dwb, lnw, lnb, b, l):
    d = a.shape[1]
    tl = CONV_TILE
    nt = l // tl
    row = lambda bi, t: (bi * nt + t, 0)
    const = lambda bi, t: (0, 0)
    return pl.pallas_call(
        _conv_kernel,
        grid=(b, nt),
        in_specs=[pl.BlockSpec((tl, d), row), pl.BlockSpec(dw.shape, const),
                  pl.BlockSpec((1, d), const), pl.BlockSpec((1, d), const), pl.BlockSpec((1, d), const)],
        out_specs=pl.BlockSpec((tl, d), row),
        out_shape=jax.ShapeDtypeStruct((b * l, d), BF16),
        scratch_shapes=_conv_scratch(d),
        compiler_params=_params("parallel", "arbitrary"),
        name="conv_prompt",
    )(a, dw, dwb, lnw, lnb)


def _conv_decode_kernel(buf_ref, a_ref, dw_ref, dwb_ref, lnw_ref, lnb_ref, y_ref, new_ref):
    w = CONV_WIDTH - 1
    a = a_ref[...]
    y = a * dw_ref[w:w + 1, :]
    for j in range(w):
        y = y + buf_ref[j] * dw_ref[j:j + 1, :]
    y_ref[...] = _ln_swish(y + dwb_ref[...], lnw_ref[...], lnb_ref[...])
    for j in range(w - 1):
        new_ref[j] = buf_ref[j + 1]
    new_ref[w - 1] = a


def _conv_decode(state, a, dw, dwb, lnw, lnb, bb=8):
    _, b, w, d = state.shape
    row = pl.BlockSpec((bb, d), lambda i: (i, 0))
    sspec = pl.BlockSpec((None, w, bb, d), lambda i: (0, 0, i, 0))
    const = lambda i: (0, 0)
    y, new = pl.pallas_call(
        _conv_decode_kernel,
        grid=(b // bb,),
        in_specs=[sspec, row, pl.BlockSpec(dw.shape, const), pl.BlockSpec((1, d), const),
                  pl.BlockSpec((1, d), const), pl.BlockSpec((1, d), const)],
        out_specs=[row, sspec],
        out_shape=[jax.ShapeDtypeStruct((b, d), F32), jax.ShapeDtypeStruct((1, w, b, d), F32)],
        compiler_params=_params("parallel"),
        name="conv_decode",
    )(jnp.transpose(state, (0, 2, 1, 3)), a, dw, dwb, lnw, lnb)
    return y, jnp.transpose(new, (0, 2, 1, 3))


def _prep_weights(W):
    P = {}
    w_in = W["w_in"][0]
    kpe_cols = w_in[:, -MLA_ROPE:]
    P["w_in"] = jnp.concatenate([w_in[:, :-MLA_ROPE]] + [kpe_cols] * (LANES // MLA_ROPE), axis=1).astype(BF16)
    w_uq = W["w_uq"][0]
    nope = jnp.pad(w_uq[..., :MLA_NOPE], ((0, 0), (0, 0), (0, LANES - MLA_NOPE)))
    P["w_uq_nope"] = nope.reshape(Q_LORA, MLA_HEADS * LANES).astype(BF16)
    P["w_uq_pe"] = w_uq[..., MLA_NOPE:].reshape(Q_LORA, MLA_HEADS * MLA_ROPE).astype(BF16)
    w_ukv = W["w_ukv"][0]
    wk = jnp.transpose(w_ukv[..., :MLA_NOPE], (1, 2, 0))
    P["wk"] = jnp.pad(wk, ((0, 0), (0, LANES - MLA_NOPE), (0, 0))).astype(BF16)
    wv = jnp.transpose(w_ukv[..., MLA_NOPE:], (1, 0, 2))
    P["wv"] = wv.astype(BF16)
    eye = jnp.eye(MLA_HEADS, dtype=F32)
    P["wv_bd"] = (wv[:, :, None, :] * eye[:, None, :, None]).reshape(
        MLA_HEADS * KV_LORA, MLA_HEADS * MLA_V).astype(BF16)
    w_out = W["w_out"][0].astype(BF16)
    P["w_out_r"] = w_out[:RET_HEADS * RET_DV]
    P["w_out_m"] = w_out[RET_HEADS * RET_DV:]
    d = w_out.shape[1]
    for i in range(W["moe_w_group"].shape[0]):
        rw = jnp.concatenate([W["moe_w_group"][i], W["moe_w_router"][i].reshape(d, N_EXPERTS)], axis=1)
        rb = jnp.concatenate([W["moe_b_group"][i], W["moe_b_router"][i].reshape(N_EXPERTS)])
        padc = LANES - rw.shape[1]
        rw = jnp.pad(rw, ((0, 0), (0, padc)))
        rw_hi = rw.astype(BF16)
        P["router_w", i] = (rw_hi, (rw - rw_hi.astype(F32)).astype(BF16))
        P["router_b", i] = jnp.pad(rb, (0, padc)).reshape(1, LANES)
        P["ple_proj", i] = W["ple_proj"][i].astype(BF16)
        P["ple_gate", i] = W["ple_gate"][i].astype(BF16)
    P["pw1"] = W["conv_pw1"][0].astype(BF16)
    P["pw2"] = W["conv_pw2"][0].astype(BF16)
    P["dw"] = jnp.pad(W["conv_dw"][0], ((0, 1), (0, 0)))
    return P


def _vec(v):
    return v.reshape(1, -1)


def _mixer0(x, pos, W, P, tm, past):
    b, l, d = x.shape
    n = b * l
    h = x.reshape(n, d)
    tabs = _rope_tables(pos, RET_DK) + _rope_tables(pos, MLA_ROPE)
    rq, rk, rv, rg, qlat, qpe, c, kpe, kcat = _proj_in(
        h, _vec(W["norm_mix"][0]), P["w_in"], tabs, _vec(W["mla_q_norm"][0]), _vec(W["mla_kv_norm"][0]),
        P["w_uq_nope"], P["w_uq_pe"], P["wk"], 2 * tm if l % (2 * tm) == 0 else tm)
    gnw = _vec(W["ret_gn_w"][0])
    if past is None:
        o_r, s_new = _retention_prompt(rq, rk, rv, rg, gnw, b, l)
        o_m = _attention_prompt(qlat, qpe, kcat, P["wv"], b, l)
        pre = None
    else:
        ret_state, lat_cache, rope_cache, page_table = past
        o_r, s_new = _retention_decode(rq, rk, rv, rg, gnw, ret_state)
        o_m = _attention_decode(qlat, qpe, c, kpe, lat_cache, rope_cache, page_table)
        o_m = o_m.reshape(n, MLA_HEADS * KV_LORA)
        pre = P["wv_bd"]
    routed = _mix_out([o_r, o_m], [P["w_out_r"], P["w_out_m"]], h, _vec(W["norm_ffn"][0]),
                      P["router_w", 0], P["router_b", 0], tm, pre=pre)
    return routed, (c.reshape(1, b, l, KV_LORA), kpe.reshape(1, b, l, MLA_ROPE), s_new)


def _ple_operands(p, layer, W, P, tm, whole_sequences):
    n = p.shape[1] * p.shape[2]
    if layer == 0:
        tail, extra = "conv_in", [_vec(W["norm_mix"][1]), P["pw1"], _vec(W["conv_pw1_b"][0])]
        if whole_sequences and p.shape[2] % tm == 0 and tm % CONV_TILE == 0:
            tail = ("conv", p.shape[2] // tm)
            extra += [P["dw"], _vec(W["conv_dw_b"][0]), _vec(W["conv_ln_w"][0]), _vec(W["conv_ln_b"][0])]
    else:
        tail, extra = "final", [_vec(W["norm_final"])]
    return (p.reshape(p.shape[0], n, -1), layer, P["ple_proj", layer], _vec(W["ple_norm"][layer]),
            P["ple_gate", layer], tail, extra)


def _mixer1(outs0, W, P, tm, b, l, conv_state):
    conv_vecs = (_vec(W["conv_dw_b"][0]), _vec(W["conv_ln_w"][0]), _vec(W["conv_ln_b"][0]))
    if len(outs0) == 3:
        h3, yact, last = outs0
        new_state = last[:, CONV_HALO - (CONV_WIDTH - 1):, :][None]
    elif conv_state is None:
        h3, a = outs0
        yact = _conv_prompt(a, P["dw"], *conv_vecs, b, l)
        new_state = a.reshape(b, l, -1)[:, l - (CONV_WIDTH - 1):, :][None]
    else:
        h3, a = outs0
        yact, new_state = _conv_decode(conv_state, a, P["dw"], *conv_vecs)
    routed = _mix_out([yact], [P["pw2"]], h3, _vec(W["norm_ffn"][1]),
                      P["router_w", 1], P["router_b", 1], tm, bias=_vec(W["conv_pw2_b"][0]))
    return routed, new_state


def kernel(x_prompt, x_sample, p_prompt, p_sample, cache_mla_latent, cache_mla_rope, state_retention, state_conv,
           page_table, norm_mix, norm_ffn, norm_final, w_in, ret_gn_w, mla_q_norm, mla_kv_norm, w_uq, w_ukv, w_out,
           conv_pw1, conv_pw1_b, conv_dw, conv_dw_b, conv_ln_w, conv_ln_b, conv_pw2, conv_pw2_b,
           moe_w_group, moe_b_group, moe_w_router, moe_b_router, moe_w_gate, moe_w_up, moe_w_down,
           ple_proj, ple_norm, ple_gate):
    W = dict(norm_mix=norm_mix, norm_ffn=norm_ffn, norm_final=norm_final, w_in=w_in, ret_gn_w=ret_gn_w,
             mla_q_norm=mla_q_norm, mla_kv_norm=mla_kv_norm, w_uq=w_uq, w_ukv=w_ukv, w_out=w_out,
             conv_pw1=conv_pw1, conv_pw1_b=conv_pw1_b, conv_dw=conv_dw, conv_dw_b=conv_dw_b, conv_ln_w=conv_ln_w,
             conv_ln_b=conv_ln_b, conv_pw2=conv_pw2, conv_pw2_b=conv_pw2_b, moe_w_group=moe_w_group,
             moe_b_group=moe_b_group, moe_w_router=moe_w_router, moe_b_router=moe_b_router, moe_w_gate=moe_w_gate,
             moe_w_up=moe_w_up, moe_w_down=moe_w_down, ple_proj=ple_proj, ple_norm=ple_norm, ple_gate=ple_gate)
    assert w_in.shape[0] == 1 and conv_pw1.shape[0] == 1, "one retention/attention layer, one conv layer"
    P = _prep_weights(W)
    past_len = page_table.shape[1] * cache_mla_latent.shape[2]
    pos_prompt = jnp.arange(x_prompt.shape[1])
    pos_sample = past_len + jnp.arange(x_sample.shape[1])
    bp, lp, _ = x_prompt.shape
    bs, ls, _ = x_sample.shape
    tm_p = min(512, lp)
    tm_s = bs * ls
    moe_w = lambda i: (moe_w_gate, moe_w_up, moe_w_down, i)
    routed_p, (lat_p, rope_p, ret_p) = _mixer0(x_prompt, pos_prompt, W, P, tm_p, None)
    routed_s, (lat_s, rope_s, ret_s) = _mixer0(
        x_sample, pos_sample, W, P, tm_s, (state_retention, cache_mla_latent, cache_mla_rope, page_table))
    ples = lambda layer: [_ple_operands(p_prompt, layer, W, P, tm_p, True),
                          _ple_operands(p_sample, layer, W, P, tm_s, False)]
    outs_p, outs_s = _moe([routed_p + (tm_p,), routed_s + (tm_s,)], ples(0), *moe_w(0))
    routed_p, conv_p = _mixer1(outs_p, W, P, tm_p, bp, lp, None)
    routed_s, conv_s = _mixer1(outs_s, W, P, tm_s, bs, ls, state_conv)
    (y_p,), (y_s,) = _moe([routed_p + (tm_p,), routed_s + (tm_s,)], ples(1), *moe_w(1))
    return (y_p.reshape(x_prompt.shape), y_s.reshape(x_sample.shape), lat_p, rope_p, ret_p, conv_p,
            lat_s, rope_s, ret_s, conv_s)
```

```python
import functools

import jax
import jax.numpy as jnp
from jax import lax
from jax.experimental import pallas as pl
from jax.experimental.pallas import tpu as pltpu

F32 = jnp.float32
BF16 = jnp.bfloat16
I32 = jnp.int32

EPS = 1e-6
ROPE_THETA = 10000.0
RET_HEADS = 8
RET_DK = 64
RET_DV = 64
RET_CHUNK = 128
MLA_HEADS = 8
MLA_NOPE = 64
MLA_ROPE = 32
MLA_V = 64
Q_LORA = 256
KV_LORA = 128
CONV_WIDTH = 31
N_GROUPS = 4
EXPERTS_PER_GROUP = 8
N_EXPERTS = N_GROUPS * EXPERTS_PER_GROUP
LANES = 128
Q_BLOCK = 256
KV_BLOCK = 256
VMEM_LIMIT_BYTES = 56 * 1024 * 1024


def _params(*sem):
    return pltpu.CompilerParams(dimension_semantics=sem, vmem_limit_bytes=VMEM_LIMIT_BYTES)


def _rms(x, w):
    return x * lax.rsqrt(jnp.mean(x * x, axis=-1, keepdims=True) + EPS) * w


def _sigmoid(x):
    return 1.0 / (1.0 + jnp.exp(-x))


def _dot(a, b):
    return jnp.dot(a, b, preferred_element_type=F32)


def _dot_nt(a, b):
    return lax.dot_general(a, b, (((1,), (1,)), ((), ())), preferred_element_type=F32)


def _dot_tn(a, b):
    return lax.dot_general(a, b, (((0,), (0,)), ((), ())), preferred_element_type=F32)


def _rope(x, cos_t, sin_t, period):
    half = period // 2
    lane = lax.broadcasted_iota(I32, x.shape, 1)
    first = (lane % period) < half
    rot = jnp.where(first, pltpu.roll(x, LANES - half, 1), pltpu.roll(x, half, 1))
    return x * cos_t + rot * sin_t


def _rope_tables(pos, period):
    half = period // 2
    inv = ROPE_THETA ** (-2.0 * jnp.arange(half, dtype=F32) / period)
    ang = pos.astype(F32)[:, None] * inv[None, :]
    cos = jnp.cos(ang)
    sin = jnp.sin(ang)
    cos_p = jnp.concatenate([cos, cos], axis=1)
    sin_p = jnp.concatenate([-sin, sin], axis=1)
    reps = LANES // period
    return jnp.tile(cos_p, (1, reps)), jnp.tile(sin_p, (1, reps))


def _proj_in_kernel(x_ref, nw_ref, win_ref, cr_ref, sr_ref, cp_ref, sp_ref, qn_ref, kvn_ref,
                    wuqn_ref, wuqp_ref, wk_ref,
                    rq_ref, rk_ref, rv_ref, rg_ref, qlat_ref, qpe_ref, c_ref, kpe_ref, kcat_ref):
    hn = _rms(x_ref[...], nw_ref[...]).astype(BF16)
    cr, sr, cp, sp = cr_ref[...], sr_ref[...], cp_ref[...], sp_ref[...]
    hk = RET_HEADS * RET_DK
    hv = RET_HEADS * RET_DV

    def mm(lo, width):
        return _dot(hn, win_ref[:, lo:lo + width])

    u = mm(0, hk)
    rq_ref[...] = jnp.concatenate(
        [_rope(u[:, g:g + LANES], cr, sr, RET_DK) for g in range(0, hk, LANES)], axis=1).astype(BF16)
    u = mm(hk, hk)
    rk_ref[...] = jnp.concatenate(
        [_rope(u[:, g:g + LANES], cr, sr, RET_DK) * (RET_DK ** -0.5) for g in range(0, hk, LANES)],
        axis=1).astype(BF16)
    rv_ref[...] = mm(2 * hk, hv).astype(BF16)
    rg_ref[...] = mm(2 * hk + hv, hv)
    off = 2 * hk + 2 * hv
    cqn = _rms(mm(off, Q_LORA), qn_ref[...]).astype(BF16)
    c = _rms(mm(off + Q_LORA, KV_LORA), kvn_ref[...])
    c_ref[...] = c
    kp = _rope(mm(off + Q_LORA + KV_LORA, LANES), cp, sp, MLA_ROPE)
    kpe_ref[...] = kp[:, :MLA_ROPE]
    kcat_ref[...] = jnp.concatenate([c, kp], axis=1).astype(BF16)
    scale = (MLA_NOPE + MLA_ROPE) ** -0.5
    qn = _dot(cqn, wuqn_ref[...])
    for h in range(MLA_HEADS):
        ql = _dot(qn[:, h * LANES:(h + 1) * LANES].astype(BF16), wk_ref[h]) * scale
        qlat_ref[:, h * KV_LORA:(h + 1) * KV_LORA] = ql.astype(BF16)
    qp = _dot(cqn, wuqp_ref[...])
    qpe_ref[...] = jnp.concatenate(
        [_rope(qp[:, g:g + LANES], cp, sp, MLA_ROPE) * scale
         for g in range(0, MLA_HEADS * MLA_ROPE, LANES)], axis=1).astype(BF16)


def _proj_in(x2, nw, win, tabs, qn, kvn, wuqn, wuqp, wk, tm):
    n, d = x2.shape
    cr, sr, cp, sp = tabs
    ltab = cr.shape[0]
    nblk = ltab // tm if ltab >= tm else 0
    tab_rows = tm if nblk else ltab
    tab_map = (lambda i: (i % nblk, 0)) if nblk else (lambda i: (0, 0))
    row = lambda i: (i, 0)
    const = lambda i: (0, 0)
    hk = RET_HEADS * RET_DK
    hv = RET_HEADS * RET_DV
    tab_spec = pl.BlockSpec((tab_rows, LANES), tab_map)
    out_cols = [(hk, BF16), (hk, BF16), (hv, BF16), (hv, F32), (MLA_HEADS * KV_LORA, BF16),
                (MLA_HEADS * MLA_ROPE, BF16), (KV_LORA, F32), (MLA_ROPE, F32), (2 * LANES, BF16)]
    return pl.pallas_call(
        _proj_in_kernel,
        grid=(n // tm,),
        in_specs=[pl.BlockSpec((tm, d), row), pl.BlockSpec((1, d), const),
                  pl.BlockSpec(win.shape, const), tab_spec, tab_spec, tab_spec, tab_spec,
                  pl.BlockSpec((1, Q_LORA), const), pl.BlockSpec((1, KV_LORA), const),
                  pl.BlockSpec(wuqn.shape, const), pl.BlockSpec(wuqp.shape, const),
                  pl.BlockSpec(wk.shape, lambda i: (0, 0, 0))],
        out_specs=[pl.BlockSpec((tm, w), row) for w, _ in out_cols],
        out_shape=[jax.ShapeDtypeStruct((n, w), dt) for w, dt in out_cols],
        compiler_params=_params("parallel"),
        name="proj_in",
    )(x2, nw, win, cr, sr, cp, sp, qn, kvn, wuqn, wuqp, wk)


def _ret_kernel(rq_ref, rk_ref, rv_ref, rg_ref, gnw_ref, decay_ref, qdec_ref, kdec_ref, cdec_ref,
                o_ref, s_out_ref, s_scr):
    ci = pl.program_id(1)

    @pl.when(ci == 0)
    def _():
        s_scr[...] = jnp.zeros_like(s_scr)

    c = rq_ref.shape[0]
    lo = lax.broadcasted_iota(I32, (c, LANES), 1) < RET_DK
    rr = lax.broadcasted_iota(I32, (LANES, LANES), 0) < RET_DK
    cc = lax.broadcasted_iota(I32, (LANES, LANES), 1) < RET_DK
    diag_blocks = rr == cc
    outs = []
    for g in range(RET_HEADS // 2):
        gl = slice(g * LANES, (g + 1) * LANES)
        q2 = rq_ref[:, gl]
        k2 = rk_ref[:, gl]
        v2 = rv_ref[:, gl]
        zero = jnp.zeros_like(q2)
        sc = _dot_nt(jnp.concatenate([jnp.where(lo, q2, zero), jnp.where(lo, zero, q2)], axis=0), k2)
        pv_a = _dot((sc[:c] * decay_ref[2 * g]).astype(BF16), v2)
        pv_b = _dot((sc[c:] * decay_ref[2 * g + 1]).astype(BF16), v2)
        s2 = s_scr[g]
        o = jnp.where(lo, pv_a, pv_b) + _dot(q2, s2.astype(BF16)) * qdec_ref[:, gl]
        kd = (k2.astype(F32) * kdec_ref[:, gl]).astype(BF16)
        s_scr[g] = s2 * cdec_ref[:, gl] + jnp.where(diag_blocks, _dot_tn(kd, v2), 0.0)

        def head_mean(x):
            a = jnp.sum(jnp.where(lo, x, 0.0), axis=-1, keepdims=True)
            b = jnp.sum(jnp.where(lo, 0.0, x), axis=-1, keepdims=True)
            return jnp.where(lo, a, b) * (1.0 / RET_DV)

        mu = head_mean(o)
        var = head_mean(jnp.square(o - mu))
        outs.append((o - mu) * lax.rsqrt(var + EPS))
    gate = rg_ref[...]
    o_ref[...] = (jnp.concatenate(outs, axis=1) * gnw_ref[...] * (gate * _sigmoid(gate))).astype(BF16)

    @pl.when(ci == pl.num_programs(1) - 1)
    def _():
        for g in range(RET_HEADS // 2):
            s2 = s_scr[g]
            s_out_ref[2 * g] = s2[:RET_DK, :RET_DV]
            s_out_ref[2 * g + 1] = s2[RET_DK:, RET_DV:]


def _ret_tables(chunk):
    h = RET_HEADS
    log_g = jnp.log1p(-jnp.exp2(-5.0 - jnp.arange(h, dtype=F32)))
    idx = jnp.arange(chunk, dtype=F32)
    rel = idx[:, None] - idx[None, :]
    decay = jnp.where(rel[None] >= 0, jnp.exp(jnp.maximum(rel, 0.0)[None] * log_g[:, None, None]), 0.0)
    q_dec = jnp.exp((idx[:, None] + 1.0) * log_g[None, :])
    k_dec = jnp.exp((chunk - 1.0 - idx)[:, None] * log_g[None, :])
    c_dec = jnp.exp(chunk * log_g)
    rep = lambda t: jnp.repeat(t, RET_DK, axis=-1)
    return decay, rep(q_dec), rep(k_dec), rep(c_dec[None, :])


def _retention_prompt(rq, rk, rv, rg, gnw, b, l):
    chunk = next((c for c in (2 * RET_CHUNK, RET_CHUNK) if l % c == 0), l)
    decay, qdec, kdec, cdec = _ret_tables(chunk)
    hk = RET_HEADS * RET_DK
    n_chunks = l // chunk
    row = lambda bi, ci: (bi * n_chunks + ci, 0)
    const = lambda bi, ci: (0, 0)
    return pl.pallas_call(
        _ret_kernel,
        grid=(b, n_chunks),
        in_specs=[pl.BlockSpec((chunk, hk), row)] * 4 + [
            pl.BlockSpec((1, hk), const),
            pl.BlockSpec(decay.shape, lambda bi, ci: (0, 0, 0)),
            pl.BlockSpec((chunk, hk), const), pl.BlockSpec((chunk, hk), const),
            pl.BlockSpec((1, hk), const)],
        out_specs=[pl.BlockSpec((chunk, hk), row),
                   pl.BlockSpec((None, None, RET_HEADS, RET_DK, RET_DV), lambda bi, ci: (0, bi, 0, 0, 0))],
        out_shape=[jax.ShapeDtypeStruct((b * l, hk), BF16),
                   jax.ShapeDtypeStruct((1, b, RET_HEADS, RET_DK, RET_DV), F32)],
        scratch_shapes=[pltpu.VMEM((RET_HEADS // 2, 2 * RET_DK, 2 * RET_DV), F32)],
        compiler_params=_params("parallel", "arbitrary"),
        name="retention_prompt",
    )(rq, rk, rv, rg, gnw, decay, qdec, kdec, cdec)


def _ret_decode_kernel(q_ref, k_ref, v_ref, g_ref, gnw_ref, gam_ref, s_ref, o_ref, s_out_ref):
    gam = gam_ref[...]
    q = q_ref[...]
    k = k_ref[...]
    v = v_ref[...]
    qg = q * gam
    cross = jnp.zeros_like(v)
    for d in range(q.shape[0]):
        s_d = s_ref[d]
        cross = cross + qg[d:d + 1, :] * s_d
        s_out_ref[d] = s_d * gam + k[d:d + 1, :] * v
    o = jnp.sum(q * k, axis=0, keepdims=True) * v + cross
    mu = jnp.mean(o, axis=0, keepdims=True)
    var = jnp.mean(jnp.square(o - mu), axis=0, keepdims=True)
    g = g_ref[...]
    o_ref[...] = (o - mu) * lax.rsqrt(var + EPS) * gnw_ref[...] * (g * _sigmoid(g))


def _retention_decode(rq, rk, rv, rg, gnw, state):
    b = rq.shape[0]
    h, dk, dv = RET_HEADS, RET_DK, RET_DV
    gam = jnp.exp(jnp.log1p(-jnp.exp2(-5.0 - jnp.arange(h, dtype=F32)))).reshape(h, 1, 1)
    tr = lambda t: t.astype(F32).T
    vec = lambda d: pl.BlockSpec((d, b), lambda i: (i, 0))
    sspec = pl.BlockSpec((None, None, dk, dv, b), lambda i: (0, i, 0, 0, 0))
    o, s_new = pl.pallas_call(
        _ret_decode_kernel,
        grid=(h,),
        in_specs=[vec(dk), vec(dk), vec(dv), vec(dv), pl.BlockSpec((dv, 1), lambda i: (i, 0)),
                  pl.BlockSpec((None, 1, 1), lambda i: (i, 0, 0)), sspec],
        out_specs=[vec(dv), sspec],
        out_shape=[jax.ShapeDtypeStruct((h * dv, b), F32),
                   jax.ShapeDtypeStruct((1, h, dk, dv, b), F32)],
        compiler_params=_params("parallel"),
        name="retention_decode",
    )(tr(rq), tr(rk), tr(rv), tr(rg), gnw.reshape(h * dv, 1), gam, jnp.transpose(state, (0, 2, 3, 4, 1)))
    return o.T, jnp.transpose(s_new, (0, 4, 1, 2, 3))


def _attn_kernel(qlat_ref, qpe_ref, kcat_ref, wv_ref, o_ref, q_scr, m_scr, acc_scr, s_scr, p_scr, a_scr):
    i = pl.program_id(1)
    lane = lax.broadcasted_iota(I32, (Q_BLOCK, LANES), 1)
    per_group = LANES // MLA_ROPE
    parts = []
    for h in range(MLA_HEADS):
        g, r = divmod(h, per_group)
        pe = qpe_ref[:, g * LANES:(g + 1) * LANES]
        keep = (lane >= r * MLA_ROPE) & (lane < (r + 1) * MLA_ROPE)
        pe = jnp.where(keep, pe, jnp.zeros_like(pe))
        parts.append(jnp.concatenate([qlat_ref[:, h * KV_LORA:(h + 1) * KV_LORA], pe], axis=1))
    q_scr[...] = jnp.concatenate(parts, axis=0)
    rows = MLA_HEADS * Q_BLOCK
    m_scr[...] = jnp.full_like(m_scr, -jnp.inf)
    acc_scr[...] = jnp.zeros_like(acc_scr)
    ones = jnp.ones((KV_BLOCK, LANES), BF16)

    def keys(j):
        return kcat_ref[pl.ds(pl.multiple_of(j * KV_BLOCK, KV_BLOCK), KV_BLOCK), :]

    def scores(j, par):
        s_scr[par] = _dot_nt(q_scr[...], keys(j))

    def softmax(j, par, masked):
        s = s_scr[par]
        if masked:
            qpos = i * Q_BLOCK + lax.broadcasted_iota(I32, (rows, KV_BLOCK), 0) % Q_BLOCK
            kpos = j * KV_BLOCK + lax.broadcasted_iota(I32, (rows, KV_BLOCK), 1)
            s = jnp.where(kpos <= qpos, s, -jnp.inf)
        s0, s1 = s[:, :LANES], s[:, LANES:]
        m_old = m_scr[...]
        m_new = jnp.maximum(m_old, jnp.max(jnp.maximum(s0, s1), axis=-1, keepdims=True))
        a_scr[par] = jnp.exp(m_old - m_new)
        p_scr[par] = jnp.concatenate([jnp.exp(s0 - m_new), jnp.exp(s1 - m_new)], axis=1).astype(BF16)
        m_scr[...] = m_new

    def accumulate(j, par):
        alpha = a_scr[par]
        vext = jnp.concatenate([keys(j)[:, :KV_LORA], ones], axis=1)
        acc_scr[...] = jnp.concatenate([alpha, alpha], axis=1) * acc_scr[...] + _dot(p_scr[par], vext)

    def step(t, par):
        scores(t + 1, 1 - par)
        softmax(t, par, False)
        accumulate(t - 1, 1 - par)

    n_full = (i * Q_BLOCK) // KV_BLOCK
    scores(0, 0)

    @pl.when(n_full >= 1)
    def _():
        scores(1, 1)
        softmax(0, 0, False)

    def body(u, carry):
        step(2 * u + 1, 1)
        step(2 * u + 2, 0)
        return carry

    n_pairs = jnp.maximum(n_full - 1, 0) // 2
    lax.fori_loop(0, n_pairs, body, 0)
    even = n_full % 2 == 0

    @pl.when(even & (n_full >= 2))
    def _():
        step(n_full - 1, 1)

    @pl.when(even)
    def _():
        softmax(n_full, 0, True)

        @pl.when(n_full >= 1)
        def _():
            accumulate(n_full - 1, 1)

        accumulate(n_full, 0)

    @pl.when(jnp.logical_not(even))
    def _():
        softmax(n_full, 1, True)
        accumulate(n_full - 1, 0)
        accumulate(n_full, 1)

    o = acc_scr[:, :KV_LORA] / acc_scr[:, KV_LORA:]
    outs = [_dot(o[h * Q_BLOCK:(h + 1) * Q_BLOCK].astype(BF16), wv_ref[h]) for h in range(MLA_HEADS)]
    o_ref[...] = jnp.concatenate(outs, axis=1).astype(BF16)


def _attention_prompt(qlat, qpe, kcat, wv, b, l):
    nq = l // Q_BLOCK
    rows = MLA_HEADS * Q_BLOCK
    qmap = lambda bi, i: (bi * nq + i, 0)
    return pl.pallas_call(
        _attn_kernel,
        grid=(b, nq),
        in_specs=[pl.BlockSpec((Q_BLOCK, qlat.shape[1]), qmap),
                  pl.BlockSpec((Q_BLOCK, qpe.shape[1]), qmap),
                  pl.BlockSpec((l, kcat.shape[1]), lambda bi, i: (bi, 0)),
                  pl.BlockSpec(wv.shape, lambda bi, i: (0, 0, 0))],
        out_specs=pl.BlockSpec((Q_BLOCK, MLA_HEADS * MLA_V), qmap),
        out_shape=jax.ShapeDtypeStruct((b * l, MLA_HEADS * MLA_V), BF16),
        scratch_shapes=[pltpu.VMEM((rows, 2 * LANES), BF16), pltpu.VMEM((rows, LANES), F32),
                        pltpu.VMEM((rows, 2 * KV_LORA), F32), pltpu.VMEM((2, rows, KV_BLOCK), F32),
                        pltpu.VMEM((2, rows, KV_BLOCK), BF16), pltpu.VMEM((2, rows, LANES), F32)],
        compiler_params=_params("parallel", "arbitrary"),
        name="attention_prompt",
    )(qlat, qpe, kcat, wv)


def _attn_decode_kernel(pt_ref, ql_ref, qp_ref, cn_ref, kn_ref, lat_hbm, rope_hbm, o_ref,
                        kbuf, rbuf, sem):
    b = pl.program_id(0)
    nb = pl.num_programs(0)
    n_pages = pt_ref.shape[1]
    page = lat_hbm.shape[2]

    def copies(seq, slot, p):
        phys = pt_ref[seq, p]
        dst = pl.ds(p * page, page)
        return (pltpu.make_async_copy(lat_hbm.at[0, phys], kbuf.at[slot, dst], sem.at[slot, 0]),
                pltpu.make_async_copy(rope_hbm.at[0, phys], rbuf.at[slot, :, dst], sem.at[slot, 1]))

    def start_all(seq, slot):
        def body(p, carry):
            for cp in copies(seq, slot, p):
                cp.start()
            return carry
        lax.fori_loop(0, n_pages, body, 0, unroll=4)

    def wait_all(slot):
        pltpu.make_async_copy(kbuf.at[slot], kbuf.at[slot], sem.at[slot, 0]).wait()
        pltpu.make_async_copy(rbuf.at[slot], rbuf.at[slot], sem.at[slot, 1]).wait()

    slot = b % 2

    @pl.when(b == 0)
    def _():
        start_all(0, 0)

    @pl.when(b + 1 < nb)
    def _():
        start_all(b + 1, 1 - slot)

    wait_all(slot)
    ql = ql_ref[...]
    qp = qp_ref[...]
    kc = kbuf[slot].astype(BF16)
    kr = rbuf[slot].astype(BF16)
    s_past = _dot_nt(ql, kc) + _dot(qp, kr)
    cn = cn_ref[...].astype(BF16).astype(F32)
    kn = kn_ref[...].astype(BF16).astype(F32)
    s_new = (jnp.sum(ql.astype(F32) * cn, axis=-1, keepdims=True)
             + jnp.sum(qp.astype(F32) * kn, axis=-1, keepdims=True))
    m = jnp.maximum(jnp.max(s_past, axis=-1, keepdims=True), s_new)
    p_past = jnp.exp(s_past - m)
    p_new = jnp.exp(s_new - m)
    denom = jnp.sum(p_past, axis=-1, keepdims=True) + p_new
    o = _dot(p_past.astype(BF16), kc) + p_new.astype(BF16).astype(F32) * cn
    o_ref[...] = o / denom


def _attention_decode(qlat, qpe, c_new, kpe_new, lat_cache, rope_cache, page_table):
    b = qlat.shape[0]
    h = MLA_HEADS
    n_pages = page_table.shape[1]
    page = lat_cache.shape[2]
    past = n_pages * page
    blk = lambda w: pl.BlockSpec((None, h, w), lambda i, pt: (i, 0, 0))
    one = lambda w: pl.BlockSpec((None, 1, w), lambda i, pt: (i, 0, 0))
    grid_spec = pltpu.PrefetchScalarGridSpec(
        num_scalar_prefetch=1,
        grid=(b,),
        in_specs=[blk(KV_LORA), blk(MLA_ROPE), one(KV_LORA), one(MLA_ROPE),
                  pl.BlockSpec(memory_space=pl.ANY), pl.BlockSpec(memory_space=pl.ANY)],
        out_specs=blk(KV_LORA),
        scratch_shapes=[pltpu.VMEM((2, past, KV_LORA), F32), pltpu.VMEM((2, MLA_ROPE, past), F32),
                        pltpu.SemaphoreType.DMA((2, 2))],
    )
    return pl.pallas_call(
        _attn_decode_kernel,
        grid_spec=grid_spec,
        out_shape=jax.ShapeDtypeStruct((b, h, KV_LORA), F32),
        compiler_params=_params("arbitrary"),
        name="attention_decode",
    )(page_table, qlat.reshape(b, h, KV_LORA), qpe.reshape(b, h, MLA_ROPE),
      c_new.reshape(b, 1, KV_LORA), kpe_new.reshape(b, 1, MLA_ROPE), lat_cache,
      jnp.swapaxes(rope_cache, 2, 3))


def _route(logits):
    lane = lax.broadcasted_iota(I32, logits.shape, 1)
    lane_f = lane.astype(F32)
    neg = -jnp.inf
    big = float(LANES)
    gmask = lane < N_GROUPS
    gl = jnp.where(gmask, logits, neg)
    gmax = jnp.max(gl, axis=-1, keepdims=True)
    g_sel = jnp.min(jnp.where(gl == gmax, lane_f, big), axis=-1, keepdims=True)
    gsum = jnp.sum(jnp.where(gmask, jnp.exp(gl - gmax), 0.0), axis=-1, keepdims=True)
    g_w = 1.0 / gsum
    lo = N_GROUPS + EXPERTS_PER_GROUP * g_sel
    emask = (lane_f >= lo) & (lane_f < lo + EXPERTS_PER_GROUP)
    el = jnp.where(emask, logits, neg)
    v1 = jnp.max(el, axis=-1, keepdims=True)
    i1 = jnp.min(jnp.where(el == v1, lane_f, big), axis=-1, keepdims=True)
    el2 = jnp.where(lane_f == i1, neg, el)
    v2 = jnp.max(el2, axis=-1, keepdims=True)
    i2 = jnp.min(jnp.where(el2 == v2, lane_f, big), axis=-1, keepdims=True)
    e2 = jnp.exp(v2 - v1)
    w1 = g_w / (1.0 + e2)
    w2 = g_w * e2 / (1.0 + e2)
    return i1 - N_GROUPS, i2 - N_GROUPS, w1, w2


def _bf16_parts(w):
    hi = w.astype(BF16).astype(F32)
    mid = (w - hi).astype(BF16).astype(F32)
    return hi, mid, w - hi - mid


def _lane_pack(shape, cols):
    lane = lax.broadcasted_iota(I32, shape, 1)
    out = jnp.zeros(shape, F32)
    for k, col in enumerate(cols):
        out = jnp.where(lane == k, col, out)
    return out


def _mix_out_kernel(*refs, n_act, has_pre, has_bias):
    refs = list(refs)
    acts = [refs.pop(0) for _ in range(n_act)]
    ws = [refs.pop(0) for _ in range(n_act)]
    pre = refs.pop(0) if has_pre else None
    bias = refs.pop(0) if has_bias else None
    h_ref, nw_ref, rwh_ref, rwl_ref, rb_ref, h1_ref, xe_ref, rid_ref, cnt_ref = refs
    out = None
    for k, (a_ref, w_ref) in enumerate(zip(acts, ws)):
        a = a_ref[...]
        if has_pre and k == n_act - 1:
            a = _dot(a.astype(BF16), pre[...])
        t = _dot(a.astype(BF16), w_ref[...])
        out = t if out is None else out + t
    if has_bias:
        out = out + bias[...]
    h1 = h_ref[...] + out
    h1_ref[...] = h1
    xn = _rms(h1, nw_ref[...])
    xh = xn.astype(BF16)
    xl = (xn - xh.astype(F32)).astype(BF16)
    tm = xn.shape[0]
    prod = _dot(jnp.concatenate([xh, xl], axis=0), jnp.concatenate([rwh_ref[...], rwl_ref[...]], axis=1))
    logits = (prod[:tm, :LANES] + prod[:tm, LANES:]) + (prod[tm:, :LANES] + prod[tm:, LANES:]) + rb_ref[...]
    e1, e2, w1, w2 = _route(logits)
    meta = _lane_pack(logits.shape, (e1, e2) + _bf16_parts(w1) + _bf16_parts(w2))
    xe_ref[...] = jnp.concatenate([xn, meta], axis=1).astype(BF16)
    rid_ref[...] = _lane_pack(logits.shape, (e1, e2)).astype(I32)
    lane_f = lax.broadcasted_iota(I32, logits.shape, 1).astype(F32)
    hit = jnp.where((lane_f == e1) | (lane_f == e2), 1.0, 0.0)
    cnt_ref[...] = jnp.sum(hit, axis=0, keepdims=True)


def _mix_out(acts, ws, h, nw, rw, rb, tm, pre=None, bias=None):
    n, d = h.shape
    row = lambda i: (i, 0)
    const = lambda i: (0, 0)
    ins = list(acts) + list(ws)
    specs = [pl.BlockSpec((tm, a.shape[1]), row) for a in acts] + [pl.BlockSpec(w.shape, const) for w in ws]
    if pre is not None:
        ins.append(pre)
        specs.append(pl.BlockSpec(pre.shape, const))
    if bias is not None:
        ins.append(bias)
        specs.append(pl.BlockSpec(bias.shape, const))
    ins += [h, nw, rw[0], rw[1], rb]
    specs += [pl.BlockSpec((tm, d), row), pl.BlockSpec((1, d), const),
              pl.BlockSpec(rw[0].shape, const), pl.BlockSpec(rw[1].shape, const), pl.BlockSpec(rb.shape, const)]
    kern = functools.partial(_mix_out_kernel, n_act=len(acts), has_pre=pre is not None,
                             has_bias=bias is not None)
    outs = pl.pallas_call(
        kern,
        grid=(n // tm,),
        in_specs=specs,
        out_specs=[pl.BlockSpec((tm, d), row), pl.BlockSpec((tm, d + LANES), row),
                   pl.BlockSpec((tm, LANES), row), pl.BlockSpec((None, 1, LANES), lambda i: (i, 0, 0))],
        out_shape=[jax.ShapeDtypeStruct((n, d), F32), jax.ShapeDtypeStruct((n, d + LANES), BF16),
                   jax.ShapeDtypeStruct((n, LANES), I32), jax.ShapeDtypeStruct((n // tm, 1, LANES), F32)],
        compiler_params=_params("parallel"),
        name="mix_out_route",
    )(*ins)
    return tuple(outs)


SEG_ALIGN = 16
FFN_ROWS = 512
PERM_ROWS = 256
CONV_ROWS = 64


def _seg_bits(t):
    bits = []
    b = SEG_ALIGN
    while b <= t:
        bits.append(b)
        b *= 2
    return tuple(reversed(bits))


def _slot_cap(t):
    cap = 2 * t + N_EXPERTS * (SEG_ALIGN - 1)
    return -(-cap // PERM_ROWS) * PERM_ROWS


def _moe_meta(counts, tile_tokens):
    cnt = counts[:, :N_EXPERTS].astype(I32)
    ell = ((cnt + SEG_ALIGN - 1) // SEG_ALIGN) * SEG_ALIGN
    off = jnp.cumsum(ell, axis=1) - ell
    tot = jnp.sum(ell, axis=0)
    region = ((tot + FFN_ROWS - 1) // FFN_ROWS) * FFN_ROWS
    gend = jnp.cumsum(region)
    gbase = gend - region
    gseg = gbase[None, :] + jnp.cumsum(ell, axis=0) - ell
    n_pairs = 2 * sum(tile_tokens)
    n_rows_max = n_pairs + len(tile_tokens) * N_EXPERTS * (SEG_ALIGN - 1) + N_EXPERTS * (FFN_ROWS - 1)
    n_steps = -(-n_rows_max // FFN_ROWS)
    starts = jnp.arange(n_steps, dtype=I32) * FFN_ROWS
    step_expert = jnp.minimum(jnp.sum((gend[None, :] <= starts[:, None]).astype(I32), axis=1), N_EXPERTS - 1)
    n_used = (gend[-1] // FFN_ROWS).reshape(1)
    off_f = off.astype(F32)
    return dict(off=off, ell=ell, gseg=gseg, tail_start=gbase + tot, tail_len=region - tot, end=gend[-1:],
                step_expert=step_expert, n_used=n_used, n_rows=n_steps * FFN_ROWS,
                off_col=off_f[:, :, None], off_row=jnp.pad(off_f, ((0, 0), (0, LANES - N_EXPERTS)))[:, None, :])


def _chunk_copies(n, bits, make):
    for b in bits:
        hi = n & (-(2 * b))

        @pl.when((n & b) != 0)
        def _():
            make(hi, b)


def _segment_copies(n, make):
    big = 2 * SEG_ALIGN
    shift = big.bit_length() - 1
    n_big = lax.shift_right_logical(n, shift)

    def body(k, carry):
        make(lax.shift_left(k, shift), big)
        return carry

    lax.fori_loop(0, n_big, body, 0)

    @pl.when((n & SEG_ALIGN) != 0)
    def _():
        make(lax.shift_left(n_big, shift), SEG_ALIGN)


def _pair_slots_rows(ids_t, off_col, t):
    sub = lax.broadcasted_iota(I32, (N_EXPERTS, t), 0).astype(F32)
    e1 = jnp.where(sub == ids_t[0:1, :], 1.0, 0.0)
    e2 = jnp.where(sub == ids_t[1:2, :], 1.0, 0.0)
    before = jnp.where(lax.broadcasted_iota(I32, (t, t), 0) < lax.broadcasted_iota(I32, (t, t), 1), 1.0, 0.0)
    before = before.astype(BF16)
    cum1 = _dot(e1.astype(BF16), before)
    cum2 = _dot(e2.astype(BF16), before)
    cnt1 = jnp.sum(e1, axis=1, keepdims=True)
    slot0 = jnp.sum(e1 * (off_col + cum1), axis=0, keepdims=True)
    slot1 = jnp.sum(e2 * (off_col + cnt1 + cum2), axis=0, keepdims=True)
    return slot0, slot1


def _pair_slots_cols(ids, off_row, t):
    lane = lax.broadcasted_iota(I32, (t, LANES), 1).astype(F32)
    idf = ids.astype(F32)
    e1 = jnp.where(lane == idf[:, 0:1], 1.0, 0.0)
    e2 = jnp.where(lane == idf[:, 1:2], 1.0, 0.0)
    before = jnp.where(lax.broadcasted_iota(I32, (t, t), 1) < lax.broadcasted_iota(I32, (t, t), 0), 1.0, 0.0)
    before = before.astype(BF16)
    cum1 = _dot(before, e1.astype(BF16))
    cum2 = _dot(before, e2.astype(BF16))
    cnt1 = jnp.sum(e1, axis=0, keepdims=True)
    slot0 = jnp.sum(e1 * (off_row + cum1), axis=1, keepdims=True)
    slot1 = jnp.sum(e2 * (off_row + cnt1 + cum2), axis=1, keepdims=True)
    return slot0, slot1


def _permute_kernel(off_ref, ell_ref, gseg_ref, tstart_ref, tlen_ref, end_ref, xe_ref, ids_ref, offcol_ref, *rest,
                    tile0, zero_tiles):
    xg_hbm, xs_scr, zero_scr, sem = rest[-4:]
    s = pl.program_id(0)
    last = pl.num_programs(0) - 1
    t = xe_ref.shape[0]
    cap = xs_scr.shape[1]
    slot = s % 2

    def seg_start(tile, buf):
        def body(e, carry):
            o = off_ref[tile, e]
            g = gseg_ref[tile, e]

            def make(hi, b):
                src = xs_scr.at[buf, pl.ds(pl.multiple_of(o + hi, SEG_ALIGN), b)]
                dst = xg_hbm.at[pl.ds(pl.multiple_of(g + hi, SEG_ALIGN), b)]
                pltpu.make_async_copy(src, dst, sem.at[buf]).start()

            _segment_copies(ell_ref[tile, e], make)
            return carry

        lax.fori_loop(0, N_EXPERTS, body, 0)

    def seg_wait(tile, buf):
        def make(hi, b):
            pltpu.make_async_copy(xs_scr.at[buf, pl.ds(0, b)], xg_hbm.at[pl.ds(0, b)], sem.at[buf]).wait()

        _chunk_copies(off_ref[tile, N_EXPERTS - 1] + ell_ref[tile, N_EXPERTS - 1], _seg_bits(cap), make)

    if zero_tiles is not None:
        zrows = zero_scr.shape[0]

        def zero_fill(g, n, fn):
            def make(hi, b):
                dst = xg_hbm.at[pl.ds(pl.multiple_of(g + hi, SEG_ALIGN), b)]
                fn(pltpu.make_async_copy(zero_scr.at[pl.ds(0, b)], dst, sem.at[2]))

            _chunk_copies(n, _seg_bits(zrows), make)

        def zero_copies(fn):
            def body(e, carry):
                zero_fill(tstart_ref[e], tlen_ref[e], fn)
                for zt in zero_tiles:
                    zero_fill(gseg_ref[zt, e], ell_ref[zt, e], fn)
                return carry

            lax.fori_loop(0, N_EXPERTS, body, 0)

            def trailing(k, carry):
                dst = xg_hbm.at[pl.ds(pl.multiple_of(end_ref[0] + k * zrows, SEG_ALIGN), zrows)]
                fn(pltpu.make_async_copy(zero_scr, dst, sem.at[2]))
                return carry

            n_trailing = lax.shift_right_logical(xg_hbm.shape[0] - end_ref[0], zrows.bit_length() - 1)
            lax.fori_loop(0, n_trailing, trailing, 0)

        @pl.when(s == 0)
        def _():
            zero_scr[...] = jnp.zeros_like(zero_scr)
            zero_copies(lambda cp: cp.start())
            zero_copies(lambda cp: cp.wait())

    @pl.when(s >= 2)
    def _():
        seg_wait(tile0 + s - 2, slot)

    tile = tile0 + s
    ids_t = ids_ref[...].astype(F32).T
    slot0, slot1 = _pair_slots_rows(ids_t, offcol_ref[...], t)
    used = off_ref[tile, N_EXPERTS - 1] + ell_ref[tile, N_EXPERTS - 1]
    xe = xe_ref[...]
    for r0 in range(0, cap, PERM_ROWS):
        @pl.when(r0 < used)
        def _():
            rows = (lax.broadcasted_iota(I32, (PERM_ROWS, t), 0) + r0).astype(F32)
            pm = jnp.where((rows == slot0) | (rows == slot1), 1.0, 0.0).astype(BF16)
            xs_scr[slot, pl.ds(r0, PERM_ROWS), :] = _dot(pm, xe).astype(BF16)

    seg_start(tile, slot)

    @pl.when(s == last)
    def _():
        @pl.when(s >= 1)
        def _():
            seg_wait(tile - 1, 1 - slot)

        seg_wait(tile, slot)


def _moe_permute(meta, xe, ids, t, tile0, xg_prev=None):
    n, de = xe.shape
    cap = _slot_cap(t)
    n_tiles = meta["off"].shape[0]
    zero_tiles = None if xg_prev is not None else tuple(range(tile0 + n // t, n_tiles))
    prefetch = [meta["off"], meta["ell"], meta["gseg"], meta["tail_start"], meta["tail_len"], meta["end"]]
    imap = lambda s, *_: (s, 0)
    in_specs = [pl.BlockSpec((t, de), imap), pl.BlockSpec((t, LANES), imap),
                pl.BlockSpec((None, N_EXPERTS, 1), lambda s, *_: (tile0 + s, 0, 0))]
    ins = [xe, ids, meta["off_col"]]
    aliases = {}
    if xg_prev is not None:
        in_specs.append(pl.BlockSpec(memory_space=pl.ANY))
        ins.append(xg_prev)
        aliases = {len(prefetch) + len(ins) - 1: 0}
    grid_spec = pltpu.PrefetchScalarGridSpec(
        num_scalar_prefetch=len(prefetch),
        grid=(n // t,),
        in_specs=in_specs,
        out_specs=pl.BlockSpec(memory_space=pl.ANY),
        scratch_shapes=[pltpu.VMEM((2, cap, de), BF16), pltpu.VMEM((FFN_ROWS // 2, de), BF16),
                        pltpu.SemaphoreType.DMA((3,))],
    )
    return pl.pallas_call(
        functools.partial(_permute_kernel, tile0=tile0, zero_tiles=zero_tiles),
        grid_spec=grid_spec,
        out_shape=jax.ShapeDtypeStruct((meta["n_rows"], de), BF16),
        input_output_aliases=aliases,
        compiler_params=_params("arbitrary"),
        name="moe_permute",
    )(*prefetch, *ins)


def _ffn_kernel(te_ref, nu_ref, x_ref, wg_ref, wu_ref, wd_ref, o_ref, wg_b, wu_b, wd_b):
    i = pl.program_id(0)
    e = te_ref[i]
    d = o_ref.shape[1]

    @pl.when((i == 0) | (e != te_ref[jnp.maximum(i - 1, 0)]))
    def _():
        wg_b[...] = wg_ref[...].astype(BF16)
        wu_b[...] = wu_ref[...].astype(BF16)
        wd_b[...] = wd_ref[...].astype(BF16)

    @pl.when(i < nu_ref[0])
    def _():
        x = x_ref[:, :d]
        meta = x_ref[:, d:].astype(F32)
        w1 = meta[:, 2:3] + meta[:, 3:4] + meta[:, 4:5]
        w2 = meta[:, 5:6] + meta[:, 6:7] + meta[:, 7:8]
        w = jnp.where(meta[:, 0:1] == e.astype(F32), w1, w2)
        g = _dot(x, wg_b[...])
        u = _dot(x, wu_b[...])
        hmid = (g * _sigmoid(g) * u).astype(BF16)
        o_ref[...] = (_dot(hmid, wd_b[...]) * w).astype(BF16)

    @pl.when(i >= nu_ref[0])
    def _():
        o_ref[...] = jnp.zeros_like(o_ref)


def _moe_ffn(xs, tile_expert, n_used, wg, wu, wd, tr, layer):
    n_rows, d = xs.shape[0], wg.shape[2]
    de = wg.shape[3]
    wmap = lambda i, te, nu: (layer, te[i], 0, 0)
    grid_spec = pltpu.PrefetchScalarGridSpec(
        num_scalar_prefetch=2,
        grid=(n_rows // tr,),
        in_specs=[pl.BlockSpec((tr, xs.shape[1]), lambda i, te, nu: (jnp.maximum(jnp.minimum(i, nu[0] - 1), 0), 0)),
                  pl.BlockSpec((None, None, d, de), wmap),
                  pl.BlockSpec((None, None, d, de), wmap),
                  pl.BlockSpec((None, None, de, d), wmap)],
        out_specs=pl.BlockSpec((tr, d), lambda i, te, nu: (i, 0)),
        scratch_shapes=[pltpu.VMEM((d, de), BF16), pltpu.VMEM((d, de), BF16), pltpu.VMEM((de, d), BF16)],
    )
    return pl.pallas_call(
        _ffn_kernel,
        grid_spec=grid_spec,
        out_shape=jax.ShapeDtypeStruct((n_rows, d), BF16),
        compiler_params=_params("arbitrary"),
        name="moe_ffn",
    )(tile_expert, n_used, xs, wg, wu, wd)


def _unpermute_kernel(off_ref, ell_ref, gseg_ref, ids_ref, offrow_ref, h_ref, yg_hbm, *rest, tile0, tail):
    n_out = 2 if tail == "conv_in" else 1
    ple_refs, out_refs, (ys_scr, sem) = rest[:-n_out - 2], rest[-n_out - 2:-2], rest[-2:]
    s = pl.program_id(0)
    t = h_ref.shape[0]
    cap = ys_scr.shape[1]
    slot = s % 2

    def seg_start(tile, buf):
        def body(e, carry):
            o = off_ref[tile, e]
            g = gseg_ref[tile, e]

            def make(hi, b):
                src = yg_hbm.at[pl.ds(pl.multiple_of(g + hi, SEG_ALIGN), b)]
                dst = ys_scr.at[buf, pl.ds(pl.multiple_of(o + hi, SEG_ALIGN), b)]
                pltpu.make_async_copy(src, dst, sem.at[buf]).start()

            _segment_copies(ell_ref[tile, e], make)
            return carry

        lax.fori_loop(0, N_EXPERTS, body, 0)

    def seg_wait(tile, buf):
        def make(hi, b):
            pltpu.make_async_copy(yg_hbm.at[pl.ds(0, b)], ys_scr.at[buf, pl.ds(0, b)], sem.at[buf]).wait()

        _chunk_copies(off_ref[tile, N_EXPERTS - 1] + ell_ref[tile, N_EXPERTS - 1], _seg_bits(cap), make)

    tile = tile0 + s

    @pl.when(s == 0)
    def _():
        ys_scr[...] = jnp.zeros_like(ys_scr)
        seg_start(tile, slot)

    @pl.when(s + 1 < pl.num_programs(0))
    def _():
        seg_start(tile + 1, 1 - slot)

    seg_wait(tile, slot)
    slot0, slot1 = _pair_slots_cols(ids_ref[...], offrow_ref[...], t)
    cols = lax.broadcasted_iota(I32, (t, cap), 1).astype(F32)
    pu = jnp.where((cols == slot0) | (cols == slot1), 1.0, 0.0).astype(BF16)
    h = h_ref[...] + _dot(pu, ys_scr[slot])
    p_ref, proj_ref, pn_ref, gate_ref = ple_refs[:4]
    pp = _dot(p_ref[...].astype(BF16), proj_ref[...])
    g = _dot(_rms(h, pn_ref[...]).astype(BF16), gate_ref[...])
    h3 = h + pp * _sigmoid(g)
    if tail == "conv_in":
        nm_ref, pw1_ref, pb1_ref = ple_refs[4:]
        h3_ref, a_ref = out_refs
        h3_ref[...] = h3
        d = h3.shape[1]
        hn = _rms(h3, nm_ref[...]).astype(BF16)
        ua = _dot(hn, pw1_ref[:, :d]) + pb1_ref[:, :d]
        ub = _dot(hn, pw1_ref[:, d:]) + pb1_ref[:, d:]
        a_ref[...] = ua * _sigmoid(ub)
    else:
        (nf_ref,) = ple_refs[4:]
        (y_ref,) = out_refs
        y_ref[...] = _rms(h3, nf_ref[...])


def _moe_unpermute(meta, h, ids, yg, t, tile0, ple):
    n, d = h.shape
    p, layer, proj, pn, gate, tail, extra = ple
    cap = _slot_cap(t)
    prefetch = [meta["off"], meta["ell"], meta["gseg"]]
    imap = lambda s, *_: (s, 0)
    const = lambda s, *_: (0, 0)
    n_out = 2 if tail == "conv_in" else 1
    grid_spec = pltpu.PrefetchScalarGridSpec(
        num_scalar_prefetch=len(prefetch),
        grid=(n // t,),
        in_specs=[pl.BlockSpec((t, LANES), imap),
                  pl.BlockSpec((None, 1, LANES), lambda s, *_: (tile0 + s, 0, 0)),
                  pl.BlockSpec((t, d), imap), pl.BlockSpec(memory_space=pl.ANY),
                  pl.BlockSpec((None, t, p.shape[2]), lambda s, *_: (layer, s, 0)),
                  pl.BlockSpec(proj.shape, const), pl.BlockSpec((1, d), const), pl.BlockSpec(gate.shape, const)]
        + [pl.BlockSpec(e.shape, const) for e in extra],
        out_specs=[pl.BlockSpec((t, d), imap)] * n_out,
        scratch_shapes=[pltpu.VMEM((2, cap, d), BF16), pltpu.SemaphoreType.DMA((2,))],
    )
    return pl.pallas_call(
        functools.partial(_unpermute_kernel, tile0=tile0, tail=tail),
        grid_spec=grid_spec,
        out_shape=[jax.ShapeDtypeStruct((n, d), F32)] * n_out,
        compiler_params=_params("arbitrary"),
        name="moe_unpermute_" + tail,
    )(*prefetch, ids, meta["off_row"], h, yg, p, proj, pn, gate, *extra)


def _moe(groups, ples, wg, wu, wd, layer):
    tiles = [t for g in groups for t in [g[4]] * (g[0].shape[0] // g[4])]
    meta = _moe_meta(jnp.concatenate([g[3].reshape(-1, LANES) for g in groups], axis=0), tiles)
    xg = None
    tile0 = 0
    for h1, xe, ids, _, t in groups:
        xg = _moe_permute(meta, xe, ids, t, tile0, xg)
        tile0 += h1.shape[0] // t
    yg = _moe_ffn(xg, meta["step_expert"], meta["n_used"], wg, wu, wd, FFN_ROWS, layer)
    outs = []
    tile0 = 0
    for (h1, xe, ids, _, t), ple in zip(groups, ples):
        outs.append(_moe_unpermute(meta, h1, ids, yg, t, tile0, ple))
        tile0 += h1.shape[0] // t
    return outs


def _ln_swish(y, w, b):
    mu = jnp.mean(y, axis=-1, keepdims=True)
    var = jnp.mean(jnp.square(y - mu), axis=-1, keepdims=True)
    z = (y - mu) * lax.rsqrt(var + EPS) * w + b
    return z * _sigmoid(z)


def _conv_kernel(a_ref, dw_ref, dwb_ref, lnw_ref, lnb_ref, o_ref, ext, shifted, acc):
    t = pl.program_id(1)
    tl = a_ref.shape[0]
    halo = ext.shape[0] - tl
    sub = 8

    @pl.when(t == 0)
    def _():
        ext[0:halo, :] = jnp.zeros((halo, ext.shape[1]), F32)

    ext[halo:, :] = a_ref[...]
    first = halo - (CONV_WIDTH - 1)
    srows = shifted.shape[1]
    for c0 in range(0, ext.shape[1], LANES):
        cs = slice(c0, c0 + LANES)
        for r in range(1, sub):
            shifted[r - 1] = ext[r:r + srows, cs]
        wts = [jnp.broadcast_to(dw_ref[j:j + 1, cs], (CONV_ROWS, LANES)) for j in range(CONV_WIDTH)]

        def rows_body(i, carry, cs=cs, wts=wts):
            r0 = pl.multiple_of(i * CONV_ROWS, CONV_ROWS)
            s = None
            for j in range(CONV_WIDTH):
                r = (first + j) % sub
                at = pl.ds(r0 + (first + j - r), CONV_ROWS)
                win = ext[at, cs] if r == 0 else shifted[r - 1, at, :]
                term = win * wts[j]
                s = term if s is None else s + term
            acc[pl.ds(r0, CONV_ROWS), cs] = s
            return carry

        lax.fori_loop(0, tl // CONV_ROWS, rows_body, 0)
    ext[0:halo, :] = ext[tl:tl + halo, :]
    o_ref[...] = _ln_swish(acc[...] + dwb_ref[...], lnw_ref[...], lnb_ref[...]).astype(BF16)


def _conv_prompt(a, dw, dwb, lnw, lnb, b, l, tl=256):
    d = a.shape[1]
    nt = l // tl
    halo = 32
    row = lambda bi, t: (bi * nt + t, 0)
    const = lambda bi, t: (0, 0)
    return pl.pallas_call(
        _conv_kernel,
        grid=(b, nt),
        in_specs=[pl.BlockSpec((tl, d), row), pl.BlockSpec(dw.shape, const),
                  pl.BlockSpec((1, d), const), pl.BlockSpec((1, d), const), pl.BlockSpec((1, d), const)],
        out_specs=pl.BlockSpec((tl, d), row),
        out_shape=jax.ShapeDtypeStruct((b * l, d), BF16),
        scratch_shapes=[pltpu.VMEM((tl + halo, d), F32), pltpu.VMEM((7, tl + halo - 8, LANES), F32),
                        pltpu.VMEM((tl, d), F32)],
        compiler_params=_params("parallel", "arbitrary"),
        name="conv_prompt",
    )(a, dw, dwb, lnw, lnb)


def _conv_decode_kernel(buf_ref, a_ref, dw_ref, dwb_ref, lnw_ref, lnb_ref, y_ref, new_ref):
    w = CONV_WIDTH - 1
    a = a_ref[...]
    y = a * dw_ref[w:w + 1, :]
    for j in range(w):
        y = y + buf_ref[j] * dw_ref[j:j + 1, :]
    y_ref[...] = _ln_swish(y + dwb_ref[...], lnw_ref[...], lnb_ref[...])
    for j in range(w - 1):
        new_ref[j] = buf_ref[j + 1]
    new_ref[w - 1] = a


def _conv_decode(state, a, dw, dwb, lnw, lnb, bb=8):
    _, b, w, d = state.shape
    row = pl.BlockSpec((bb, d), lambda i: (i, 0))
    sspec = pl.BlockSpec((None, w, bb, d), lambda i: (0, 0, i, 0))
    const = lambda i: (0, 0)
    y, new = pl.pallas_call(
        _conv_decode_kernel,
        grid=(b // bb,),
        in_specs=[sspec, row, pl.BlockSpec(dw.shape, const), pl.BlockSpec((1, d), const),
                  pl.BlockSpec((1, d), const), pl.BlockSpec((1, d), const)],
        out_specs=[row, sspec],
        out_shape=[jax.ShapeDtypeStruct((b, d), F32), jax.ShapeDtypeStruct((1, w, b, d), F32)],
        compiler_params=_params("parallel"),
        name="conv_decode",
    )(jnp.transpose(state, (0, 2, 1, 3)), a, dw, dwb, lnw, lnb)
    return y, jnp.transpose(new, (0, 2, 1, 3))


def _prep_weights(W):
    P = {}
    w_in = W["w_in"][0]
    kpe_cols = w_in[:, -MLA_ROPE:]
    P["w_in"] = jnp.concatenate([w_in[:, :-MLA_ROPE]] + [kpe_cols] * (LANES // MLA_ROPE), axis=1).astype(BF16)
    w_uq = W["w_uq"][0]
    nope = jnp.pad(w_uq[..., :MLA_NOPE], ((0, 0), (0, 0), (0, LANES - MLA_NOPE)))
    P["w_uq_nope"] = nope.reshape(Q_LORA, MLA_HEADS * LANES).astype(BF16)
    P["w_uq_pe"] = w_uq[..., MLA_NOPE:].reshape(Q_LORA, MLA_HEADS * MLA_ROPE).astype(BF16)
    w_ukv = W["w_ukv"][0]
    wk = jnp.transpose(w_ukv[..., :MLA_NOPE], (1, 2, 0))
    P["wk"] = jnp.pad(wk, ((0, 0), (0, LANES - MLA_NOPE), (0, 0))).astype(BF16)
    wv = jnp.transpose(w_ukv[..., MLA_NOPE:], (1, 0, 2))
    P["wv"] = wv.astype(BF16)
    eye = jnp.eye(MLA_HEADS, dtype=F32)
    P["wv_bd"] = (wv[:, :, None, :] * eye[:, None, :, None]).reshape(
        MLA_HEADS * KV_LORA, MLA_HEADS * MLA_V).astype(BF16)
    w_out = W["w_out"][0].astype(BF16)
    P["w_out_r"] = w_out[:RET_HEADS * RET_DV]
    P["w_out_m"] = w_out[RET_HEADS * RET_DV:]
    d = w_out.shape[1]
    for i in range(W["moe_w_group"].shape[0]):
        rw = jnp.concatenate([W["moe_w_group"][i], W["moe_w_router"][i].reshape(d, N_EXPERTS)], axis=1)
        rb = jnp.concatenate([W["moe_b_group"][i], W["moe_b_router"][i].reshape(N_EXPERTS)])
        padc = LANES - rw.shape[1]
        rw = jnp.pad(rw, ((0, 0), (0, padc)))
        rw_hi = rw.astype(BF16)
        P["router_w", i] = (rw_hi, (rw - rw_hi.astype(F32)).astype(BF16))
        P["router_b", i] = jnp.pad(rb, (0, padc)).reshape(1, LANES)
        P["ple_proj", i] = W["ple_proj"][i].astype(BF16)
        P["ple_gate", i] = W["ple_gate"][i].astype(BF16)
    P["pw1"] = W["conv_pw1"][0].astype(BF16)
    P["pw2"] = W["conv_pw2"][0].astype(BF16)
    P["dw"] = jnp.pad(W["conv_dw"][0], ((0, 1), (0, 0)))
    return P


def _vec(v):
    return v.reshape(1, -1)


def _mixer0(x, pos, W, P, tm, past):
    b, l, d = x.shape
    n = b * l
    h = x.reshape(n, d)
    tabs = _rope_tables(pos, RET_DK) + _rope_tables(pos, MLA_ROPE)
    rq, rk, rv, rg, qlat, qpe, c, kpe, kcat = _proj_in(
        h, _vec(W["norm_mix"][0]), P["w_in"], tabs, _vec(W["mla_q_norm"][0]), _vec(W["mla_kv_norm"][0]),
        P["w_uq_nope"], P["w_uq_pe"], P["wk"], 2 * tm if l % (2 * tm) == 0 else tm)
    gnw = _vec(W["ret_gn_w"][0])
    if past is None:
        o_r, s_new = _retention_prompt(rq, rk, rv, rg, gnw, b, l)
        o_m = _attention_prompt(qlat, qpe, kcat, P["wv"], b, l)
        pre = None
    else:
        ret_state, lat_cache, rope_cache, page_table = past
        o_r, s_new = _retention_decode(rq, rk, rv, rg, gnw, ret_state)
        o_m = _attention_decode(qlat, qpe, c, kpe, lat_cache, rope_cache, page_table)
        o_m = o_m.reshape(n, MLA_HEADS * KV_LORA)
        pre = P["wv_bd"]
    routed = _mix_out([o_r, o_m], [P["w_out_r"], P["w_out_m"]], h, _vec(W["norm_ffn"][0]),
                      P["router_w", 0], P["router_b", 0], tm, pre=pre)
    return routed, (c.reshape(1, b, l, KV_LORA), kpe.reshape(1, b, l, MLA_ROPE), s_new)


def _ple_operands(p, layer, W, P):
    n = p.shape[1] * p.shape[2]
    if layer == 0:
        tail, extra = "conv_in", [_vec(W["norm_mix"][1]), P["pw1"], _vec(W["conv_pw1_b"][0])]
    else:
        tail, extra = "final", [_vec(W["norm_final"])]
    return (p.reshape(p.shape[0], n, -1), layer, P["ple_proj", layer], _vec(W["ple_norm"][layer]),
            P["ple_gate", layer], tail, extra)


def _mixer1(h3, a, W, P, tm, b, l, conv_state):
    n, d = h3.shape
    conv_vecs = (_vec(W["conv_dw_b"][0]), _vec(W["conv_ln_w"][0]), _vec(W["conv_ln_b"][0]))
    if conv_state is None:
        yact = _conv_prompt(a, P["dw"], *conv_vecs, b, l)
        new_state = a.reshape(b, l, d)[:, l - (CONV_WIDTH - 1):, :][None]
    else:
        yact, new_state = _conv_decode(conv_state, a, P["dw"], *conv_vecs)
    routed = _mix_out([yact], [P["pw2"]], h3, _vec(W["norm_ffn"][1]),
                      P["router_w", 1], P["router_b", 1], tm, bias=_vec(W["conv_pw2_b"][0]))
    return routed, new_state


def kernel(x_prompt, x_sample, p_prompt, p_sample, cache_mla_latent, cache_mla_rope, state_retention, state_conv,
           page_table, norm_mix, norm_ffn, norm_final, w_in, ret_gn_w, mla_q_norm, mla_kv_norm, w_uq, w_ukv, w_out,
           conv_pw1, conv_pw1_b, conv_dw, conv_dw_b, conv_ln_w, conv_ln_b, conv_pw2, conv_pw2_b,
           moe_w_group, moe_b_group, moe_w_router, moe_b_router, moe_w_gate, moe_w_up, moe_w_down,
           ple_proj, ple_norm, ple_gate):
    W = dict(norm_mix=norm_mix, norm_ffn=norm_ffn, norm_final=norm_final, w_in=w_in, ret_gn_w=ret_gn_w,
             mla_q_norm=mla_q_norm, mla_kv_norm=mla_kv_norm, w_uq=w_uq, w_ukv=w_ukv, w_out=w_out,
             conv_pw1=conv_pw1, conv_pw1_b=conv_pw1_b, conv_dw=conv_dw, conv_dw_b=conv_dw_b, conv_ln_w=conv_ln_w,
             conv_ln_b=conv_ln_b, conv_pw2=conv_pw2, conv_pw2_b=conv_pw2_b, moe_w_group=moe_w_group,
             moe_b_group=moe_b_group, moe_w_router=moe_w_router, moe_b_router=moe_b_router, moe_w_gate=moe_w_gate,
             moe_w_up=moe_w_up, moe_w_down=moe_w_down, ple_proj=ple_proj, ple_norm=ple_norm, ple_gate=ple_gate)
    assert w_in.shape[0] == 1 and conv_pw1.shape[0] == 1, "one retention/attention layer, one conv layer"
    P = _prep_weights(W)
    past_len = page_table.shape[1] * cache_mla_latent.shape[2]
    pos_prompt = jnp.arange(x_prompt.shape[1])
    pos_sample = past_len + jnp.arange(x_sample.shape[1])
    bp, lp, _ = x_prompt.shape
    bs, ls, _ = x_sample.shape
    tm_p = min(512, lp)
    tm_s = bs * ls
    moe_w = lambda i: (moe_w_gate, moe_w_up, moe_w_down, i)
    routed_p, (lat_p, rope_p, ret_p) = _mixer0(x_prompt, pos_prompt, W, P, tm_p, None)
    routed_s, (lat_s, rope_s, ret_s) = _mixer0(
        x_sample, pos_sample, W, P, tm_s, (state_retention, cache_mla_latent, cache_mla_rope, page_table))
    ples = lambda layer: [_ple_operands(p_prompt, layer, W, P), _ple_operands(p_sample, layer, W, P)]
    (h3_p, a_p), (h3_s, a_s) = _moe([routed_p + (tm_p,), routed_s + (tm_s,)], ples(0), *moe_w(0))
    routed_p, conv_p = _mixer1(h3_p, a_p, W, P, tm_p, bp, lp, None)
    routed_s, conv_s = _mixer1(h3_s, a_s, W, P, tm_s, bs, ls, state_conv)
    (y_p,), (y_s,) = _moe([routed_p + (tm_p,), routed_s + (tm_s,)], ples(1), *moe_w(1))
    return (y_p.reshape(x_prompt.shape), y_s.reshape(x_sample.shape), lat_p, rope_p, ret_p, conv_p,
            lat_s, rope_s, ret_s, conv_s)
```
